```python
import math
import jax, jax.numpy as jnp
from jax import lax
import numpy as np

D_MODEL = 1024
BATCH = 2
SEQ = 8192
DEPTH = 1
DEC_BATCH = 128
DEC_SEQ = 4
PAST_LEN = 2048
PAGE_SIZE = 128

H_M = 4
DH_M = 128
W_M = H_M * DH_M
CHUNK = 64
F_BIAS_LO = 3.0
F_BIAS_HI = 6.0
H_D = 4
DQK_D = 64
DV_D = 2 * DQK_D
W_D = H_D * DV_D
Q_BLOCK = 128
D_FF = 2816
CONV_W = 3
EPS = 1e-6
IN_SIZES = (W_M, W_M, W_M, W_M, 2 * H_M, H_D * 2 * DQK_D, H_D * 2 * DQK_D, W_D, D_MODEL, D_MODEL)
D_IN = 4 * W_M + 2 * H_M + 2 * H_D * 2 * DQK_D + W_D + 2 * D_MODEL

kernel_name = 'hybrid_mlstm_diffattn_convffn_step'


def rms_norm(x, g):
    xf = x.astype(jnp.float32)
    y = xf * lax.rsqrt(jnp.mean(xf * xf, axis=-1, keepdims=True) + EPS)
    return (y * g.astype(jnp.float32)).astype(x.dtype)


def adaln_terms(c, w_ada, b_ada):
    ada = jnp.einsum('bd,de->be', c, w_ada) + b_ada
    return jnp.split(ada[:, None, :], 6, axis=-1)


def split_columns(z):
    offsets = np.cumsum(np.array(IN_SIZES))[:-1].tolist()
    return jnp.split(z, offsets, axis=-1)


def chunk_len(t):
    return CHUNK if t % CHUNK == 0 else t


def mlstm_scan(q, k, v, i_raw, logf, C0, n0, m0):
    B, T, H, Dh = q.shape
    L = chunk_len(T)
    nc = T // L

    def to_chunks(a):
        return jnp.moveaxis(a.reshape((B, nc, L) + a.shape[2:]), 1, 0)

    causal = jnp.tril(jnp.ones((L, L), dtype=bool))[None, :, :, None]

    def step(carry, blk):
        C, n, m = carry
        qb, kb, vb, ib, fb = blk
        F = jnp.cumsum(fb, axis=1)
        a = ib - F
        m_t = F + jnp.maximum(m[:, None, :], lax.cummax(a, axis=1))
        logD = jnp.where(causal, F[:, :, None, :] + a[:, None, :, :] - m_t[:, :, None, :], -jnp.inf)
        s = jnp.einsum('bthd,bshd->btsh', qb, kb) * jnp.exp(logD)
        inter = jnp.exp(F + m[:, None, :] - m_t)
        num = jnp.einsum('btsh,bshv->bthv', s, vb) + inter[..., None] * jnp.einsum('bhvd,bthd->bthv', C, qb)
        den = jnp.sum(s, axis=2) + inter * jnp.einsum('bhd,bthd->bth', n, qb)
        h = num / jnp.maximum(jnp.abs(den), jnp.exp(-m_t))[..., None]
        m_end = m_t[:, -1]
        F_end = F[:, -1]
        w_end = jnp.exp(F_end[:, None, :] + a - m_end[:, None, :])
        decay = jnp.exp(F_end + m - m_end)
        C_new = decay[..., None, None] * C + jnp.einsum('bsh,bshv,bshd->bhvd', w_end, vb, kb)
        n_new = decay[..., None] * n + jnp.einsum('bsh,bshd->bhd', w_end, kb)
        return (C_new, n_new, m_end), h

    blocks = (to_chunks(q), to_chunks(k), to_chunks(v), to_chunks(i_raw), to_chunks(logf))
    (C, n, m), hs = lax.scan(step, (C0, n0, m0), blocks)
    return jnp.moveaxis(hs, 0, 1).reshape(B, T, H, Dh), C, n, m


def mlstm_branch(mq, mk, mv, mo, mif, b_if, g_norm, C0, n0, m0):
    B, T, _ = mq.shape
    f32 = jnp.float32
    q = mq.astype(f32).reshape(B, T, H_M, DH_M)
    k = mk.astype(f32).reshape(B, T, H_M, DH_M) * (DH_M ** -0.5)
    v = mv.astype(f32).reshape(B, T, H_M, DH_M)
    gates = mif.astype(f32) + b_if.astype(f32)
    i_raw = gates[..., :H_M]
    logf = jax.nn.log_sigmoid(gates[..., H_M:])
    h, C, n, m = mlstm_scan(q, k, v, i_raw, logf, C0.astype(f32), n0.astype(f32), m0.astype(f32))
    h = rms_norm(h, g_norm.reshape(H_M, DH_M)).astype(mq.dtype)
    out = jax.nn.sigmoid(mo) * h.reshape(B, T, W_M)
    return out, C, n, m


def diff_attend(q1, q2, k1, k2, v, mask, lam):
    scale = DQK_D ** -0.5
    s1 = jnp.einsum('bqhd,bkhd->bhqk', q1, k1).astype(jnp.float32) * scale
    s2 = jnp.einsum('bqhd,bkhd->bhqk', q2, k2).astype(jnp.float32) * scale
    s1 = jnp.where(mask, s1, -jnp.inf)
    s2 = jnp.where(mask, s2, -jnp.inf)
    a = jax.nn.softmax(s1, axis=-1) - lam * jax.nn.softmax(s2, axis=-1)
    return jnp.einsum('bhqk,bkhv->bqhv', a.astype(v.dtype), v)


def diff_branch(dq, dk, dv, lq1, lk1, lq2, lk2, g_norm, lam_init, past_k, past_v):
    B, T, _ = dq.shape
    f32 = jnp.float32
    q = dq.reshape(B, T, H_D, 2 * DQK_D)
    k_rows = dk.reshape(B, T, H_D, 2 * DQK_D)
    v_rows = dv.reshape(B, T, H_D, DV_D)
    lam = (jnp.exp(jnp.sum(lq1.astype(f32) * lk1.astype(f32)))
           - jnp.exp(jnp.sum(lq2.astype(f32) * lk2.astype(f32))) + lam_init)
    q1, q2 = q[..., :DQK_D], q[..., DQK_D:]
    if past_k is None:
        k1, k2 = k_rows[..., :DQK_D], k_rows[..., DQK_D:]
        nb = T // Q_BLOCK
        key_pos = jnp.arange(T)

        def block(args):
            bi, qb1, qb2 = args
            q_pos = bi * Q_BLOCK + jnp.arange(Q_BLOCK)
            return diff_attend(qb1, qb2, k1, k2, v_rows, key_pos[None, :] <= q_pos[:, None], lam)

        def to_blocks(a):
            return jnp.moveaxis(a.reshape(B, nb, Q_BLOCK, H_D, DQK_D), 1, 0)

        o = lax.map(block, (jnp.arange(nb), to_blocks(q1), to_blocks(q2)))
        o = jnp.moveaxis(o, 0, 1).reshape(B, T, H_D, DV_D)
    else:
        past_len = past_k.shape[1]
        k_all = jnp.concatenate([past_k.astype(k_rows.dtype), k_rows], axis=1)
        v_all = jnp.concatenate([past_v.astype(v_rows.dtype), v_rows], axis=1)
        mask = jnp.arange(past_len + T)[None, :] <= past_len + jnp.arange(T)[:, None]
        o = diff_attend(q1, q2, k_all[..., :DQK_D], k_all[..., DQK_D:], v_all, mask, lam)
    o = rms_norm(o, g_norm.reshape(H_D, DV_D)) * (1.0 - lam_init)
    return o.reshape(B, T, W_D), k_rows, v_rows


def conv_ffn(h, w_up, conv_w, conv_b, w_down, conv_state):
    T = h.shape[1]
    u = jnp.einsum('btd,df->btf', h, w_up)
    a, b = u[..., :D_FF], u[..., D_FF:]
    a_ext = jnp.concatenate([conv_state.astype(a.dtype), a], axis=1)
    conv = conv_b
    for j in range(CONV_W):
        conv = conv + conv_w[j] * a_ext[:, j:j + T]
    out = jnp.einsum('btf,fd->btd', jax.nn.gelu(conv, approximate=True) * b, w_down)
    return out, a_ext[:, -(CONV_W - 1):]


def layer_forward(x, c, p, lam_init, past_k, past_v, C0, n0, m0, conv0):
    shift1, scale1, gate1, shift2, scale2, gate2 = adaln_terms(c, p['w_ada'], p['b_ada'])
    h = rms_norm(x, p['g_pre_mix']) * (1.0 + scale1) + shift1
    z = jnp.einsum('btd,de->bte', h, p['w_in'])
    mq, mk, mv, mo, mif, dq, dk, dv, gm, gd = split_columns(z)
    hm, C, n, m = mlstm_branch(mq, mk, mv, mo, mif, p['b_if'], p['g_mlstm'], C0, n0, m0)
    hd, k_rows, v_rows = diff_branch(dq, dk, dv, p['lq1'], p['lk1'], p['lq2'], p['lk2'],
                                     p['g_diff'], lam_init, past_k, past_v)
    merged = (jax.nn.sigmoid(gm) * jnp.einsum('btw,wd->btd', hm, p['w_proj_m'])
              + jax.nn.sigmoid(gd) * jnp.einsum('btw,wd->btd', hd, p['w_proj_d']))
    y = jnp.einsum('btd,de->bte', merged, p['w_out'])
    x = x + gate1 * rms_norm(y, p['g_post_mix'])
    h2 = rms_norm(x, p['g_pre_ffn']) * (1.0 + scale2) + shift2
    f, conv_new = conv_ffn(h2, p['w_up'], p['conv_w'], p['conv_b'], p['w_down'], conv0)
    x = x + gate2 * rms_norm(f, p['g_post_ffn'])
    return x, k_rows, v_rows, C, n, m, conv_new


def setup_inputs(seed: int = 0) -> dict:
    key = jax.random.key(seed)
    ks = jax.random.split(key, 40)
    n_pages = PAST_LEN // PAGE_SIZE
    n_used = DEC_BATCH * n_pages
    n_pool = (n_used * 5) // 4

    def nrm(k, shape, scale=1.0):
        return scale * jax.random.normal(k, shape, jnp.float32)

    def gain(k, shape):
        return 1.0 + 0.05 * jax.random.normal(k, shape, jnp.float32)

    page_table = jax.random.permutation(ks[6], n_pool)[:n_used].reshape(DEC_BATCH, n_pages).astype(jnp.int32)
    f_bias = jnp.broadcast_to(jnp.linspace(F_BIAS_LO, F_BIAS_HI, H_M, dtype=jnp.float32), (DEPTH, H_M))
    b_if = jnp.concatenate([nrm(ks[16], (DEPTH, H_M), 0.1), f_bias + nrm(ks[17], (DEPTH, H_M), 0.1)], axis=-1)
    return {
        'x_prompt': nrm(ks[0], (BATCH, SEQ, D_MODEL)),
        'x_sample': nrm(ks[1], (DEC_BATCH, DEC_SEQ, D_MODEL)),
        'c_prompt': nrm(ks[2], (BATCH, D_MODEL)),
        'c_sample': nrm(ks[3], (DEC_BATCH, D_MODEL)),
        'cache_k': nrm(ks[4], (DEPTH, n_pool, PAGE_SIZE, H_D, 2 * DQK_D)),
        'cache_v': nrm(ks[5], (DEPTH, n_pool, PAGE_SIZE, H_D, DV_D)),
        'page_table': page_table,
        'state_C': nrm(ks[7], (DEPTH, DEC_BATCH, H_M, DH_M, DH_M), 0.1),
        'state_n': nrm(ks[8], (DEPTH, DEC_BATCH, H_M, DH_M), 0.1),
        'state_m': jax.random.uniform(ks[9], (DEPTH, DEC_BATCH, H_M), jnp.float32, 0.0, 4.0),
        'state_conv': nrm(ks[10], (DEPTH, DEC_BATCH, CONV_W - 1, D_FF)),
        'w_ada': nrm(ks[11], (DEPTH, D_MODEL, 6 * D_MODEL), 0.5 * D_MODEL ** -0.5),
        'b_ada': nrm(ks[12], (DEPTH, 6 * D_MODEL), 0.01),
        'g_pre_mix': gain(ks[13], (DEPTH, D_MODEL)),
        'g_post_mix': gain(ks[14], (DEPTH, D_MODEL)),
        'w_in': nrm(ks[15], (DEPTH, D_MODEL, D_IN), D_MODEL ** -0.5),
        'b_if': b_if,
        'g_mlstm': gain(ks[18], (DEPTH, W_M)),
        'lambda_q1': nrm(ks[19], (DEPTH, DQK_D), 0.1),
        'lambda_k1': nrm(ks[20], (DEPTH, DQK_D), 0.1),
        'lambda_q2': nrm(ks[21], (DEPTH, DQK_D), 0.1),
        'lambda_k2': nrm(ks[22], (DEPTH, DQK_D), 0.1),
        'g_diff': gain(ks[23], (DEPTH, W_D)),
        'w_proj_m': nrm(ks[24], (DEPTH, W_M, D_MODEL), W_M ** -0.5),
        'w_proj_d': nrm(ks[25], (DEPTH, W_D, D_MODEL), W_D ** -0.5),
        'w_out': nrm(ks[26], (DEPTH, D_MODEL, D_MODEL), D_MODEL ** -0.5),
        'g_pre_ffn': gain(ks[27], (DEPTH, D_MODEL)),
        'g_post_ffn': gain(ks[28], (DEPTH, D_MODEL)),
        'w_up': nrm(ks[29], (DEPTH, D_MODEL, 2 * D_FF), D_MODEL ** -0.5),
        'conv_w': nrm(ks[30], (DEPTH, CONV_W, D_FF), CONV_W ** -0.5),
        'conv_b': nrm(ks[31], (DEPTH, D_FF), 0.01),
        'w_down': nrm(ks[32], (DEPTH, D_FF, D_MODEL), D_FF ** -0.5),
    }


def reference(x_prompt, x_sample, c_prompt, c_sample, cache_k, cache_v, page_table,
              state_C, state_n, state_m, state_conv, w_ada, b_ada, g_pre_mix, g_post_mix,
              w_in, b_if, g_mlstm, lambda_q1, lambda_k1, lambda_q2, lambda_k2, g_diff,
              w_proj_m, w_proj_d, w_out, g_pre_ffn, g_post_ffn, w_up, conv_w, conv_b, w_down):
    f32 = jnp.float32
    B = x_prompt.shape[0]
    n_seq, n_pages = page_table.shape
    page = cache_k.shape[2]
    xp, xs = x_prompt, x_sample
    kp_l, vp_l, Cp_l, np_l, mp_l, cvp_l = [], [], [], [], [], []
    ks_l, vs_l, Cs_l, ns_l, ms_l, cvs_l = [], [], [], [], [], []
    for l in range(DEPTH):
        p = {'w_ada': w_ada[l], 'b_ada': b_ada[l], 'g_pre_mix': g_pre_mix[l], 'g_post_mix': g_post_mix[l],
             'w_in': w_in[l], 'b_if': b_if[l], 'g_mlstm': g_mlstm[l],
             'lq1': lambda_q1[l], 'lk1': lambda_k1[l], 'lq2': lambda_q2[l], 'lk2': lambda_k2[l],
             'g_diff': g_diff[l], 'w_proj_m': w_proj_m[l], 'w_proj_d': w_proj_d[l], 'w_out': w_out[l],
             'g_pre_ffn': g_pre_ffn[l], 'g_post_ffn': g_post_ffn[l], 'w_up': w_up[l],
             'conv_w': conv_w[l], 'conv_b': conv_b[l], 'w_down': w_down[l]}
        lam_init = 0.8 - 0.6 * math.exp(-0.3 * l)
        xp, kp, vp, Cp, n_p, mp, cvp = layer_forward(
            xp, c_prompt, p, lam_init, None, None,
            jnp.zeros((B, H_M, DH_M, DH_M), f32), jnp.zeros((B, H_M, DH_M), f32),
            jnp.zeros((B, H_M), f32), jnp.zeros((B, CONV_W - 1, D_FF), xp.dtype))
        past_k = cache_k[l][page_table].reshape(n_seq, n_pages * page, H_D, 2 * DQK_D)
        past_v = cache_v[l][page_table].reshape(n_seq, n_pages * page, H_D, DV_D)
        xs, ksm, vsm, Cs, n_s, ms, cvs = layer_forward(
            xs, c_sample, p, lam_init, past_k, past_v,
            state_C[l], state_n[l], state_m[l], state_conv[l])
        kp_l.append(kp); vp_l.append(vp)
        Cp_l.append(Cp.astype(state_C.dtype)); np_l.append(n_p.astype(state_n.dtype))
        mp_l.append(mp.astype(state_m.dtype)); cvp_l.append(cvp.astype(state_conv.dtype))
        ks_l.append(ksm); vs_l.append(vsm)
        Cs_l.append(Cs.astype(state_C.dtype)); ns_l.append(n_s.astype(state_n.dtype))
        ms_l.append(ms.astype(state_m.dtype)); cvs_l.append(cvs.astype(state_conv.dtype))
    return (xp, xs,
            jnp.stack(kp_l), jnp.stack(vp_l), jnp.stack(Cp_l), jnp.stack(np_l), jnp.stack(mp_l), jnp.stack(cvp_l),
            jnp.stack(ks_l), jnp.stack(vs_l), jnp.stack(Cs_l), jnp.stack(ns_l), jnp.stack(ms_l), jnp.stack(cvs_l))
```

```python
import functools
import math

import jax
import jax.numpy as jnp
from jax import lax
from jax.experimental import pallas as pl
from jax.experimental.pallas import tpu as pltpu

F32 = jnp.float32
BF16 = jnp.bfloat16

D_MODEL = 1024
H_M = 4
DH_M = 128
W_M = H_M * DH_M
H_D = 4
DQK_D = 64
DV_D = 2 * DQK_D
W_D = H_D * DV_D
D_FF = 2816
CONV_W = 3
EPS = 1e-6
LANES = 128
SUBLANES = 8
NEG_BIG = -1e30
NEW_PAD = 16
LOG2E = 1.4426950408889634

C_MQ, C_MK, C_MV, C_MO = 0, 512, 1024, 1536
C_IG, C_FG = 2048, 2176
C_DQ, C_DK, C_DV = 2304, 2816, 3328
C_GM, C_GD = 3840, 4864
D_IN_PAD = 5888

VMEM_LIMIT = 56 * 1024 * 1024


def _cparams(sem):
    return pltpu.CompilerParams(dimension_semantics=sem, vmem_limit_bytes=VMEM_LIMIT)


def _const_spec(shape):
    nd = len(shape)
    return pl.BlockSpec(shape, lambda *_: (0,) * nd, pipeline_mode=pl.Buffered(1))


def _rms(x, g):
    ms = jnp.mean(x * x, axis=-1, keepdims=True)
    return x * lax.rsqrt(ms + EPS) * g


def _nt(a, b):
    return lax.dot_general(a, b, (((1,), (1,)), ((), ())), preferred_element_type=F32)


def _tn(a, b):
    return lax.dot_general(a, b, (((0,), (0,)), ((), ())), preferred_element_type=F32)


def _dot(a, b):
    return jnp.dot(a, b, preferred_element_type=F32)


def _idiv(x, n):
    assert n & (n - 1) == 0
    return lax.shift_right_logical(x, jnp.int32(n.bit_length() - 1))


def _imod(x, n):
    assert n & (n - 1) == 0
    return lax.bitwise_and(x, jnp.int32(n - 1))


def _split3(x):
    hi = x.astype(BF16)
    r1 = x - hi.astype(F32)
    mid = r1.astype(BF16)
    r2 = r1 - mid.astype(F32)
    return hi, mid, r2.astype(BF16)


def _ada_kernel(c_ref, w_ref, b_ref, o_ref):
    o_ref[...] = _dot(c_ref[...].astype(BF16), w_ref[...].astype(BF16)) + b_ref[...]


def _ada(c_all, w_ada, b_ada):
    n = c_all.shape[0]
    tn = 1024
    return pl.pallas_call(
        _ada_kernel,
        grid=(6 * D_MODEL // tn,),
        in_specs=[
            pl.BlockSpec((n, D_MODEL), lambda j: (0, 0)),
            pl.BlockSpec((D_MODEL, tn), lambda j: (0, j)),
            pl.BlockSpec((1, tn), lambda j: (0, j)),
        ],
        out_specs=pl.BlockSpec((n, tn), lambda j: (0, j)),
        out_shape=jax.ShapeDtypeStruct((n, 6 * D_MODEL), F32),
        compiler_params=_cparams(("arbitrary",)),
        name="adaln",
    )(c_all, w_ada, b_ada)


def _inproj_kernel(x_ref, sh_ref, sc_ref, g_ref, w_ref, bif_ref,
                   mq_ref, mk_ref, mv_ref, so_ref, gt_ref, dq_ref, dkf_ref, dkb_ref, dvf_ref, dvb_ref,
                   sgm_ref, sgd_ref, *, v_transposed):
    h = _rms(x_ref[...], g_ref[...]) * (1.0 + sc_ref[...]) + sh_ref[...]
    hb = h.astype(BF16)

    def seg(c0, c1):
        return _dot(hb, w_ref[:, c0:c1])

    mq_ref[...] = seg(C_MQ, C_MK).astype(BF16)
    mk_ref[...] = (seg(C_MK, C_MV) * (DH_M ** -0.5)).astype(BF16)
    mv_ref[...] = seg(C_MV, C_MO).astype(BF16)
    so_ref[...] = jax.nn.sigmoid(seg(C_MO, C_IG)).astype(BF16)
    gt_ref[...] = seg(C_IG, C_DQ) + bif_ref[...]
    dq_ref[...] = (seg(C_DQ, C_DK) * (DQK_D ** -0.5 * LOG2E)).astype(BF16)
    dk = seg(C_DK, C_DV)
    dkf_ref[...] = dk
    dkb_ref[...] = dk.astype(BF16)
    dv = seg(C_DV, C_GM)
    dvf_ref[...] = dv
    if v_transposed:
        dvb_ref[0] = dv.T.astype(BF16)
    else:
        dvb_ref[...] = dv.astype(BF16)
    sgm_ref[...] = jax.nn.sigmoid(seg(C_GM, C_GD)).astype(BF16)
    sgd_ref[...] = jax.nn.sigmoid(seg(C_GD, D_IN_PAD)).astype(BF16)


def _mod_spec(tm, rows_per_mod):
    if rows_per_mod is None:
        return pl.BlockSpec((tm, D_MODEL), lambda i: (i, 0))
    tiles = rows_per_mod // tm
    return pl.BlockSpec((None, 1, D_MODEL), lambda i: (i // tiles, 0, 0))


def _inproj(x, shift, scale, g, w, bif, *, tm, rows_per_mod, v_transposed):
    n = x.shape[0]
    row = lambda width: pl.BlockSpec((tm, width), lambda i: (i, 0))
    sds = lambda width, dt: jax.ShapeDtypeStruct((n, width), dt)
    if v_transposed:
        nb = n // rows_per_mod
        tiles = rows_per_mod // tm
        dvb_spec = pl.BlockSpec((1, W_D, tm), lambda i: (i // tiles, 0, i % tiles))
        dvb_sds = jax.ShapeDtypeStruct((nb, W_D, rows_per_mod), BF16)
    else:
        dvb_spec, dvb_sds = row(W_D), sds(W_D, BF16)
    return pl.pallas_call(
        functools.partial(_inproj_kernel, v_transposed=v_transposed),
        grid=(n // tm,),
        in_specs=[row(D_MODEL), _mod_spec(tm, rows_per_mod), _mod_spec(tm, rows_per_mod),
                  _const_spec((1, D_MODEL)), _const_spec((D_MODEL, D_IN_PAD)), _const_spec((1, 2 * LANES))],
        out_specs=[row(W_M), row(W_M), row(W_M), row(W_M), row(2 * LANES), row(W_D), row(W_D), row(W_D),
                   row(W_D), dvb_spec, row(D_MODEL), row(D_MODEL)],
        out_shape=[sds(W_M, BF16), sds(W_M, BF16), sds(W_M, BF16), sds(W_M, BF16), sds(2 * LANES, F32),
                   sds(W_D, BF16), sds(W_D, F32), sds(W_D, BF16), sds(W_D, F32), dvb_sds,
                   sds(D_MODEL, BF16), sds(D_MODEL, BF16)],
        compiler_params=_cparams(("arbitrary",)),
        name="inproj",
    )(x, shift, scale, g, w, bif)


def _mlstm_kernel(q_ref, k_ref, v_ref, so_ref, gt_ref, c0_ref, n0_ref, m0_ref, g_ref,
                  hm_ref, c_ref, n_ref, m_ref, cext, mscr, *, bb, L):
    j = pl.program_id(1)
    row128 = lax.broadcasted_iota(jnp.int32, (DH_M, DH_M), 0)

    @pl.when(j == 0)
    def _init():
        for b in range(bb):
            for h in range(H_M):
                idx = b * H_M + h
                cext[idx, 0:DH_M, :] = c0_ref[b, h]
                cext[idx, DH_M:2 * DH_M, :] = jnp.where(row128 == 0, n0_ref[b, h:h + 1, :], 0.0)
                mscr[idx] = jnp.broadcast_to(m0_ref[b, h:h + 1, :], (SUBLANES, LANES))

    rowi = lax.broadcasted_iota(jnp.int32, (L, L), 0)
    coli = lax.broadcasted_iota(jnp.int32, (L, L), 1)
    tril = rowi >= coli
    tril_b = jnp.where(tril, 1.0, 0.0).astype(BF16)
    ones_b = jnp.ones((L, 3 * LANES), BF16)
    lane = lax.broadcasted_iota(jnp.int32, (L, LANES), 1)
    e0 = jnp.where(lane == 0, 1.0, 0.0).astype(BF16)

    for b in range(bb):
        gt = gt_ref[b]
        i_col = gt[:, 0:LANES]
        f_raw = gt[:, LANES:2 * LANES]
        logf = -(jnp.maximum(-f_raw, 0.0) + jnp.log(1.0 + jnp.exp(-jnp.abs(f_raw))))
        f_hi, f_mid, f_lo = _split3(logf)
        F = _dot(tril_b, f_hi) + _dot(tril_b, f_mid) + _dot(tril_b, f_lo)
        a = i_col - F
        for h in range(H_M):
            idx = b * H_M + h
            hs = slice(h * DH_M, (h + 1) * DH_M)
            a_h = a[:, h:h + 1]
            F_h = F[:, h:h + 1]
            xh, xm, xl = _split3(jnp.where(lane == h, a, 0.0))
            A = _nt(ones_b, jnp.concatenate([xh, xm, xl], axis=1))
            cm = jnp.max(jnp.where(tril, A, -jnp.inf), axis=1, keepdims=True)
            m_prev = mscr[idx][0:1, 0:1]
            m_col = F_h + jnp.maximum(m_prev, cm)
            dmat = jnp.exp(jnp.where(tril, (F_h - m_col) + A, NEG_BIG))
            qh = q_ref[b, :, hs]
            kh = k_ref[b, :, hs]
            vext = jnp.concatenate([v_ref[b, :, hs], e0], axis=1)
            s = (_nt(qh, kh) * dmat).astype(BF16)
            nd = _dot(s, vext)
            qc = _nt(qh, cext[idx].astype(BF16))
            inter = jnp.exp(F_h + m_prev - m_col)
            num = nd[:, 0:DH_M] + inter * qc[:, 0:DH_M]
            den = nd[:, DH_M:DH_M + 1] + inter * qc[:, DH_M:DH_M + 1]
            hh = num / jnp.maximum(jnp.abs(den), jnp.exp(-m_col))
            hn = _rms(hh, g_ref[:, hs])
            hm_ref[b, :, hs] = (so_ref[b, :, hs].astype(F32) * hn).astype(BF16)
            m_end = m_col[L - 1:L, :]
            f_end = F_h[L - 1:L, :]
            w_end = jnp.exp(f_end + a_h - m_end)
            decay = jnp.exp(f_end + m_prev - m_end)
            kw = (kh.astype(F32) * w_end).astype(BF16)
            cext[idx] = decay * cext[idx] + _tn(vext, kw)
            mscr[idx] = jnp.broadcast_to(m_end, (SUBLANES, LANES))

    @pl.when(j == pl.num_programs(1) - 1)
    def _fin():
        for b in range(bb):
            for h in range(H_M):
                idx = b * H_M + h
                c_ref[b, h] = cext[idx, 0:DH_M, :]
                n_ref[b, h:h + 1, :] = cext[idx, DH_M:DH_M + 1, :]
                m_ref[b, h:h + 1, :] = mscr[idx][0:1, :]


def _mlstm(q, k, v, so, gt, c0, n0, m0, g, *, bb, L):
    B, T, _ = q.shape
    tok = lambda width: pl.BlockSpec((bb, L, width), lambda i, j: (i, j, 0))
    st3 = pl.BlockSpec((bb, H_M, LANES), lambda i, j: (i, 0, 0))
    st4 = pl.BlockSpec((bb, H_M, DH_M, DH_M), lambda i, j: (i, 0, 0, 0))
    return pl.pallas_call(
        functools.partial(_mlstm_kernel, bb=bb, L=L),
        grid=(B // bb, T // L),
        in_specs=[tok(W_M), tok(W_M), tok(W_M), tok(W_M), tok(2 * LANES), st4, st3, st3,
                  pl.BlockSpec((1, W_M), lambda i, j: (0, 0))],
        out_specs=[tok(W_M), st4, st3, st3],
        out_shape=[jax.ShapeDtypeStruct((B, T, W_M), BF16),
                   jax.ShapeDtypeStruct((B, H_M, DH_M, DH_M), F32),
                   jax.ShapeDtypeStruct((B, H_M, LANES), F32),
                   jax.ShapeDtypeStruct((B, H_M, LANES), F32)],
        scratch_shapes=[pltpu.VMEM((bb * H_M, 2 * DH_M, DH_M), F32),
                        pltpu.VMEM((bb * H_M, SUBLANES, LANES), F32)],
        compiler_params=_cparams(("arbitrary", "arbitrary")),
        name="mlstm",
    )(q, k, v, so, gt, c0, n0, m0, g)


def _lambda(lam_ref, lam_init):
    p = lam_ref[...]
    l1 = jnp.sum(p[0:1, :] * p[1:2, :], axis=-1, keepdims=True)
    l2 = jnp.sum(p[2:3, :] * p[3:4, :], axis=-1, keepdims=True)
    return jnp.exp(l1) - jnp.exp(l2) + lam_init


def _attn_kernel(q_ref, k_ref, vt_ref, g_ref, lam_ref, o_ref, qm, acc, mrow, lrow, *, tq, tk, lam_init):
    qi = pl.program_id(2)
    q = q_ref[0]
    lane = lax.broadcasted_iota(jnp.int32, (tq, 2 * DQK_D), 1)
    zero = jnp.zeros_like(q)
    qm[0:tq, :] = jnp.where(lane < DQK_D, q, zero)
    qm[tq:2 * tq, :] = jnp.where(lane >= DQK_D, q, zero)
    acc[...] = jnp.zeros_like(acc)
    mrow[...] = jnp.full_like(mrow, NEG_BIG)
    lrow[...] = jnp.zeros_like(lrow)

    def step(kb, masked):
        k0 = pl.multiple_of(kb * tk, tk)
        s = _nt(k_ref[0, pl.ds(k0, tk), :], qm[...])
        if masked:
            kv_pos = k0 + lax.broadcasted_iota(jnp.int32, (tk, 2 * tq), 0)
            ql = lax.broadcasted_iota(jnp.int32, (tk, 2 * tq), 1)
            q_pos = qi * tq + jnp.where(ql >= tq, ql - tq, ql)
            s = jnp.where(kv_pos <= q_pos, s, NEG_BIG)
        m_old = mrow[...]
        m_new = jnp.maximum(m_old, jnp.max(s, axis=0, keepdims=True))
        alpha = jnp.exp2(m_old - m_new)
        p = jnp.exp2(s - m_new)
        lrow[...] = alpha * lrow[...] + jnp.sum(p, axis=0, keepdims=True)
        acc[...] = acc[...] * alpha + _dot(vt_ref[0, :, pl.ds(k0, tk)], p.astype(BF16))
        mrow[...] = m_new

    n_full = qi * (tq // tk)

    def body(kb, carry):
        step(kb, False)
        return carry

    lax.fori_loop(0, n_full, body, 0)
    for d in range(tq // tk):
        step(n_full + d, True)

    l = lrow[...]
    a = acc[...]
    lam = _lambda(lam_ref, lam_init)
    o = a[:, 0:tq] / l[:, 0:tq] - lam * (a[:, tq:2 * tq] / l[:, tq:2 * tq])
    ms = jnp.mean(o * o, axis=0, keepdims=True)
    on = o * lax.rsqrt(ms + EPS) * (g_ref[0] * (1.0 - lam_init))
    o_ref[0] = on.T.astype(BF16)


def _attn_prompt(q, k, vt, g3, lam_p, *, tq, tk, lam_init):
    B, T, _ = q.shape
    return pl.pallas_call(
        functools.partial(_attn_kernel, tq=tq, tk=tk, lam_init=lam_init),
        grid=(B, H_D, T // tq),
        in_specs=[pl.BlockSpec((1, tq, DV_D), lambda b, h, i: (b, i, h)),
                  pl.BlockSpec((1, T, DV_D), lambda b, h, i: (b, 0, h)),
                  pl.BlockSpec((1, DV_D, T), lambda b, h, i: (b, h, 0)),
                  pl.BlockSpec((1, DV_D, 1), lambda b, h, i: (h, 0, 0)),
                  pl.BlockSpec((4, DQK_D), lambda b, h, i: (0, 0))],
        out_specs=pl.BlockSpec((1, tq, DV_D), lambda b, h, i: (b, i, h)),
        out_shape=jax.ShapeDtypeStruct((B, T, W_D), BF16),
        scratch_shapes=[pltpu.VMEM((2 * tq, DV_D), BF16), pltpu.VMEM((DV_D, 2 * tq), F32),
                        pltpu.VMEM((1, 2 * tq), F32), pltpu.VMEM((1, 2 * tq), F32)],
        compiler_params=_cparams(("arbitrary", "arbitrary", "arbitrary")),
        name="attn_prompt",
    )(q, k, vt, g3, lam_p)


def _decode_kernel(pt_ref, q_ref, kn_ref, vn_ref, g_ref, lam_ref, *rest, n_pages, page, t_new, lam_init):
    k_refs = rest[:n_pages]
    v_refs = rest[n_pages:2 * n_pages]
    o_ref = rest[2 * n_pages]
    nr = 2 * H_D * t_new
    del pt_ref
    r = lax.broadcasted_iota(jnp.int32, (nr, W_D), 0)
    c = lax.broadcasted_iota(jnp.int32, (nr, W_D), 1)
    r_head = _imod(_idiv(r, t_new), H_D)
    c_head = _idiv(c, DV_D)
    same_map = _idiv(r, H_D * t_new) == _idiv(_imod(c, DV_D), DQK_D)
    q = q_ref[0]
    qb = jnp.where(r_head == c_head, jnp.where(same_map, q, jnp.zeros_like(q)), jnp.zeros_like(q))

    s_old = jnp.concatenate([_nt(qb, k_refs[j][...].astype(BF16)) for j in range(n_pages)], axis=1)
    s_new = _nt(qb, kn_ref[0].astype(BF16))
    rr = lax.broadcasted_iota(jnp.int32, (nr, NEW_PAD), 0)
    cc = lax.broadcasted_iota(jnp.int32, (nr, NEW_PAD), 1)
    s_new = jnp.where(cc <= _imod(rr, t_new), s_new, NEG_BIG)
    m = jnp.maximum(jnp.max(s_old, axis=1, keepdims=True), jnp.max(s_new, axis=1, keepdims=True))
    p_old = jnp.exp2(s_old - m)
    p_new = jnp.exp2(s_new - m)
    l = jnp.sum(p_old, axis=1, keepdims=True) + jnp.sum(p_new, axis=1, keepdims=True)
    pb = p_old.astype(BF16)
    out = _dot(p_new.astype(BF16), vn_ref[0].astype(BF16))
    for j in range(n_pages):
        out = out + _dot(pb[:, j * page:(j + 1) * page], v_refs[j][...].astype(BF16))
    out = jnp.where(r_head == c_head, out, 0.0)
    o_r = out[:, 0:DV_D]
    for hh in range(1, H_D):
        o_r = o_r + out[:, hh * DV_D:(hh + 1) * DV_D]
    o_r = o_r / l
    half = H_D * t_new
    lam = _lambda(lam_ref, lam_init)
    o = o_r[0:half, :] - lam * o_r[half:2 * half, :]
    o_ref[0] = (_rms(o, g_ref[...]) * (1.0 - lam_init)).astype(BF16)


def _attn_decode(page_table, q_rep, k_new, v_new, cache_k2, cache_v2, g_rows, lam_p, *, page, t_new, lam_init):
    n_seq, n_pages = page_table.shape
    nr = 2 * H_D * t_new
    half = H_D * t_new

    def page_spec(j):
        return pl.BlockSpec((page, W_D), lambda b, pt: (pt[b, j], 0))

    grid_spec = pltpu.PrefetchScalarGridSpec(
        num_scalar_prefetch=1,
        grid=(n_seq,),
        in_specs=[pl.BlockSpec((1, nr, W_D), lambda b, pt: (b, 0, 0)),
                  pl.BlockSpec((1, NEW_PAD, W_D), lambda b, pt: (b, 0, 0)),
                  pl.BlockSpec((1, NEW_PAD, W_D), lambda b, pt: (b, 0, 0)),
                  pl.BlockSpec((half, DV_D), lambda b, pt: (0, 0)),
                  pl.BlockSpec((4, DQK_D), lambda b, pt: (0, 0))]
                 + [page_spec(j) for j in range(n_pages)] + [page_spec(j) for j in range(n_pages)],
        out_specs=pl.BlockSpec((1, half, DV_D), lambda b, pt: (b, 0, 0)),
    )
    return pl.pallas_call(
        functools.partial(_decode_kernel, n_pages=n_pages, page=page, t_new=t_new, lam_init=lam_init),
        grid_spec=grid_spec,
        out_shape=jax.ShapeDtypeStruct((n_seq, half, DV_D), BF16),
        compiler_params=_cparams(("arbitrary",)),
        name="attn_decode",
    )(page_table, q_rep, k_new, v_new, g_rows, lam_p, *([cache_k2] * n_pages), *([cache_v2] * n_pages))


def _merge_kernel(x_ref, hm_ref, hd_ref, sgm_ref, sgd_ref, gate1_ref, sh2_ref, sc2_ref, gpost_ref, gpre_ref,
                  wpm_ref, wpd_ref, wout_ref, x1_ref, h2_ref):
    pm = _dot(hm_ref[...], wpm_ref[...])
    pd = _dot(hd_ref[...], wpd_ref[...])
    merged = sgm_ref[...].astype(F32) * pm + sgd_ref[...].astype(F32) * pd
    y = _dot(merged.astype(BF16), wout_ref[...])
    x1 = x_ref[...] + gate1_ref[...] * _rms(y, gpost_ref[...])
    x1_ref[...] = x1
    h2_ref[...] = (_rms(x1, gpre_ref[...]) * (1.0 + sc2_ref[...]) + sh2_ref[...]).astype(BF16)


def _merge(x, hm, hd, sgm, sgd, gate1, sh2, sc2, gpost, gpre, wpm, wpd, wout, *, tm, rows_per_mod):
    n = x.shape[0]
    row = lambda width: pl.BlockSpec((tm, width), lambda i: (i, 0))
    mod = _mod_spec(tm, rows_per_mod)
    return pl.pallas_call(
        _merge_kernel,
        grid=(n // tm,),
        in_specs=[row(D_MODEL), row(W_M), row(W_D), row(D_MODEL), row(D_MODEL), mod, mod, mod,
                  _const_spec((1, D_MODEL)), _const_spec((1, D_MODEL)),
                  _const_spec((W_M, D_MODEL)), _const_spec((W_D, D_MODEL)), _const_spec((D_MODEL, D_MODEL))],
        out_specs=[row(D_MODEL), row(D_MODEL)],
        out_shape=[jax.ShapeDtypeStruct((n, D_MODEL), F32), jax.ShapeDtypeStruct((n, D_MODEL), BF16)],
        compiler_params=_cparams(("arbitrary",)),
        name="merge",
    )(x, hm, hd, sgm, sgd, gate1, sh2, sc2, gpost, gpre, wpm, wpd, wout)


def _gelu_tanh(x):
    return 0.5 * x * (1.0 + jnp.tanh(math.sqrt(2.0 / math.pi) * (x + 0.044715 * (x * x * x))))


def _ffn_kernel(*refs, tm, tiles_per_seq, t_seq):
    if tiles_per_seq is None:
        (h2_ref, x1_ref, gate2_ref, gpost_ref, wup_ref, cw_ref, cb_ref, wdn_ref, p1_ref, p2_ref,
         y_ref, a_ref, a_scr) = refs
        a_scr[0:SUBLANES, :] = jnp.zeros((SUBLANES, D_FF), F32)
    else:
        (h2_ref, x1_ref, gate2_ref, gpost_ref, wup_ref, cw_ref, cb_ref, wdn_ref,
         y_ref, tail_ref, a_scr) = refs
        @pl.when(pl.program_id(0) % tiles_per_seq == 0)
        def _zero():
            a_scr[0:SUBLANES, :] = jnp.zeros((SUBLANES, D_FF), F32)

    u = _dot(h2_ref[...], wup_ref[...])
    a = u[:, 0:D_FF]
    a_scr[SUBLANES:SUBLANES + tm, :] = a
    prev1 = a_scr[SUBLANES - 1:SUBLANES - 1 + tm, :]
    prev2 = a_scr[SUBLANES - 2:SUBLANES - 2 + tm, :]
    if tiles_per_seq is None:
        tpos = lax.broadcasted_iota(jnp.int32, (tm, D_FF), 0) % t_seq
        prev1 = jnp.where(tpos >= 1, prev1, p1_ref[...])
        prev2 = jnp.where(tpos >= 2, prev2, p2_ref[...])
        a_ref[...] = a
    else:
        tail = a_scr[tm:tm + SUBLANES, :]
        tail_ref[0] = tail
        a_scr[0:SUBLANES, :] = tail
    conv = cb_ref[...] + cw_ref[0:1, :] * prev2 + cw_ref[1:2, :] * prev1 + cw_ref[2:3, :] * a
    act = (_gelu_tanh(conv) * u[:, D_FF:2 * D_FF]).astype(BF16)
    f = _dot(act, wdn_ref[...])
    y_ref[...] = x1_ref[...] + gate2_ref[...] * _rms(f, gpost_ref[...])


def _ffn(h2, x1, gate2, gpost, wup, cw, cb, wdn, p1=None, p2=None, *, tm, rows_per_mod, t_seq):
    n = h2.shape[0]
    row = lambda width: pl.BlockSpec((tm, width), lambda i: (i, 0))
    in_specs = [row(D_MODEL), row(D_MODEL), _mod_spec(tm, rows_per_mod), _const_spec((1, D_MODEL)),
                _const_spec((D_MODEL, 2 * D_FF)), _const_spec((SUBLANES, D_FF)), _const_spec((1, D_FF)),
                _const_spec((D_FF, D_MODEL))]
    args = [h2, x1, gate2, gpost, wup, cw, cb, wdn]
    if rows_per_mod is None:
        tiles_per_seq = None
        in_specs += [row(D_FF), row(D_FF)]
        args += [p1, p2]
        out_specs = [row(D_MODEL), row(D_FF)]
        out_shape = [jax.ShapeDtypeStruct((n, D_MODEL), F32), jax.ShapeDtypeStruct((n, D_FF), F32)]
    else:
        tiles_per_seq = rows_per_mod // tm
        out_specs = [row(D_MODEL), pl.BlockSpec((1, SUBLANES, D_FF), lambda i: (i // tiles_per_seq, 0, 0))]
        out_shape = [jax.ShapeDtypeStruct((n, D_MODEL), F32),
                     jax.ShapeDtypeStruct((n // rows_per_mod, SUBLANES, D_FF), F32)]
    return pl.pallas_call(
        functools.partial(_ffn_kernel, tm=tm, tiles_per_seq=tiles_per_seq, t_seq=t_seq),
        grid=(n // tm,),
        in_specs=in_specs,
        out_specs=out_specs,
        out_shape=out_shape,
        scratch_shapes=[pltpu.VMEM((tm + SUBLANES, D_FF), F32)],
        compiler_params=_cparams(("arbitrary",)),
        name="ffn",
    )(*args)


def _prep_weights(w_in, b_if, w_proj_m, w_proj_d, w_out, w_up, conv_w, conv_b, w_down):
    off = [0]
    for s in (W_M, W_M, W_M, W_M, 2 * H_M, H_D * 2 * DQK_D, H_D * 2 * DQK_D, W_D, D_MODEL, D_MODEL):
        off.append(off[-1] + s)
    mq, mk, mv, mo, mif, dq, dk, dv, gm, gd = [w_in[:, off[i]:off[i + 1]] for i in range(10)]
    zpad = jnp.zeros((D_MODEL, LANES - H_M), F32)
    w_pad = jnp.concatenate([mq, mk, mv, mo, mif[:, :H_M], zpad, mif[:, H_M:], zpad, dq, dk, dv, gm, gd],
                            axis=1).astype(BF16)
    bpad = jnp.zeros((LANES - H_M,), F32)
    bif = jnp.concatenate([b_if[:H_M], bpad, b_if[H_M:], bpad]).reshape(1, 2 * LANES)
    cw = jnp.concatenate([conv_w, jnp.zeros((SUBLANES - CONV_W, D_FF), F32)], axis=0)
    return dict(w_in=w_pad, bif=bif, wpm=w_proj_m.astype(BF16), wpd=w_proj_d.astype(BF16),
                wout=w_out.astype(BF16), wup=w_up.astype(BF16), cw=cw, cb=conv_b.reshape(1, D_FF),
                wdn=w_down.astype(BF16))


def _layer(l, pw, gains, lam_p, ada_p, ada_s, x_prompt, x_sample, cache_k, cache_v, page_table,
           state_C, state_n, state_m, state_conv, *, tm_p, l_chunk, tq, tk):
    B, T, _ = x_prompt.shape
    S, Ts, _ = x_sample.shape
    lam_init = 0.8 - 0.6 * math.exp(-0.3 * l)
    g_pre_mix, g_post_mix, g_mlstm, g_diff, g_pre_ffn, g_post_ffn = gains
    row = lambda g: g.reshape(1, -1)

    xp = x_prompt.reshape(B * T, D_MODEL)
    modp = [a.reshape(B, 1, D_MODEL) for a in jnp.split(ada_p, 6, axis=-1)]
    (mq, mk, mv, so, gt, dq, dkf, dkb, dvf, dvt, sgm, sgd) = _inproj(
        xp, modp[0], modp[1], row(g_pre_mix), pw["w_in"], pw["bif"], tm=tm_p, rows_per_mod=T, v_transposed=True)
    tok = lambda a: a.reshape(B, T, -1)
    zc = jnp.zeros((B, H_M, DH_M, DH_M), F32)
    zn = jnp.zeros((B, H_M, LANES), F32)
    hm, Cp, n_p, m_p = _mlstm(tok(mq), tok(mk), tok(mv), tok(so), tok(gt), zc, zn, zn, row(g_mlstm),
                              bb=B, L=l_chunk)
    hd = _attn_prompt(tok(dq), tok(dkb), dvt, g_diff.reshape(H_D, DV_D, 1), lam_p, tq=tq, tk=tk,
                      lam_init=lam_init)
    x1, h2 = _merge(xp, hm.reshape(B * T, W_M), hd.reshape(B * T, W_D), sgm, sgd, modp[2], modp[3], modp[4],
                    row(g_post_mix), row(g_pre_ffn), pw["wpm"], pw["wpd"], pw["wout"], tm=tm_p, rows_per_mod=T)
    yp, tail = _ffn(h2, x1, modp[5], row(g_post_ffn), pw["wup"], pw["cw"], pw["cb"], pw["wdn"],
                    tm=tm_p, rows_per_mod=T, t_seq=T)
    out_p = (yp.reshape(B, T, D_MODEL), dkf.reshape(B, T, H_D, 2 * DQK_D), dvf.reshape(B, T, H_D, DV_D),
             Cp, n_p, m_p[:, :, 0], tail[:, SUBLANES - (CONV_W - 1):, :])

    ns = S * Ts
    xs = x_sample.reshape(ns, D_MODEL)
    mods = [jnp.repeat(a, Ts, axis=0) for a in jnp.split(ada_s, 6, axis=-1)]
    (mq, mk, mv, so, gt, dq, dkf, dkb, dvf, dvb, sgm, sgd) = _inproj(
        xs, mods[0], mods[1], row(g_pre_mix), pw["w_in"], pw["bif"], tm=ns, rows_per_mod=None, v_transposed=False)
    Lp = 16
    padt = lambda a: jnp.pad(a.reshape(S, Ts, -1), ((0, 0), (0, Lp - Ts), (0, 0)))
    gt3 = gt.reshape(S, Ts, 2 * LANES)
    gpad = jnp.concatenate([jnp.full((S, Lp - Ts, LANES), NEG_BIG, F32),
                            jnp.full((S, Lp - Ts, LANES), -NEG_BIG, F32)], axis=-1)
    gt_p = jnp.concatenate([gt3, gpad], axis=1)
    m0 = jnp.broadcast_to(state_m[l][:, :, None], (S, H_M, LANES))
    hm, Cs, n_s, m_s = _mlstm(padt(mq), padt(mk), padt(mv), padt(so), gt_p, state_C[l], state_n[l], m0,
                              row(g_mlstm), bb=8, L=Lp)
    hm = hm[:, :Ts, :].reshape(ns, W_M)
    n_pool, page = cache_k.shape[1], cache_k.shape[2]
    ck2 = cache_k[l].reshape(n_pool * page, W_D)
    cv2 = cache_v[l].reshape(n_pool * page, W_D)
    q_rep = jnp.broadcast_to(dq.reshape(S, 1, Ts, W_D), (S, 2 * H_D, Ts, W_D)).reshape(S, 2 * H_D * Ts, W_D)
    padp = lambda a: jnp.pad(a.reshape(S, Ts, W_D), ((0, 0), (0, NEW_PAD - Ts), (0, 0)))
    g_rows = jnp.repeat(g_diff.reshape(H_D, DV_D), Ts, axis=0)
    hd = _attn_decode(page_table, q_rep, padp(dkf), padp(dvf), ck2, cv2, g_rows, lam_p,
                      page=page, t_new=Ts, lam_init=lam_init)
    hd = hd.reshape(S, H_D, Ts, DV_D).transpose(0, 2, 1, 3).reshape(ns, W_D)
    x1, h2 = _merge(xs, hm, hd, sgm, sgd, mods[2], mods[3], mods[4], row(g_post_mix), row(g_pre_ffn),
                    pw["wpm"], pw["wpd"], pw["wout"], tm=ns, rows_per_mod=None)
    st = state_conv[l]
    zrow = jnp.zeros((S, Ts - 1, D_FF), F32)
    p1 = jnp.concatenate([st[:, 1:2], zrow], axis=1).reshape(ns, D_FF)
    p2 = jnp.concatenate([st, jnp.zeros((S, Ts - 2, D_FF), F32)], axis=1).reshape(ns, D_FF)
    ys, a_s = _ffn(h2, x1, mods[5], row(g_post_ffn), pw["wup"], pw["cw"], pw["cb"], pw["wdn"], p1, p2,
                   tm=ns, rows_per_mod=None, t_seq=Ts)
    out_s = (ys.reshape(S, Ts, D_MODEL), dkf.reshape(S, Ts, H_D, 2 * DQK_D), dvf.reshape(S, Ts, H_D, DV_D),
             Cs, n_s, m_s[:, :, 0], a_s.reshape(S, Ts, D_FF)[:, Ts - (CONV_W - 1):, :])
    return out_p, out_s


def kernel(x_prompt, x_sample, c_prompt, c_sample, cache_k, cache_v, page_table, state_C, state_n, state_m,
           state_conv, w_ada, b_ada, g_pre_mix, g_post_mix, w_in, b_if, g_mlstm, lambda_q1, lambda_k1,
           lambda_q2, lambda_k2, g_diff, w_proj_m, w_proj_d, w_out, g_pre_ffn, g_post_ffn, w_up, conv_w,
           conv_b, w_down):
    depth = w_in.shape[0]
    B = x_prompt.shape[0]
    S = x_sample.shape[0]
    pad = (-B) % SUBLANES
    c_all = jnp.concatenate([c_prompt, jnp.zeros((pad, D_MODEL), F32), c_sample], axis=0)
    xp, xs = x_prompt, x_sample
    outs_p, outs_s = [], []
    for l in range(depth):
        ada = _ada(c_all, w_ada[l], b_ada[l].reshape(1, -1))
        pw = _prep_weights(w_in[l], b_if[l], w_proj_m[l], w_proj_d[l], w_out[l], w_up[l], conv_w[l],
                           conv_b[l], w_down[l])
        gains = (g_pre_mix[l], g_post_mix[l], g_mlstm[l], g_diff[l], g_pre_ffn[l], g_post_ffn[l])
        lam_p = jnp.stack([lambda_q1[l], lambda_k1[l], lambda_q2[l], lambda_k2[l]], axis=0)
        op, os_ = _layer(l, pw, gains, lam_p, ada[:B], ada[B + pad:], xp, xs, cache_k, cache_v, page_table,
                         state_C, state_n, state_m, state_conv, tm_p=256, l_chunk=128, tq=256, tk=256)
        xp, xs = op[0], os_[0]
        outs_p.append(op[1:])
        outs_s.append(os_[1:])
    stack = lambda outs, i: jnp.stack([o[i] for o in outs])
    return ((xp, xs) + tuple(stack(outs_p, i) for i in range(6)) + tuple(stack(outs_s, i) for i in range(6)))
```

```python
import functools
import math

import jax
import jax.numpy as jnp
from jax import lax
from jax.experimental import pallas as pl
from jax.experimental.pallas import tpu as pltpu

F32 = jnp.float32
BF16 = jnp.bfloat16

D_MODEL = 1024
H_M = 4
DH_M = 128
W_M = H_M * DH_M
H_D = 4
DQK_D = 64
DV_D = 2 * DQK_D
W_D = H_D * DV_D
D_FF = 2816
CONV_W = 3
EPS = 1e-6
LANES = 128
SUBLANES = 8
NEG_BIG = -1e30
LOG2E = 1.4426950408889634
C_MQ, C_MK, C_MV, C_MO = 0, 512, 1024, 1536
C_IG, C_FG = 2048, 2176
C_DQ, C_DK, C_DV = 2304, 2816, 3328
C_GM, C_GD = 3840, 4864
D_IN_PAD = 5888

VMEM_LIMIT = 56 * 1024 * 1024


def _cparams(sem):
    return pltpu.CompilerParams(dimension_semantics=sem, vmem_limit_bytes=VMEM_LIMIT)


def _const_spec(shape):
    nd = len(shape)
    return pl.BlockSpec(shape, lambda *_: (0,) * nd, pipeline_mode=pl.Buffered(1))


def _rms(x, g):
    ms = jnp.mean(x * x, axis=-1, keepdims=True)
    return x * lax.rsqrt(ms + EPS) * g


def _nt(a, b):
    return lax.dot_general(a, b, (((1,), (1,)), ((), ())), preferred_element_type=F32)


def _tn(a, b):
    return lax.dot_general(a, b, (((0,), (0,)), ((), ())), preferred_element_type=F32)


def _dot(a, b):
    return jnp.dot(a, b, preferred_element_type=F32)


def _idiv(x, n):
    assert n & (n - 1) == 0
    return lax.shift_right_logical(x, jnp.int32(n.bit_length() - 1))


def _imod(x, n):
    assert n & (n - 1) == 0
    return lax.bitwise_and(x, jnp.int32(n - 1))


def _split3(x):
    hi = x.astype(BF16)
    r1 = x - hi.astype(F32)
    mid = r1.astype(BF16)
    r2 = r1 - mid.astype(F32)
    return hi, mid, r2.astype(BF16)


def _ada_kernel(c_ref, w_ref, b_ref, o_ref):
    o_ref[...] = _dot(c_ref[...].astype(BF16), w_ref[...].astype(BF16)) + b_ref[...]


def _ada(c_all, w_ada, b_ada):
    n = c_all.shape[0]
    tn = 1024
    return pl.pallas_call(
        _ada_kernel,
        grid=(6 * D_MODEL // tn,),
        in_specs=[
            pl.BlockSpec((n, D_MODEL), lambda j: (0, 0)),
            pl.BlockSpec((D_MODEL, tn), lambda j: (0, j)),
            pl.BlockSpec((1, tn), lambda j: (0, j)),
        ],
        out_specs=pl.BlockSpec((n, tn), lambda j: (0, j)),
        out_shape=jax.ShapeDtypeStruct((n, 6 * D_MODEL), F32),
        compiler_params=_cparams(("arbitrary",)),
        name="adaln",
    )(c_all, w_ada, b_ada)


def _inproj_kernel(x_ref, sh_ref, sc_ref, g_ref, w_ref, bif_ref,
                   mq_ref, mk_ref, mv_ref, so_ref, gt_ref, dq_ref, dkf_ref, dkb_ref, dvf_ref, dvb_ref,
                   sgm_ref, sgd_ref, *, v_transposed):
    h = _rms(x_ref[...], g_ref[...]) * (1.0 + sc_ref[...]) + sh_ref[...]
    hb = h.astype(BF16)

    def seg(c0, c1):
        return _dot(hb, w_ref[:, c0:c1])

    mq_ref[...] = seg(C_MQ, C_MK).astype(BF16)
    mk_ref[...] = (seg(C_MK, C_MV) * (DH_M ** -0.5)).astype(BF16)
    mv_ref[...] = seg(C_MV, C_MO).astype(BF16)
    so_ref[...] = jax.nn.sigmoid(seg(C_MO, C_IG)).astype(BF16)
    gt_ref[...] = seg(C_IG, C_DQ) + bif_ref[...]
    dq_ref[...] = (seg(C_DQ, C_DK) * (DQK_D ** -0.5 * LOG2E)).astype(BF16)
    dk = seg(C_DK, C_DV)
    dkf_ref[...] = dk
    dkb_ref[...] = dk.astype(BF16)
    dv = seg(C_DV, C_GM)
    dvf_ref[...] = dv
    if v_transposed:
        dvb_ref[0] = dv.T.astype(BF16)
    else:
        dvb_ref[...] = dv.astype(BF16)
    sgm_ref[...] = jax.nn.sigmoid(seg(C_GM, C_GD)).astype(BF16)
    sgd_ref[...] = jax.nn.sigmoid(seg(C_GD, D_IN_PAD)).astype(BF16)


def _mod_spec(tm, rows_per_mod):
    if rows_per_mod is None:
        return pl.BlockSpec((tm, D_MODEL), lambda i: (i, 0))
    tiles = rows_per_mod // tm
    return pl.BlockSpec((None, 1, D_MODEL), lambda i: (i // tiles, 0, 0))


def _inproj(x, shift, scale, g, w, bif, *, tm, rows_per_mod, v_transposed):
    n = x.shape[0]
    row = lambda width: pl.BlockSpec((tm, width), lambda i: (i, 0))
    sds = lambda width, dt: jax.ShapeDtypeStruct((n, width), dt)
    if v_transposed:
        nb = n // rows_per_mod
        tiles = rows_per_mod // tm
        dvb_spec = pl.BlockSpec((1, W_D, tm), lambda i: (i // tiles, 0, i % tiles))
        dvb_sds = jax.ShapeDtypeStruct((nb, W_D, rows_per_mod), BF16)
    else:
        dvb_spec, dvb_sds = row(W_D), sds(W_D, BF16)
    return pl.pallas_call(
        functools.partial(_inproj_kernel, v_transposed=v_transposed),
        grid=(n // tm,),
        in_specs=[row(D_MODEL), _mod_spec(tm, rows_per_mod), _mod_spec(tm, rows_per_mod),
                  _const_spec((1, D_MODEL)), _const_spec((D_MODEL, D_IN_PAD)), _const_spec((1, 2 * LANES))],
        out_specs=[row(W_M), row(W_M), row(W_M), row(W_M), row(2 * LANES), row(W_D), row(W_D), row(W_D),
                   row(W_D), dvb_spec, row(D_MODEL), row(D_MODEL)],
        out_shape=[sds(W_M, BF16), sds(W_M, BF16), sds(W_M, BF16), sds(W_M, BF16), sds(2 * LANES, F32),
                   sds(W_D, BF16), sds(W_D, F32), sds(W_D, BF16), sds(W_D, F32), dvb_sds,
                   sds(D_MODEL, BF16), sds(D_MODEL, BF16)],
        compiler_params=_cparams(("arbitrary",)),
        name="inproj",
    )(x, shift, scale, g, w, bif)


def _mlstm_kernel(q_ref, k_ref, v_ref, so_ref, gt_ref, c0_ref, n0_ref, m0_ref, g_ref,
                  hm_ref, c_ref, n_ref, m_ref, cext, mscr, *, bb, L):
    j = pl.program_id(1)
    row128 = lax.broadcasted_iota(jnp.int32, (DH_M, DH_M), 0)

    @pl.when(j == 0)
    def _init():
        for b in range(bb):
            for h in range(H_M):
                idx = b * H_M + h
                cext[idx, 0:DH_M, :] = c0_ref[b, h]
                cext[idx, DH_M:2 * DH_M, :] = jnp.where(row128 == 0, n0_ref[b, h:h + 1, :], 0.0)
                mscr[idx] = jnp.broadcast_to(m0_ref[b, h:h + 1, :], (SUBLANES, LANES))

    rowi = lax.broadcasted_iota(jnp.int32, (L, L), 0)
    coli = lax.broadcasted_iota(jnp.int32, (L, L), 1)
    tril = rowi >= coli
    tril_b = jnp.where(tril, 1.0, 0.0).astype(BF16)
    ones_b = jnp.ones((L, 3 * LANES), BF16)
    lane = lax.broadcasted_iota(jnp.int32, (L, LANES), 1)
    e0 = jnp.where(lane == 0, 1.0, 0.0).astype(BF16)

    for b in range(bb):
        gt = gt_ref[b]
        i_col = gt[:, 0:LANES]
        f_raw = gt[:, LANES:2 * LANES]
        logf = -(jnp.maximum(-f_raw, 0.0) + jnp.log(1.0 + jnp.exp(-jnp.abs(f_raw))))
        f_hi, f_mid, f_lo = _split3(logf)
        F = _dot(tril_b, f_hi) + _dot(tril_b, f_mid) + _dot(tril_b, f_lo)
        a = i_col - F
        for h in range(H_M):
            idx = b * H_M + h
            hs = slice(h * DH_M, (h + 1) * DH_M)
            a_h = a[:, h:h + 1]
            F_h = F[:, h:h + 1]
            xh, xm, xl = _split3(jnp.where(lane == h, a, 0.0))
            A = _nt(ones_b, jnp.concatenate([xh, xm, xl], axis=1))
            cm = jnp.max(jnp.where(tril, A, -jnp.inf), axis=1, keepdims=True)
            m_prev = mscr[idx][0:1, 0:1]
            m_col = F_h + jnp.maximum(m_prev, cm)
            dmat = jnp.exp(jnp.where(tril, (F_h - m_col) + A, NEG_BIG))
            qh = q_ref[b, :, hs]
            kh = k_ref[b, :, hs]
            vext = jnp.concatenate([v_ref[b, :, hs], e0], axis=1)
            s = (_nt(qh, kh) * dmat).astype(BF16)
            nd = _dot(s, vext)
            qc = _nt(qh, cext[idx].astype(BF16))
            inter = jnp.exp(F_h + m_prev - m_col)
            num = nd[:, 0:DH_M] + inter * qc[:, 0:DH_M]
            den = nd[:, DH_M:DH_M + 1] + inter * qc[:, DH_M:DH_M + 1]
            hh = num / jnp.maximum(jnp.abs(den), jnp.exp(-m_col))
            hn = _rms(hh, g_ref[:, hs])
            hm_ref[b, :, hs] = (so_ref[b, :, hs].astype(F32) * hn).astype(BF16)
            m_end = m_col[L - 1:L, :]
            f_end = F_h[L - 1:L, :]
            w_end = jnp.exp(f_end + a_h - m_end)
            decay = jnp.exp(f_end + m_prev - m_end)
            kw = (kh.astype(F32) * w_end).astype(BF16)
            cext[idx] = decay * cext[idx] + _tn(vext, kw)
            mscr[idx] = jnp.broadcast_to(m_end, (SUBLANES, LANES))

    @pl.when(j == pl.num_programs(1) - 1)
    def _fin():
        for b in range(bb):
            for h in range(H_M):
                idx = b * H_M + h
                c_ref[b, h] = cext[idx, 0:DH_M, :]
                n_ref[b, h:h + 1, :] = cext[idx, DH_M:DH_M + 1, :]
                m_ref[b, h:h + 1, :] = mscr[idx][0:1, :]


def _mlstm(q, k, v, so, gt, c0, n0, m0, g, *, bb, L):
    B, T, _ = q.shape
    tok = lambda width: pl.BlockSpec((bb, L, width), lambda i, j: (i, j, 0))
    st3 = pl.BlockSpec((bb, H_M, LANES), lambda i, j: (i, 0, 0))
    st4 = pl.BlockSpec((bb, H_M, DH_M, DH_M), lambda i, j: (i, 0, 0, 0))
    return pl.pallas_call(
        functools.partial(_mlstm_kernel, bb=bb, L=L),
        grid=(B // bb, T // L),
        in_specs=[tok(W_M), tok(W_M), tok(W_M), tok(W_M), tok(2 * LANES), st4, st3, st3,
                  pl.BlockSpec((1, W_M), lambda i, j: (0, 0))],
        out_specs=[tok(W_M), st4, st3, st3],
        out_shape=[jax.ShapeDtypeStruct((B, T, W_M), BF16),
                   jax.ShapeDtypeStruct((B, H_M, DH_M, DH_M), F32),
                   jax.ShapeDtypeStruct((B, H_M, LANES), F32),
                   jax.ShapeDtypeStruct((B, H_M, LANES), F32)],
        scratch_shapes=[pltpu.VMEM((bb * H_M, 2 * DH_M, DH_M), F32),
                        pltpu.VMEM((bb * H_M, SUBLANES, LANES), F32)],
        compiler_params=_cparams(("arbitrary", "arbitrary")),
        name="mlstm",
    )(q, k, v, so, gt, c0, n0, m0, g)


def _lambda(lam_ref, lam_init):
    p = lam_ref[...]
    l1 = jnp.sum(p[0:1, :] * p[1:2, :], axis=-1, keepdims=True)
    l2 = jnp.sum(p[2:3, :] * p[3:4, :], axis=-1, keepdims=True)
    return jnp.exp(l1) - jnp.exp(l2) + lam_init


def _attn_kernel(q_ref, k_ref, vt_ref, g_ref, lam_ref, o_ref, qm, acc, mrow, lrow, *, tq, tk, lam_init):
    qi = pl.program_id(2)
    q = q_ref[0]
    lane = lax.broadcasted_iota(jnp.int32, (tq, 2 * DQK_D), 1)
    zero = jnp.zeros_like(q)
    qm[0:tq, :] = jnp.where(lane < DQK_D, q, zero)
    qm[tq:2 * tq, :] = jnp.where(lane >= DQK_D, q, zero)
    acc[...] = jnp.zeros_like(acc)
    mrow[...] = jnp.full_like(mrow, NEG_BIG)
    lrow[...] = jnp.zeros_like(lrow)

    def step(kb, masked):
        k0 = pl.multiple_of(kb * tk, tk)
        s = _nt(k_ref[0, pl.ds(k0, tk), :], qm[...])
        if masked:
            kv_pos = k0 + lax.broadcasted_iota(jnp.int32, (tk, 2 * tq), 0)
            ql = lax.broadcasted_iota(jnp.int32, (tk, 2 * tq), 1)
            q_pos = qi * tq + jnp.where(ql >= tq, ql - tq, ql)
            s = jnp.where(kv_pos <= q_pos, s, NEG_BIG)
        m_old = mrow[...]
        m_new = jnp.maximum(m_old, jnp.max(s, axis=0, keepdims=True))
        alpha = jnp.exp2(m_old - m_new)
        p = jnp.exp2(s - m_new)
        lrow[...] = alpha * lrow[...] + jnp.sum(p, axis=0, keepdims=True)
        acc[...] = acc[...] * alpha + _dot(vt_ref[0, :, pl.ds(k0, tk)], p.astype(BF16))
        mrow[...] = m_new

    n_full = (qi * tq) // tk

    def body(kb, carry):
        step(kb, False)
        return carry

    lax.fori_loop(0, n_full, body, 0)
    step(n_full, True)

    l = lrow[...]
    a = acc[...]
    lam = _lambda(lam_ref, lam_init)
    o = a[:, 0:tq] / l[:, 0:tq] - lam * (a[:, tq:2 * tq] / l[:, tq:2 * tq])
    ms = jnp.mean(o * o, axis=0, keepdims=True)
    on = o * lax.rsqrt(ms + EPS) * (g_ref[0] * (1.0 - lam_init))
    o_ref[0] = on.T.astype(BF16)


def _attn_prompt(q, k, vt, g3, lam_p, *, tq, tk, lam_init):
    B, T, _ = q.shape
    return pl.pallas_call(
        functools.partial(_attn_kernel, tq=tq, tk=tk, lam_init=lam_init),
        grid=(B, H_D, T // tq),
        in_specs=[pl.BlockSpec((1, tq, DV_D), lambda b, h, i: (b, i, h)),
                  pl.BlockSpec((1, T, DV_D), lambda b, h, i: (b, 0, h)),
                  pl.BlockSpec((1, DV_D, T), lambda b, h, i: (b, h, 0)),
                  pl.BlockSpec((1, DV_D, 1), lambda b, h, i: (h, 0, 0)),
                  pl.BlockSpec((4, DQK_D), lambda b, h, i: (0, 0))],
        out_specs=pl.BlockSpec((1, tq, DV_D), lambda b, h, i: (b, i, h)),
        out_shape=jax.ShapeDtypeStruct((B, T, W_D), BF16),
        scratch_shapes=[pltpu.VMEM((2 * tq, DV_D), BF16), pltpu.VMEM((DV_D, 2 * tq), F32),
                        pltpu.VMEM((1, 2 * tq), F32), pltpu.VMEM((1, 2 * tq), F32)],
        compiler_params=_cparams(("arbitrary", "arbitrary", "arbitrary")),
        name="attn_prompt",
    )(q, k, vt, g3, lam_p)


def _decode_kernel(pt_ref, q_ref, kn_ref, vn_ref, g_ref, lam_ref, *rest, n_pages, page, t_new, lam_init):
    k_refs = rest[:n_pages]
    v_refs = rest[n_pages:2 * n_pages]
    o_ref = rest[2 * n_pages]
    nr = 2 * H_D * t_new
    pr = page * H_D
    nn = t_new * H_D
    del pt_ref
    q = q_ref[0]
    rq = lax.broadcasted_iota(jnp.int32, (nr, DV_D), 0)
    cq = lax.broadcasted_iota(jnp.int32, (nr, DV_D), 1)
    qb = jnp.where(_idiv(rq, H_D * t_new) == _idiv(cq, DQK_D), q, jnp.zeros_like(q))

    r = lax.broadcasted_iota(jnp.int32, (nr, pr), 0)
    c = lax.broadcasted_iota(jnp.int32, (nr, pr), 1)
    head_ok = _imod(_idiv(r, t_new), H_D) == _imod(c, H_D)
    s_old = jnp.concatenate(
        [jnp.where(head_ok, _nt(qb, k_refs[j][...].astype(BF16)), NEG_BIG) for j in range(n_pages)], axis=1)
    rn = lax.broadcasted_iota(jnp.int32, (nr, nn), 0)
    cn = lax.broadcasted_iota(jnp.int32, (nr, nn), 1)
    s_new = _nt(qb, kn_ref[0].astype(BF16))
    s_new = jnp.where(_imod(_idiv(rn, t_new), H_D) == _imod(cn, H_D), s_new, NEG_BIG)
    s_new = jnp.where(_idiv(cn, H_D) <= _imod(rn, t_new), s_new, NEG_BIG)
    m = jnp.maximum(jnp.max(s_old, axis=1, keepdims=True), jnp.max(s_new, axis=1, keepdims=True))
    p_old = jnp.exp2(s_old - m)
    p_new = jnp.exp2(s_new - m)
    l = jnp.sum(p_old, axis=1, keepdims=True) + jnp.sum(p_new, axis=1, keepdims=True)
    pb = p_old.astype(BF16)
    out = _dot(p_new.astype(BF16), vn_ref[0].astype(BF16))
    for j in range(n_pages):
        out = out + _dot(pb[:, j * pr:(j + 1) * pr], v_refs[j][...].astype(BF16))
    o_r = out / l
    half = H_D * t_new
    lam = _lambda(lam_ref, lam_init)
    o = o_r[0:half, :] - lam * o_r[half:2 * half, :]
    o_ref[0] = (_rms(o, g_ref[...]) * (1.0 - lam_init)).astype(BF16)


def _attn_decode(page_table, q_rep, k_new, v_new, cache_k2, cache_v2, g_rows, lam_p, *, page, t_new, lam_init):
    n_seq, n_pages = page_table.shape
    nr = 2 * H_D * t_new
    half = H_D * t_new

    def page_spec(j):
        return pl.BlockSpec((page * H_D, DV_D), lambda b, pt: (pt[b, j], 0))

    grid_spec = pltpu.PrefetchScalarGridSpec(
        num_scalar_prefetch=1,
        grid=(n_seq,),
        in_specs=[pl.BlockSpec((1, nr, DV_D), lambda b, pt: (b, 0, 0)),
                  pl.BlockSpec((1, half, DV_D), lambda b, pt: (b, 0, 0)),
                  pl.BlockSpec((1, half, DV_D), lambda b, pt: (b, 0, 0)),
                  pl.BlockSpec((half, DV_D), lambda b, pt: (0, 0)),
                  pl.BlockSpec((4, DQK_D), lambda b, pt: (0, 0))]
                 + [page_spec(j) for j in range(n_pages)] + [page_spec(j) for j in range(n_pages)],
        out_specs=pl.BlockSpec((1, half, DV_D), lambda b, pt: (b, 0, 0)),
    )
    return pl.pallas_call(
        functools.partial(_decode_kernel, n_pages=n_pages, page=page, t_new=t_new, lam_init=lam_init),
        grid_spec=grid_spec,
        out_shape=jax.ShapeDtypeStruct((n_seq, half, DV_D), BF16),
        compiler_params=_cparams(("arbitrary",)),
        name="attn_decode",
    )(page_table, q_rep, k_new, v_new, g_rows, lam_p, *([cache_k2] * n_pages), *([cache_v2] * n_pages))


def _merge_kernel(x_ref, hm_ref, hd_ref, sgm_ref, sgd_ref, gate1_ref, sh2_ref, sc2_ref, gpost_ref, gpre_ref,
                  wpm_ref, wpd_ref, wout_ref, x1_ref, h2_ref):
    pm = _dot(hm_ref[...], wpm_ref[...])
    pd = _dot(hd_ref[...], wpd_ref[...])
    merged = sgm_ref[...].astype(F32) * pm + sgd_ref[...].astype(F32) * pd
    y = _dot(merged.astype(BF16), wout_ref[...])
    x1 = x_ref[...] + gate1_ref[...] * _rms(y, gpost_ref[...])
    x1_ref[...] = x1
    h2_ref[...] = (_rms(x1, gpre_ref[...]) * (1.0 + sc2_ref[...]) + sh2_ref[...]).astype(BF16)


def _merge(x, hm, hd, sgm, sgd, gate1, sh2, sc2, gpost, gpre, wpm, wpd, wout, *, tm, rows_per_mod):
    n = x.shape[0]
    row = lambda width: pl.BlockSpec((tm, width), lambda i: (i, 0))
    mod = _mod_spec(tm, rows_per_mod)
    return pl.pallas_call(
        _merge_kernel,
        grid=(n // tm,),
        in_specs=[row(D_MODEL), row(W_M), row(W_D), row(D_MODEL), row(D_MODEL), mod, mod, mod,
                  _const_spec((1, D_MODEL)), _const_spec((1, D_MODEL)),
                  _const_spec((W_M, D_MODEL)), _const_spec((W_D, D_MODEL)), _const_spec((D_MODEL, D_MODEL))],
        out_specs=[row(D_MODEL), row(D_MODEL)],
        out_shape=[jax.ShapeDtypeStruct((n, D_MODEL), F32), jax.ShapeDtypeStruct((n, D_MODEL), BF16)],
        compiler_params=_cparams(("arbitrary",)),
        name="merge",
    )(x, hm, hd, sgm, sgd, gate1, sh2, sc2, gpost, gpre, wpm, wpd, wout)


def _gelu_tanh(x):
    return 0.5 * x * (1.0 + jnp.tanh(math.sqrt(2.0 / math.pi) * (x + 0.044715 * (x * x * x))))


def _ffn_kernel(*refs, tm, tiles_per_seq, t_seq):
    if tiles_per_seq is None:
        (h2_ref, x1_ref, gate2_ref, gpost_ref, wup_ref, cw_ref, cb_ref, wdn_ref, p1_ref, p2_ref,
         y_ref, a_ref, a_scr) = refs
        a_scr[0:SUBLANES, :] = jnp.zeros((SUBLANES, D_FF), F32)
    else:
        (h2_ref, x1_ref, gate2_ref, gpost_ref, wup_ref, cw_ref, cb_ref, wdn_ref,
         y_ref, tail_ref, a_scr) = refs
        @pl.when(pl.program_id(0) % tiles_per_seq == 0)
        def _zero():
            a_scr[0:SUBLANES, :] = jnp.zeros((SUBLANES, D_FF), F32)

    u = _dot(h2_ref[...], wup_ref[...])
    a = u[:, 0:D_FF]
    a_scr[SUBLANES:SUBLANES + tm, :] = a
    prev1 = a_scr[SUBLANES - 1:SUBLANES - 1 + tm, :]
    prev2 = a_scr[SUBLANES - 2:SUBLANES - 2 + tm, :]
    if tiles_per_seq is None:
        tpos = lax.broadcasted_iota(jnp.int32, (tm, D_FF), 0) % t_seq
        prev1 = jnp.where(tpos >= 1, prev1, p1_ref[...])
        prev2 = jnp.where(tpos >= 2, prev2, p2_ref[...])
        a_ref[...] = a
    else:
        tail = a_scr[tm:tm + SUBLANES, :]
        tail_ref[0] = tail
        a_scr[0:SUBLANES, :] = tail
    conv = cb_ref[...] + cw_ref[0:1, :] * prev2 + cw_ref[1:2, :] * prev1 + cw_ref[2:3, :] * a
    act = (_gelu_tanh(conv) * u[:, D_FF:2 * D_FF]).astype(BF16)
    f = _dot(act, wdn_ref[...])
    y_ref[...] = x1_ref[...] + gate2_ref[...] * _rms(f, gpost_ref[...])


def _ffn(h2, x1, gate2, gpost, wup, cw, cb, wdn, p1=None, p2=None, *, tm, rows_per_mod, t_seq):
    n = h2.shape[0]
    row = lambda width: pl.BlockSpec((tm, width), lambda i: (i, 0))
    in_specs = [row(D_MODEL), row(D_MODEL), _mod_spec(tm, rows_per_mod), _const_spec((1, D_MODEL)),
                _const_spec((D_MODEL, 2 * D_FF)), _const_spec((SUBLANES, D_FF)), _const_spec((1, D_FF)),
                _const_spec((D_FF, D_MODEL))]
    args = [h2, x1, gate2, gpost, wup, cw, cb, wdn]
    if rows_per_mod is None:
        tiles_per_seq = None
        in_specs += [row(D_FF), row(D_FF)]
        args += [p1, p2]
        out_specs = [row(D_MODEL), row(D_FF)]
        out_shape = [jax.ShapeDtypeStruct((n, D_MODEL), F32), jax.ShapeDtypeStruct((n, D_FF), F32)]
    else:
        tiles_per_seq = rows_per_mod // tm
        out_specs = [row(D_MODEL), pl.BlockSpec((1, SUBLANES, D_FF), lambda i: (i // tiles_per_seq, 0, 0))]
        out_shape = [jax.ShapeDtypeStruct((n, D_MODEL), F32),
                     jax.ShapeDtypeStruct((n // rows_per_mod, SUBLANES, D_FF), F32)]
    return pl.pallas_call(
        functools.partial(_ffn_kernel, tm=tm, tiles_per_seq=tiles_per_seq, t_seq=t_seq),
        grid=(n // tm,),
        in_specs=in_specs,
        out_specs=out_specs,
        out_shape=out_shape,
        scratch_shapes=[pltpu.VMEM((tm + SUBLANES, D_FF), F32)],
        compiler_params=_cparams(("arbitrary",)),
        name="ffn",
    )(*args)


def _prep_weights(w_in, b_if, w_proj_m, w_proj_d, w_out, w_up, conv_w, conv_b, w_down):
    off = [0]
    for s in (W_M, W_M, W_M, W_M, 2 * H_M, H_D * 2 * DQK_D, H_D * 2 * DQK_D, W_D, D_MODEL, D_MODEL):
        off.append(off[-1] + s)
    mq, mk, mv, mo, mif, dq, dk, dv, gm, gd = [w_in[:, off[i]:off[i + 1]] for i in range(10)]
    zpad = jnp.zeros((D_MODEL, LANES - H_M), F32)
    w_pad = jnp.concatenate([mq, mk, mv, mo, mif[:, :H_M], zpad, mif[:, H_M:], zpad, dq, dk, dv, gm, gd],
                            axis=1).astype(BF16)
    bpad = jnp.zeros((LANES - H_M,), F32)
    bif = jnp.concatenate([b_if[:H_M], bpad, b_if[H_M:], bpad]).reshape(1, 2 * LANES)
    cw = jnp.concatenate([conv_w, jnp.zeros((SUBLANES - CONV_W, D_FF), F32)], axis=0)
    return dict(w_in=w_pad, bif=bif, wpm=w_proj_m.astype(BF16), wpd=w_proj_d.astype(BF16),
                wout=w_out.astype(BF16), wup=w_up.astype(BF16), cw=cw, cb=conv_b.reshape(1, D_FF),
                wdn=w_down.astype(BF16))


def _layer(l, pw, gains, lam_p, ada_p, ada_s, x_prompt, x_sample, cache_k, cache_v, page_table,
           state_C, state_n, state_m, state_conv, *, tm_p, l_chunk, tq, tk):
    B, T, _ = x_prompt.shape
    S, Ts, _ = x_sample.shape
    lam_init = 0.8 - 0.6 * math.exp(-0.3 * l)
    g_pre_mix, g_post_mix, g_mlstm, g_diff, g_pre_ffn, g_post_ffn = gains
    row = lambda g: g.reshape(1, -1)

    xp = x_prompt.reshape(B * T, D_MODEL)
    modp = [a.reshape(B, 1, D_MODEL) for a in jnp.split(ada_p, 6, axis=-1)]
    (mq, mk, mv, so, gt, dq, dkf, dkb, dvf, dvt, sgm, sgd) = _inproj(
        xp, modp[0], modp[1], row(g_pre_mix), pw["w_in"], pw["bif"], tm=tm_p, rows_per_mod=T, v_transposed=True)
    tok = lambda a: a.reshape(B, T, -1)
    zc = jnp.zeros((B, H_M, DH_M, DH_M), F32)
    zn = jnp.zeros((B, H_M, LANES), F32)
    hm, Cp, n_p, m_p = _mlstm(tok(mq), tok(mk), tok(mv), tok(so), tok(gt), zc, zn, zn, row(g_mlstm),
                              bb=B, L=l_chunk)
    hd = _attn_prompt(tok(dq), tok(dkb), dvt, g_diff.reshape(H_D, DV_D, 1), lam_p, tq=tq, tk=tk,
                      lam_init=lam_init)
    x1, h2 = _merge(xp, hm.reshape(B * T, W_M), hd.reshape(B * T, W_D), sgm, sgd, modp[2], modp[3], modp[4],
                    row(g_post_mix), row(g_pre_ffn), pw["wpm"], pw["wpd"], pw["wout"], tm=tm_p, rows_per_mod=T)
    yp, tail = _ffn(h2, x1, modp[5], row(g_post_ffn), pw["wup"], pw["cw"], pw["cb"], pw["wdn"],
                    tm=tm_p, rows_per_mod=T, t_seq=T)
    out_p = (yp.reshape(B, T, D_MODEL), dkf.reshape(B, T, H_D, 2 * DQK_D), dvf.reshape(B, T, H_D, DV_D),
             Cp, n_p, m_p[:, :, 0], tail[:, SUBLANES - (CONV_W - 1):, :])

    ns = S * Ts
    xs = x_sample.reshape(ns, D_MODEL)
    mods = [jnp.repeat(a, Ts, axis=0) for a in jnp.split(ada_s, 6, axis=-1)]
    (mq, mk, mv, so, gt, dq, dkf, dkb, dvf, dvb, sgm, sgd) = _inproj(
        xs, mods[0], mods[1], row(g_pre_mix), pw["w_in"], pw["bif"], tm=ns, rows_per_mod=None, v_transposed=False)
    Lp = 16
    padt = lambda a: jnp.pad(a.reshape(S, Ts, -1), ((0, 0), (0, Lp - Ts), (0, 0)))
    gt3 = gt.reshape(S, Ts, 2 * LANES)
    gpad = jnp.concatenate([jnp.full((S, Lp - Ts, LANES), NEG_BIG, F32),
                            jnp.full((S, Lp - Ts, LANES), -NEG_BIG, F32)], axis=-1)
    gt_p = jnp.concatenate([gt3, gpad], axis=1)
    m0 = jnp.broadcast_to(state_m[l][:, :, None], (S, H_M, LANES))
    hm, Cs, n_s, m_s = _mlstm(padt(mq), padt(mk), padt(mv), padt(so), gt_p, state_C[l], state_n[l], m0,
                              row(g_mlstm), bb=8, L=Lp)
    hm = hm[:, :Ts, :].reshape(ns, W_M)
    n_pool, page = cache_k.shape[1], cache_k.shape[2]
    ck2 = cache_k[l].reshape(n_pool * page * H_D, DV_D)
    cv2 = cache_v[l].reshape(n_pool * page * H_D, DV_D)
    q_rep = dq.reshape(S, Ts, H_D, DV_D).transpose(0, 2, 1, 3)
    q_rep = jnp.broadcast_to(q_rep[:, None], (S, 2, H_D, Ts, DV_D)).reshape(S, 2 * H_D * Ts, DV_D)
    new_rows = lambda a: a.reshape(S, Ts * H_D, DV_D)
    g_rows = jnp.repeat(g_diff.reshape(H_D, DV_D), Ts, axis=0)
    hd = _attn_decode(page_table, q_rep, new_rows(dkf), new_rows(dvf), ck2, cv2, g_rows, lam_p,
                      page=page, t_new=Ts, lam_init=lam_init)
    hd = hd.reshape(S, H_D, Ts, DV_D).transpose(0, 2, 1, 3).reshape(ns, W_D)
    x1, h2 = _merge(xs, hm, hd, sgm, sgd, mods[2], mods[3], mods[4], row(g_post_mix), row(g_pre_ffn),
                    pw["wpm"], pw["wpd"], pw["wout"], tm=ns, rows_per_mod=None)
    st = state_conv[l]
    zrow = jnp.zeros((S, Ts - 1, D_FF), F32)
    p1 = jnp.concatenate([st[:, 1:2], zrow], axis=1).reshape(ns, D_FF)
    p2 = jnp.concatenate([st, jnp.zeros((S, Ts - 2, D_FF), F32)], axis=1).reshape(ns, D_FF)
    ys, a_s = _ffn(h2, x1, mods[5], row(g_post_ffn), pw["wup"], pw["cw"], pw["cb"], pw["wdn"], p1, p2,
                   tm=ns, rows_per_mod=None, t_seq=Ts)
    out_s = (ys.reshape(S, Ts, D_MODEL), dkf.reshape(S, Ts, H_D, 2 * DQK_D), dvf.reshape(S, Ts, H_D, DV_D),
             Cs, n_s, m_s[:, :, 0], a_s.reshape(S, Ts, D_FF)[:, Ts - (CONV_W - 1):, :])
    return out_p, out_s


def kernel(x_prompt, x_sample, c_prompt, c_sample, cache_k, cache_v, page_table, state_C, state_n, state_m,
           state_conv, w_ada, b_ada, g_pre_mix, g_post_mix, w_in, b_if, g_mlstm, lambda_q1, lambda_k1,
           lambda_q2, lambda_k2, g_diff, w_proj_m, w_proj_d, w_out, g_pre_ffn, g_post_ffn, w_up, conv_w,
           conv_b, w_down):
    depth = w_in.shape[0]
    B = x_prompt.shape[0]
    S = x_sample.shape[0]
    pad = (-B) % SUBLANES
    c_all = jnp.concatenate([c_prompt, jnp.zeros((pad, D_MODEL), F32), c_sample], axis=0)
    xp, xs = x_prompt, x_sample
    outs_p, outs_s = [], []
    for l in range(depth):
        ada = _ada(c_all, w_ada[l], b_ada[l].reshape(1, -1))
        pw = _prep_weights(w_in[l], b_if[l], w_proj_m[l], w_proj_d[l], w_out[l], w_up[l], conv_w[l],
                           conv_b[l], w_down[l])
        gains = (g_pre_mix[l], g_post_mix[l], g_mlstm[l], g_diff[l], g_pre_ffn[l], g_post_ffn[l])
        lam_p = jnp.stack([lambda_q1[l], lambda_k1[l], lambda_q2[l], lambda_k2[l]], axis=0)
        op, os_ = _layer(l, pw, gains, lam_p, ada[:B], ada[B + pad:], xp, xs, cache_k, cache_v, page_table,
                         state_C, state_n, state_m, state_conv, tm_p=256, l_chunk=128, tq=512,
                         tk=min(1024, xp.shape[1]))
        xp, xs = op[0], os_[0]
        outs_p.append(op[1:])
        outs_s.append(os_[1:])
    stack = lambda outs, i: jnp.stack([o[i] for o in outs])
    return ((xp, xs) + tuple(stack(outs_p, i) for i in range(6)) + tuple(stack(outs_s, i) for i in range(6)))
```

```python
import functools
import math

import jax
import jax.numpy as jnp
from jax import lax
from jax.experimental import pallas as pl
from jax.experimental.pallas import tpu as pltpu

F32 = jnp.float32
BF16 = jnp.bfloat16

D_MODEL = 1024
H_M = 4
DH_M = 128
W_M = H_M * DH_M
H_D = 4
DQK_D = 64
DV_D = 2 * DQK_D
W_D = H_D * DV_D
D_FF = 2816
CONV_W = 3
EPS = 1e-6
LANES = 128
SUBLANES = 8
NEG_BIG = -1e30
LOG2E = 1.4426950408889634
C_MQ, C_MK, C_MV, C_MO = 0, 512, 1024, 1536
C_IG, C_FG = 2048, 2176
C_DQ, C_DK, C_DV = 2304, 2816, 3328
C_GM, C_GD = 3840, 4864
D_IN_PAD = 5888

VMEM_LIMIT = 56 * 1024 * 1024


def _cparams(sem):
    return pltpu.CompilerParams(dimension_semantics=sem, vmem_limit_bytes=VMEM_LIMIT)


def _const_spec(shape):
    nd = len(shape)
    return pl.BlockSpec(shape, lambda *_: (0,) * nd, pipeline_mode=pl.Buffered(1))


def _rms(x, g):
    ms = jnp.mean(x * x, axis=-1, keepdims=True)
    return x * lax.rsqrt(ms + EPS) * g


def _nt(a, b):
    return lax.dot_general(a, b, (((1,), (1,)), ((), ())), preferred_element_type=F32)


def _tn(a, b):
    return lax.dot_general(a, b, (((0,), (0,)), ((), ())), preferred_element_type=F32)


def _dot(a, b):
    return jnp.dot(a, b, preferred_element_type=F32)


def _idiv(x, n):
    assert n & (n - 1) == 0
    return lax.shift_right_logical(x, jnp.int32(n.bit_length() - 1))


def _imod(x, n):
    assert n & (n - 1) == 0
    return lax.bitwise_and(x, jnp.int32(n - 1))


def _split3(x):
    hi = x.astype(BF16)
    r1 = x - hi.astype(F32)
    mid = r1.astype(BF16)
    r2 = r1 - mid.astype(F32)
    return hi, mid, r2.astype(BF16)


def _ada_kernel(c_ref, w_ref, b_ref, o_ref):
    o_ref[...] = _dot(c_ref[...].astype(BF16), w_ref[...].astype(BF16)) + b_ref[...]


def _ada(c_all, w_ada, b_ada):
    n = c_all.shape[0]
    tn = 1024
    return pl.pallas_call(
        _ada_kernel,
        grid=(6 * D_MODEL // tn,),
        in_specs=[
            pl.BlockSpec((n, D_MODEL), lambda j: (0, 0)),
            pl.BlockSpec((D_MODEL, tn), lambda j: (0, j)),
            pl.BlockSpec((1, tn), lambda j: (0, j)),
        ],
        out_specs=pl.BlockSpec((n, tn), lambda j: (0, j)),
        out_shape=jax.ShapeDtypeStruct((n, 6 * D_MODEL), F32),
        compiler_params=_cparams(("arbitrary",)),
        name="adaln",
    )(c_all, w_ada, b_ada)


def _inproj_kernel(x_ref, sh_ref, sc_ref, g_ref, w_ref, bif_ref,
                   mq_ref, mk_ref, mv_ref, so_ref, gt_ref, dq_ref, dkf_ref, dkb_ref, dvf_ref, dvb_ref,
                   sgm_ref, sgd_ref, *, v_transposed):
    h = _rms(x_ref[...], g_ref[...]) * (1.0 + sc_ref[...]) + sh_ref[...]
    hb = h.astype(BF16)

    def seg(c0, c1):
        return _dot(hb, w_ref[:, c0:c1])

    mq_ref[...] = seg(C_MQ, C_MK).astype(BF16)
    mk_ref[...] = (seg(C_MK, C_MV) * (DH_M ** -0.5)).astype(BF16)
    mv_ref[...] = seg(C_MV, C_MO).astype(BF16)
    so_ref[...] = jax.nn.sigmoid(seg(C_MO, C_IG)).astype(BF16)
    gt_ref[...] = seg(C_IG, C_DQ) + bif_ref[...]
    dq_ref[...] = (seg(C_DQ, C_DK) * (DQK_D ** -0.5 * LOG2E)).astype(BF16)
    dk = seg(C_DK, C_DV)
    dkb_ref[...] = dk.astype(BF16)
    dv = seg(C_DV, C_GM)
    tm = dk.shape[0]
    for hh in range(H_D):
        dkf_ref[pl.ds(hh, tm, stride=H_D), :] = dk[:, hh * DV_D:(hh + 1) * DV_D]
        dvf_ref[pl.ds(hh, tm, stride=H_D), :] = dv[:, hh * DV_D:(hh + 1) * DV_D]
    if v_transposed:
        dvb_ref[0] = dv.T.astype(BF16)
    else:
        dvb_ref[...] = dv.astype(BF16)
    sgm_ref[...] = jax.nn.sigmoid(seg(C_GM, C_GD)).astype(BF16)
    sgd_ref[...] = jax.nn.sigmoid(seg(C_GD, D_IN_PAD)).astype(BF16)


def _mod_spec(tm, rows_per_mod):
    if rows_per_mod is None:
        return pl.BlockSpec((tm, D_MODEL), lambda i: (i, 0))
    tiles = rows_per_mod // tm
    return pl.BlockSpec((None, 1, D_MODEL), lambda i: (i // tiles, 0, 0))


def _inproj(x, shift, scale, g, w, bif, *, tm, rows_per_mod, v_transposed):
    n = x.shape[0]
    row = lambda width: pl.BlockSpec((tm, width), lambda i: (i, 0))
    sds = lambda width, dt: jax.ShapeDtypeStruct((n, width), dt)
    head_rows = pl.BlockSpec((tm * H_D, DV_D), lambda i: (i, 0))
    head_rows_sds = jax.ShapeDtypeStruct((n * H_D, DV_D), F32)
    if v_transposed:
        nb = n // rows_per_mod
        tiles = rows_per_mod // tm
        dvb_spec = pl.BlockSpec((1, W_D, tm), lambda i: (i // tiles, 0, i % tiles))
        dvb_sds = jax.ShapeDtypeStruct((nb, W_D, rows_per_mod), BF16)
    else:
        dvb_spec, dvb_sds = row(W_D), sds(W_D, BF16)
    return pl.pallas_call(
        functools.partial(_inproj_kernel, v_transposed=v_transposed),
        grid=(n // tm,),
        in_specs=[row(D_MODEL), _mod_spec(tm, rows_per_mod), _mod_spec(tm, rows_per_mod),
                  _const_spec((1, D_MODEL)), _const_spec((D_MODEL, D_IN_PAD)), _const_spec((1, 2 * LANES))],
        out_specs=[row(W_M), row(W_M), row(W_M), row(W_M), row(2 * LANES), row(W_D), head_rows, row(W_D),
                   head_rows, dvb_spec, row(D_MODEL), row(D_MODEL)],
        out_shape=[sds(W_M, BF16), sds(W_M, BF16), sds(W_M, BF16), sds(W_M, BF16), sds(2 * LANES, F32),
                   sds(W_D, BF16), head_rows_sds, sds(W_D, BF16), head_rows_sds, dvb_sds,
                   sds(D_MODEL, BF16), sds(D_MODEL, BF16)],
        compiler_params=_cparams(("arbitrary",)),
        name="inproj",
    )(x, shift, scale, g, w, bif)


def _mlstm_kernel(q_ref, k_ref, v_ref, so_ref, gt_ref, c0_ref, n0_ref, m0_ref, g_ref,
                  hm_ref, c_ref, n_ref, m_ref, cext, mscr, *, bb, L):
    j = pl.program_id(1)
    row128 = lax.broadcasted_iota(jnp.int32, (DH_M, DH_M), 0)

    @pl.when(j == 0)
    def _init():
        for b in range(bb):
            for h in range(H_M):
                idx = b * H_M + h
                cext[idx, 0:DH_M, :] = c0_ref[b, h]
                cext[idx, DH_M:2 * DH_M, :] = jnp.where(row128 == 0, n0_ref[b, h:h + 1, :], 0.0)
                mscr[idx] = jnp.broadcast_to(m0_ref[b, h:h + 1, :], (SUBLANES, LANES))

    rowi = lax.broadcasted_iota(jnp.int32, (L, L), 0)
    coli = lax.broadcasted_iota(jnp.int32, (L, L), 1)
    tril = rowi >= coli
    tril_b = jnp.where(tril, 1.0, 0.0).astype(BF16)
    ones_b = jnp.ones((L, 3 * LANES), BF16)
    lane = lax.broadcasted_iota(jnp.int32, (L, LANES), 1)
    e0 = jnp.where(lane == 0, 1.0, 0.0).astype(BF16)

    for b in range(bb):
        gt = gt_ref[b]
        i_col = gt[:, 0:LANES]
        f_raw = gt[:, LANES:2 * LANES]
        logf = -(jnp.maximum(-f_raw, 0.0) + jnp.log(1.0 + jnp.exp(-jnp.abs(f_raw))))
        f_hi, f_mid, f_lo = _split3(logf)
        F = _dot(tril_b, f_hi) + _dot(tril_b, f_mid) + _dot(tril_b, f_lo)
        a = i_col - F
        for h in range(H_M):
            idx = b * H_M + h
            hs = slice(h * DH_M, (h + 1) * DH_M)
            a_h = a[:, h:h + 1]
            F_h = F[:, h:h + 1]
            xh, xm, xl = _split3(jnp.where(lane == h, a, 0.0))
            A = _nt(ones_b, jnp.concatenate([xh, xm, xl], axis=1))
            cm = jnp.max(jnp.where(tril, A, -jnp.inf), axis=1, keepdims=True)
            m_prev = mscr[idx][0:1, 0:1]
            m_col = F_h + jnp.maximum(m_prev, cm)
            dmat = jnp.exp(jnp.where(tril, (F_h - m_col) + A, NEG_BIG))
            qh = q_ref[b, :, hs]
            kh = k_ref[b, :, hs]
            vext = jnp.concatenate([v_ref[b, :, hs], e0], axis=1)
            s = (_nt(qh, kh) * dmat).astype(BF16)
            nd = _dot(s, vext)
            qc = _nt(qh, cext[idx].astype(BF16))
            inter = jnp.exp(F_h + m_prev - m_col)
            num = nd[:, 0:DH_M] + inter * qc[:, 0:DH_M]
            den = nd[:, DH_M:DH_M + 1] + inter * qc[:, DH_M:DH_M + 1]
            hh = num / jnp.maximum(jnp.abs(den), jnp.exp(-m_col))
            hn = _rms(hh, g_ref[:, hs])
            hm_ref[b, :, hs] = (so_ref[b, :, hs].astype(F32) * hn).astype(BF16)
            m_end = m_col[L - 1:L, :]
            f_end = F_h[L - 1:L, :]
            w_end = jnp.exp(f_end + a_h - m_end)
            decay = jnp.exp(f_end + m_prev - m_end)
            kw = (kh.astype(F32) * w_end).astype(BF16)
            cext[idx] = decay * cext[idx] + _tn(vext, kw)
            mscr[idx] = jnp.broadcast_to(m_end, (SUBLANES, LANES))

    @pl.when(j == pl.num_programs(1) - 1)
    def _fin():
        for b in range(bb):
            for h in range(H_M):
                idx = b * H_M + h
                c_ref[b, h] = cext[idx, 0:DH_M, :]
                n_ref[b, h:h + 1, :] = cext[idx, DH_M:DH_M + 1, :]
                m_ref[b, h:h + 1, :] = mscr[idx][0:1, :]


def _mlstm(q, k, v, so, gt, c0, n0, m0, g, *, bb, L):
    B, T, _ = q.shape
    tok = lambda width: pl.BlockSpec((bb, L, width), lambda i, j: (i, j, 0))
    st3 = pl.BlockSpec((bb, H_M, LANES), lambda i, j: (i, 0, 0))
    st4 = pl.BlockSpec((bb, H_M, DH_M, DH_M), lambda i, j: (i, 0, 0, 0))
    return pl.pallas_call(
        functools.partial(_mlstm_kernel, bb=bb, L=L),
        grid=(B // bb, T // L),
        in_specs=[tok(W_M), tok(W_M), tok(W_M), tok(W_M), tok(2 * LANES), st4, st3, st3,
                  pl.BlockSpec((1, W_M), lambda i, j: (0, 0))],
        out_specs=[tok(W_M), st4, st3, st3],
        out_shape=[jax.ShapeDtypeStruct((B, T, W_M), BF16),
                   jax.ShapeDtypeStruct((B, H_M, DH_M, DH_M), F32),
                   jax.ShapeDtypeStruct((B, H_M, LANES), F32),
                   jax.ShapeDtypeStruct((B, H_M, LANES), F32)],
        scratch_shapes=[pltpu.VMEM((bb * H_M, 2 * DH_M, DH_M), F32),
                        pltpu.VMEM((bb * H_M, SUBLANES, LANES), F32)],
        compiler_params=_cparams(("arbitrary", "arbitrary")),
        name="mlstm",
    )(q, k, v, so, gt, c0, n0, m0, g)


def _lambda(lam_ref, lam_init):
    p = lam_ref[...]
    l1 = jnp.sum(p[0:1, :] * p[1:2, :], axis=-1, keepdims=True)
    l2 = jnp.sum(p[2:3, :] * p[3:4, :], axis=-1, keepdims=True)
    return jnp.exp(l1) - jnp.exp(l2) + lam_init


def _attn_kernel(q_ref, k_ref, vt_ref, g_ref, lam_ref, o_ref, qm, acc, mrow, lrow, s_a, s_b, *, tq, tk,
                 lam_init):
    qi = pl.program_id(2)
    q = q_ref[0]
    lane = lax.broadcasted_iota(jnp.int32, (tq, 2 * DQK_D), 1)
    zero = jnp.zeros_like(q)
    qm[0:tq, :] = jnp.where(lane < DQK_D, q, zero)
    qm[tq:2 * tq, :] = jnp.where(lane >= DQK_D, q, zero)
    acc[...] = jnp.zeros_like(acc)
    mrow[...] = jnp.full_like(mrow, NEG_BIG)
    lrow[...] = jnp.zeros_like(lrow)

    def scores(kb, s_buf):
        k0 = pl.multiple_of(kb * tk, tk)
        s_buf[...] = _nt(k_ref[0, pl.ds(k0, tk), :], qm[...])

    def softmax_pv(kb, s_buf, masked):
        k0 = pl.multiple_of(kb * tk, tk)
        s = s_buf[...]
        if masked:
            kv_pos = k0 + lax.broadcasted_iota(jnp.int32, (tk, 2 * tq), 0)
            ql = lax.broadcasted_iota(jnp.int32, (tk, 2 * tq), 1)
            q_pos = qi * tq + jnp.where(ql >= tq, ql - tq, ql)
            s = jnp.where(kv_pos <= q_pos, s, NEG_BIG)
        m_old = mrow[...]
        m_new = jnp.maximum(m_old, jnp.max(s, axis=0, keepdims=True))
        alpha = jnp.exp2(m_old - m_new)
        p = jnp.exp2(s - m_new)
        lrow[...] = alpha * lrow[...] + jnp.sum(p, axis=0, keepdims=True)
        acc[...] = acc[...] * alpha + _dot(vt_ref[0, :, pl.ds(k0, tk)], p.astype(BF16))
        mrow[...] = m_new

    n_full = (qi * tq) // tk
    scores(0, s_a)

    def pair(j, carry):
        kb = 2 * j
        scores(kb + 1, s_b)
        softmax_pv(kb, s_a, False)
        scores(kb + 2, s_a)
        softmax_pv(kb + 1, s_b, False)
        return carry

    lax.fori_loop(0, n_full // 2, pair, 0)
    odd = n_full % 2 == 1

    @pl.when(odd)
    def _tail_odd():
        scores(n_full, s_b)
        softmax_pv(n_full - 1, s_a, False)
        softmax_pv(n_full, s_b, True)

    @pl.when(jnp.logical_not(odd))
    def _tail_even():
        softmax_pv(n_full, s_a, True)

    l = lrow[...]
    a = acc[...]
    lam = _lambda(lam_ref, lam_init)
    o = a[:, 0:tq] / l[:, 0:tq] - lam * (a[:, tq:2 * tq] / l[:, tq:2 * tq])
    ms = jnp.mean(o * o, axis=0, keepdims=True)
    on = o * lax.rsqrt(ms + EPS) * (g_ref[0] * (1.0 - lam_init))
    o_ref[0] = on.T.astype(BF16)


def _attn_prompt(q, k, vt, g3, lam_p, *, tq, tk, lam_init):
    B, T, _ = q.shape
    return pl.pallas_call(
        functools.partial(_attn_kernel, tq=tq, tk=tk, lam_init=lam_init),
        grid=(B, H_D, T // tq),
        in_specs=[pl.BlockSpec((1, tq, DV_D), lambda b, h, i: (b, i, h)),
                  pl.BlockSpec((1, T, DV_D), lambda b, h, i: (b, 0, h)),
                  pl.BlockSpec((1, DV_D, T), lambda b, h, i: (b, h, 0)),
                  pl.BlockSpec((1, DV_D, 1), lambda b, h, i: (h, 0, 0)),
                  pl.BlockSpec((4, DQK_D), lambda b, h, i: (0, 0))],
        out_specs=pl.BlockSpec((1, tq, DV_D), lambda b, h, i: (b, i, h)),
        out_shape=jax.ShapeDtypeStruct((B, T, W_D), BF16),
        scratch_shapes=[pltpu.VMEM((2 * tq, DV_D), BF16), pltpu.VMEM((DV_D, 2 * tq), F32),
                        pltpu.VMEM((1, 2 * tq), F32), pltpu.VMEM((1, 2 * tq), F32),
                        pltpu.VMEM((tk, 2 * tq), F32), pltpu.VMEM((tk, 2 * tq), F32)],
        compiler_params=_cparams(("arbitrary", "arbitrary", "arbitrary")),
        name="attn_prompt",
    )(q, k, vt, g3, lam_p)


def _decode_kernel(pt_ref, q_ref, kn_ref, vn_ref, g_ref, lam_ref, *rest, n_pages, page, t_new, lam_init):
    k_refs = rest[:n_pages]
    v_refs = rest[n_pages:2 * n_pages]
    o_ref = rest[2 * n_pages]
    nr = 2 * H_D * t_new
    pr = page * H_D
    nn = t_new * H_D
    del pt_ref
    q = q_ref[0]
    rq = lax.broadcasted_iota(jnp.int32, (nr, DV_D), 0)
    cq = lax.broadcasted_iota(jnp.int32, (nr, DV_D), 1)
    qb = jnp.where(_idiv(rq, H_D * t_new) == _idiv(cq, DQK_D), q, jnp.zeros_like(q))

    r = lax.broadcasted_iota(jnp.int32, (nr, pr), 0)
    c = lax.broadcasted_iota(jnp.int32, (nr, pr), 1)
    head_ok = _imod(_idiv(r, t_new), H_D) == _imod(c, H_D)
    s_old = jnp.concatenate(
        [jnp.where(head_ok, _nt(qb, k_refs[j][...].astype(BF16)), NEG_BIG) for j in range(n_pages)], axis=1)
    rn = lax.broadcasted_iota(jnp.int32, (nr, nn), 0)
    cn = lax.broadcasted_iota(jnp.int32, (nr, nn), 1)
    s_new = _nt(qb, kn_ref[0].astype(BF16))
    s_new = jnp.where(_imod(_idiv(rn, t_new), H_D) == _imod(cn, H_D), s_new, NEG_BIG)
    s_new = jnp.where(_idiv(cn, H_D) <= _imod(rn, t_new), s_new, NEG_BIG)
    m = jnp.maximum(jnp.max(s_old, axis=1, keepdims=True), jnp.max(s_new, axis=1, keepdims=True))
    p_old = jnp.exp2(s_old - m)
    p_new = jnp.exp2(s_new - m)
    l = jnp.sum(p_old, axis=1, keepdims=True) + jnp.sum(p_new, axis=1, keepdims=True)
    pb = p_old.astype(BF16)
    out = _dot(p_new.astype(BF16), vn_ref[0].astype(BF16))
    for j in range(n_pages):
        out = out + _dot(pb[:, j * pr:(j + 1) * pr], v_refs[j][...].astype(BF16))
    o_r = out / l
    half = H_D * t_new
    lam = _lambda(lam_ref, lam_init)
    o = o_r[0:half, :] - lam * o_r[half:2 * half, :]
    o_ref[0] = (_rms(o, g_ref[...]) * (1.0 - lam_init)).astype(BF16)


def _attn_decode(page_table, q_rep, k_new, v_new, cache_k2, cache_v2, g_rows, lam_p, *, page, t_new, lam_init):
    n_seq, n_pages = page_table.shape
    nr = 2 * H_D * t_new
    half = H_D * t_new

    def page_spec(j):
        return pl.BlockSpec((page * H_D, DV_D), lambda b, pt: (pt[b, j], 0))

    grid_spec = pltpu.PrefetchScalarGridSpec(
        num_scalar_prefetch=1,
        grid=(n_seq,),
        in_specs=[pl.BlockSpec((1, nr, DV_D), lambda b, pt: (b, 0, 0)),
                  pl.BlockSpec((1, half, DV_D), lambda b, pt: (b, 0, 0)),
                  pl.BlockSpec((1, half, DV_D), lambda b, pt: (b, 0, 0)),
                  pl.BlockSpec((half, DV_D), lambda b, pt: (0, 0)),
                  pl.BlockSpec((4, DQK_D), lambda b, pt: (0, 0))]
                 + [page_spec(j) for j in range(n_pages)] + [page_spec(j) for j in range(n_pages)],
        out_specs=pl.BlockSpec((1, half, DV_D), lambda b, pt: (b, 0, 0)),
    )
    return pl.pallas_call(
        functools.partial(_decode_kernel, n_pages=n_pages, page=page, t_new=t_new, lam_init=lam_init),
        grid_spec=grid_spec,
        out_shape=jax.ShapeDtypeStruct((n_seq, half, DV_D), BF16),
        compiler_params=_cparams(("arbitrary",)),
        name="attn_decode",
    )(page_table, q_rep, k_new, v_new, g_rows, lam_p, *([cache_k2] * n_pages), *([cache_v2] * n_pages))


def _merge_kernel(x_ref, hm_ref, hd_ref, sgm_ref, sgd_ref, gate1_ref, sh2_ref, sc2_ref, gpost_ref, gpre_ref,
                  wpm_ref, wpd_ref, wout_ref, x1_ref, h2_ref):
    pm = _dot(hm_ref[...], wpm_ref[...])
    pd = _dot(hd_ref[...], wpd_ref[...])
    merged = sgm_ref[...].astype(F32) * pm + sgd_ref[...].astype(F32) * pd
    y = _dot(merged.astype(BF16), wout_ref[...])
    x1 = x_ref[...] + gate1_ref[...] * _rms(y, gpost_ref[...])
    x1_ref[...] = x1
    h2_ref[...] = (_rms(x1, gpre_ref[...]) * (1.0 + sc2_ref[...]) + sh2_ref[...]).astype(BF16)


def _merge(x, hm, hd, sgm, sgd, gate1, sh2, sc2, gpost, gpre, wpm, wpd, wout, *, tm, rows_per_mod):
    n = x.shape[0]
    row = lambda width: pl.BlockSpec((tm, width), lambda i: (i, 0))
    mod = _mod_spec(tm, rows_per_mod)
    return pl.pallas_call(
        _merge_kernel,
        grid=(n // tm,),
        in_specs=[row(D_MODEL), row(W_M), row(W_D), row(D_MODEL), row(D_MODEL), mod, mod, mod,
                  _const_spec((1, D_MODEL)), _const_spec((1, D_MODEL)),
                  _const_spec((W_M, D_MODEL)), _const_spec((W_D, D_MODEL)), _const_spec((D_MODEL, D_MODEL))],
        out_specs=[row(D_MODEL), row(D_MODEL)],
        out_shape=[jax.ShapeDtypeStruct((n, D_MODEL), F32), jax.ShapeDtypeStruct((n, D_MODEL), BF16)],
        compiler_params=_cparams(("arbitrary",)),
        name="merge",
    )(x, hm, hd, sgm, sgd, gate1, sh2, sc2, gpost, gpre, wpm, wpd, wout)


def _gelu_tanh(x):
    return 0.5 * x * (1.0 + jnp.tanh(math.sqrt(2.0 / math.pi) * (x + 0.044715 * (x * x * x))))


def _ffn_kernel(*refs, tm, tiles_per_seq, t_seq):
    if tiles_per_seq is None:
        (h2_ref, x1_ref, gate2_ref, gpost_ref, wup_ref, cw_ref, cb_ref, wdn_ref, p1_ref, p2_ref,
         y_ref, a_ref, a_scr) = refs
        a_scr[0:SUBLANES, :] = jnp.zeros((SUBLANES, D_FF), F32)
    else:
        (h2_ref, x1_ref, gate2_ref, gpost_ref, wup_ref, cw_ref, cb_ref, wdn_ref,
         y_ref, tail_ref, a_scr) = refs
        @pl.when(pl.program_id(0) % tiles_per_seq == 0)
        def _zero():
            a_scr[0:SUBLANES, :] = jnp.zeros((SUBLANES, D_FF), F32)

    u = _dot(h2_ref[...], wup_ref[...])
    a = u[:, 0:D_FF]
    a_scr[SUBLANES:SUBLANES + tm, :] = a
    prev1 = a_scr[SUBLANES - 1:SUBLANES - 1 + tm, :]
    prev2 = a_scr[SUBLANES - 2:SUBLANES - 2 + tm, :]
    if tiles_per_seq is None:
        tpos = lax.broadcasted_iota(jnp.int32, (tm, D_FF), 0) % t_seq
        prev1 = jnp.where(tpos >= 1, prev1, p1_ref[...])
        prev2 = jnp.where(tpos >= 2, prev2, p2_ref[...])
        a_ref[...] = a
    else:
        tail = a_scr[tm:tm + SUBLANES, :]
        tail_ref[0] = tail
        a_scr[0:SUBLANES, :] = tail
    conv = cb_ref[...] + cw_ref[0:1, :] * prev2 + cw_ref[1:2, :] * prev1 + cw_ref[2:3, :] * a
    act = (_gelu_tanh(conv) * u[:, D_FF:2 * D_FF]).astype(BF16)
    f = _dot(act, wdn_ref[...])
    y_ref[...] = x1_ref[...] + gate2_ref[...] * _rms(f, gpost_ref[...])


def _ffn(h2, x1, gate2, gpost, wup, cw, cb, wdn, p1=None, p2=None, *, tm, rows_per_mod, t_seq):
    n = h2.shape[0]
    row = lambda width: pl.BlockSpec((tm, width), lambda i: (i, 0))
    in_specs = [row(D_MODEL), row(D_MODEL), _mod_spec(tm, rows_per_mod), _const_spec((1, D_MODEL)),
                _const_spec((D_MODEL, 2 * D_FF)), _const_spec((SUBLANES, D_FF)), _const_spec((1, D_FF)),
                _const_spec((D_FF, D_MODEL))]
    args = [h2, x1, gate2, gpost, wup, cw, cb, wdn]
    if rows_per_mod is None:
        tiles_per_seq = None
        in_specs += [row(D_FF), row(D_FF)]
        args += [p1, p2]
        out_specs = [row(D_MODEL), row(D_FF)]
        out_shape = [jax.ShapeDtypeStruct((n, D_MODEL), F32), jax.ShapeDtypeStruct((n, D_FF), F32)]
    else:
        tiles_per_seq = rows_per_mod // tm
        out_specs = [row(D_MODEL), pl.BlockSpec((1, SUBLANES, D_FF), lambda i: (i // tiles_per_seq, 0, 0))]
        out_shape = [jax.ShapeDtypeStruct((n, D_MODEL), F32),
                     jax.ShapeDtypeStruct((n // rows_per_mod, SUBLANES, D_FF), F32)]
    return pl.pallas_call(
        functools.partial(_ffn_kernel, tm=tm, tiles_per_seq=tiles_per_seq, t_seq=t_seq),
        grid=(n // tm,),
        in_specs=in_specs,
        out_specs=out_specs,
        out_shape=out_shape,
        scratch_shapes=[pltpu.VMEM((tm + SUBLANES, D_FF), F32)],
        compiler_params=_cparams(("arbitrary",)),
        name="ffn",
    )(*args)


def _prep_weights(w_in, b_if, w_proj_m, w_proj_d, w_out, w_up, conv_w, conv_b, w_down):
    off = [0]
    for s in (W_M, W_M, W_M, W_M, 2 * H_M, H_D * 2 * DQK_D, H_D * 2 * DQK_D, W_D, D_MODEL, D_MODEL):
        off.append(off[-1] + s)
    mq, mk, mv, mo, mif, dq, dk, dv, gm, gd = [w_in[:, off[i]:off[i + 1]] for i in range(10)]
    zpad = jnp.zeros((D_MODEL, LANES - H_M), F32)
    w_pad = jnp.concatenate([mq, mk, mv, mo, mif[:, :H_M], zpad, mif[:, H_M:], zpad, dq, dk, dv, gm, gd],
                            axis=1).astype(BF16)
    bpad = jnp.zeros((LANES - H_M,), F32)
    bif = jnp.concatenate([b_if[:H_M], bpad, b_if[H_M:], bpad]).reshape(1, 2 * LANES)
    cw = jnp.concatenate([conv_w, jnp.zeros((SUBLANES - CONV_W, D_FF), F32)], axis=0)
    return dict(w_in=w_pad, bif=bif, wpm=w_proj_m.astype(BF16), wpd=w_proj_d.astype(BF16),
                wout=w_out.astype(BF16), wup=w_up.astype(BF16), cw=cw, cb=conv_b.reshape(1, D_FF),
                wdn=w_down.astype(BF16))


def _layer(l, pw, gains, lam_p, ada_p, ada_s, x_prompt, x_sample, cache_k, cache_v, page_table,
           state_C, state_n, state_m, state_conv, *, tm_p, l_chunk, tq, tk):
    B, T, _ = x_prompt.shape
    S, Ts, _ = x_sample.shape
    lam_init = 0.8 - 0.6 * math.exp(-0.3 * l)
    g_pre_mix, g_post_mix, g_mlstm, g_diff, g_pre_ffn, g_post_ffn = gains
    row = lambda g: g.reshape(1, -1)

    xp = x_prompt.reshape(B * T, D_MODEL)
    modp = [a.reshape(B, 1, D_MODEL) for a in jnp.split(ada_p, 6, axis=-1)]
    (mq, mk, mv, so, gt, dq, dkf, dkb, dvf, dvt, sgm, sgd) = _inproj(
        xp, modp[0], modp[1], row(g_pre_mix), pw["w_in"], pw["bif"], tm=tm_p, rows_per_mod=T, v_transposed=True)
    tok = lambda a: a.reshape(B, T, -1)
    zc = jnp.zeros((B, H_M, DH_M, DH_M), F32)
    zn = jnp.zeros((B, H_M, LANES), F32)
    hm, Cp, n_p, m_p = _mlstm(tok(mq), tok(mk), tok(mv), tok(so), tok(gt), zc, zn, zn, row(g_mlstm),
                              bb=B, L=l_chunk)
    hd = _attn_prompt(tok(dq), tok(dkb), dvt, g_diff.reshape(H_D, DV_D, 1), lam_p, tq=tq, tk=tk,
                      lam_init=lam_init)
    x1, h2 = _merge(xp, hm.reshape(B * T, W_M), hd.reshape(B * T, W_D), sgm, sgd, modp[2], modp[3], modp[4],
                    row(g_post_mix), row(g_pre_ffn), pw["wpm"], pw["wpd"], pw["wout"], tm=tm_p, rows_per_mod=T)
    yp, tail = _ffn(h2, x1, modp[5], row(g_post_ffn), pw["wup"], pw["cw"], pw["cb"], pw["wdn"],
                    tm=tm_p, rows_per_mod=T, t_seq=T)
    out_p = (yp.reshape(B, T, D_MODEL), dkf.reshape(B, T, H_D, 2 * DQK_D), dvf.reshape(B, T, H_D, DV_D),
             Cp, n_p, m_p[:, :, 0], tail[:, SUBLANES - (CONV_W - 1):, :])

    ns = S * Ts
    xs = x_sample.reshape(ns, D_MODEL)
    mods = [jnp.repeat(a, Ts, axis=0) for a in jnp.split(ada_s, 6, axis=-1)]
    (mq, mk, mv, so, gt, dq, dkf, dkb, dvf, dvb, sgm, sgd) = _inproj(
        xs, mods[0], mods[1], row(g_pre_mix), pw["w_in"], pw["bif"], tm=ns, rows_per_mod=None, v_transposed=False)
    Lp = 16
    padt = lambda a: jnp.pad(a.reshape(S, Ts, -1), ((0, 0), (0, Lp - Ts), (0, 0)))
    gt3 = gt.reshape(S, Ts, 2 * LANES)
    gpad = jnp.concatenate([jnp.full((S, Lp - Ts, LANES), NEG_BIG, F32),
                            jnp.full((S, Lp - Ts, LANES), -NEG_BIG, F32)], axis=-1)
    gt_p = jnp.concatenate([gt3, gpad], axis=1)
    m0 = jnp.broadcast_to(state_m[l][:, :, None], (S, H_M, LANES))
    hm, Cs, n_s, m_s = _mlstm(padt(mq), padt(mk), padt(mv), padt(so), gt_p, state_C[l], state_n[l], m0,
                              row(g_mlstm), bb=8, L=Lp)
    hm = hm[:, :Ts, :].reshape(ns, W_M)
    n_pool, page = cache_k.shape[1], cache_k.shape[2]
    ck2 = cache_k[l].reshape(n_pool * page * H_D, DV_D)
    cv2 = cache_v[l].reshape(n_pool * page * H_D, DV_D)
    q_rep = dq.reshape(S, Ts, H_D, DV_D).transpose(0, 2, 1, 3)
    q_rep = jnp.broadcast_to(q_rep[:, None], (S, 2, H_D, Ts, DV_D)).reshape(S, 2 * H_D * Ts, DV_D)
    new_rows = lambda a: a.reshape(S, Ts * H_D, DV_D)
    g_rows = jnp.repeat(g_diff.reshape(H_D, DV_D), Ts, axis=0)
    hd = _attn_decode(page_table, q_rep, new_rows(dkf), new_rows(dvf), ck2, cv2, g_rows, lam_p,
                      page=page, t_new=Ts, lam_init=lam_init)
    hd = hd.reshape(S, H_D, Ts, DV_D).transpose(0, 2, 1, 3).reshape(ns, W_D)
    x1, h2 = _merge(xs, hm, hd, sgm, sgd, mods[2], mods[3], mods[4], row(g_post_mix), row(g_pre_ffn),
                    pw["wpm"], pw["wpd"], pw["wout"], tm=ns, rows_per_mod=None)
    st = state_conv[l]
    zrow = jnp.zeros((S, Ts - 1, D_FF), F32)
    p1 = jnp.concatenate([st[:, 1:2], zrow], axis=1).reshape(ns, D_FF)
    p2 = jnp.concatenate([st, jnp.zeros((S, Ts - 2, D_FF), F32)], axis=1).reshape(ns, D_FF)
    ys, a_s = _ffn(h2, x1, mods[5], row(g_post_ffn), pw["wup"], pw["cw"], pw["cb"], pw["wdn"], p1, p2,
                   tm=ns, rows_per_mod=None, t_seq=Ts)
    out_s = (ys.reshape(S, Ts, D_MODEL), dkf.reshape(S, Ts, H_D, 2 * DQK_D), dvf.reshape(S, Ts, H_D, DV_D),
             Cs, n_s, m_s[:, :, 0], a_s.reshape(S, Ts, D_FF)[:, Ts - (CONV_W - 1):, :])
    return out_p, out_s


def kernel(x_prompt, x_sample, c_prompt, c_sample, cache_k, cache_v, page_table, state_C, state_n, state_m,
           state_conv, w_ada, b_ada, g_pre_mix, g_post_mix, w_in, b_if, g_mlstm, lambda_q1, lambda_k1,
           lambda_q2, lambda_k2, g_diff, w_proj_m, w_proj_d, w_out, g_pre_ffn, g_post_ffn, w_up, conv_w,
           conv_b, w_down):
    depth = w_in.shape[0]
    B = x_prompt.shape[0]
    S = x_sample.shape[0]
    pad = (-B) % SUBLANES
    c_all = jnp.concatenate([c_prompt, jnp.zeros((pad, D_MODEL), F32), c_sample], axis=0)
    xp, xs = x_prompt, x_sample
    outs_p, outs_s = [], []
    for l in range(depth):
        ada = _ada(c_all, w_ada[l], b_ada[l].reshape(1, -1))
        pw = _prep_weights(w_in[l], b_if[l], w_proj_m[l], w_proj_d[l], w_out[l], w_up[l], conv_w[l],
                           conv_b[l], w_down[l])
        gains = (g_pre_mix[l], g_post_mix[l], g_mlstm[l], g_diff[l], g_pre_ffn[l], g_post_ffn[l])
        lam_p = jnp.stack([lambda_q1[l], lambda_k1[l], lambda_q2[l], lambda_k2[l]], axis=0)
        op, os_ = _layer(l, pw, gains, lam_p, ada[:B], ada[B + pad:], xp, xs, cache_k, cache_v, page_table,
                         state_C, state_n, state_m, state_conv, tm_p=512, l_chunk=128, tq=512,
                         tk=min(1024, xp.shape[1]))
        xp, xs = op[0], os_[0]
        outs_p.append(op[1:])
        outs_s.append(os_[1:])
    stack = lambda outs, i: jnp.stack([o[i] for o in outs])
    return ((xp, xs) + tuple(stack(outs_p, i) for i in range(6)) + tuple(stack(outs_s, i) for i in range(6)))
```

```python
import functools
import math

import jax
import jax.numpy as jnp
from jax import lax
from jax.experimental import pallas as pl
from jax.experimental.pallas import tpu as pltpu

F32 = jnp.float32
BF16 = jnp.bfloat16

D_MODEL = 1024
H_M = 4
DH_M = 128
W_M = H_M * DH_M
H_D = 4
DQK_D = 64
DV_D = 2 * DQK_D
W_D = H_D * DV_D
D_FF = 2816
CONV_W = 3
EPS = 1e-6
LANES = 128
SUBLANES = 8
NEG_BIG = -1e30
LOG2E = 1.4426950408889634
C_MQ, C_MK, C_MV, C_MO = 0, 512, 1024, 1536
C_IG, C_FG = 2048, 2176
C_DQ, C_DK, C_DV = 2304, 2816, 3328
C_GM, C_GD = 3840, 4864
D_IN_PAD = 5888

VMEM_LIMIT = 56 * 1024 * 1024


def _cparams(sem):
    return pltpu.CompilerParams(dimension_semantics=sem, vmem_limit_bytes=VMEM_LIMIT)


def _const_spec(shape):
    nd = len(shape)
    return pl.BlockSpec(shape, lambda *_: (0,) * nd, pipeline_mode=pl.Buffered(1))


def _rms(x, g):
    ms = jnp.mean(x * x, axis=-1, keepdims=True)
    return x * lax.rsqrt(ms + EPS) * g


def _nt(a, b):
    return lax.dot_general(a, b, (((1,), (1,)), ((), ())), preferred_element_type=F32)


def _tn(a, b):
    return lax.dot_general(a, b, (((0,), (0,)), ((), ())), preferred_element_type=F32)


def _dot(a, b):
    return jnp.dot(a, b, preferred_element_type=F32)


def _idiv(x, n):
    assert n & (n - 1) == 0
    return lax.shift_right_logical(x, jnp.int32(n.bit_length() - 1))


def _imod(x, n):
    assert n & (n - 1) == 0
    return lax.bitwise_and(x, jnp.int32(n - 1))


def _split3(x):
    hi = x.astype(BF16)
    r1 = x - hi.astype(F32)
    mid = r1.astype(BF16)
    r2 = r1 - mid.astype(F32)
    return hi, mid, r2.astype(BF16)


def _ada_kernel(c_ref, w_ref, b_ref, o_ref):
    o_ref[...] = _dot(c_ref[...].astype(BF16), w_ref[...].astype(BF16)) + b_ref[...]


def _ada(c_all, w_ada, b_ada):
    n = c_all.shape[0]
    tn = 1024
    return pl.pallas_call(
        _ada_kernel,
        grid=(6 * D_MODEL // tn,),
        in_specs=[
            pl.BlockSpec((n, D_MODEL), lambda j: (0, 0)),
            pl.BlockSpec((D_MODEL, tn), lambda j: (0, j)),
            pl.BlockSpec((1, tn), lambda j: (0, j)),
        ],
        out_specs=pl.BlockSpec((n, tn), lambda j: (0, j)),
        out_shape=jax.ShapeDtypeStruct((n, 6 * D_MODEL), F32),
        compiler_params=_cparams(("arbitrary",)),
        name="adaln",
    )(c_all, w_ada, b_ada)


def _inproj_kernel(x_ref, sh_ref, sc_ref, g_ref, w_ref, bif_ref,
                   mq_ref, mk_ref, mv_ref, so_ref, gt_ref, dq_ref, dkf_ref, dkb_ref, dvf_ref, dvb_ref,
                   sgm_ref, sgd_ref, *, v_transposed):
    h = _rms(x_ref[...], g_ref[...]) * (1.0 + sc_ref[...]) + sh_ref[...]
    hb = h.astype(BF16)

    def seg(c0, c1):
        return _dot(hb, w_ref[:, c0:c1])

    mq_ref[...] = seg(C_MQ, C_MK).astype(BF16)
    mk_ref[...] = (seg(C_MK, C_MV) * (DH_M ** -0.5)).astype(BF16)
    mv_ref[...] = seg(C_MV, C_MO).astype(BF16)
    so_ref[...] = jax.nn.sigmoid(seg(C_MO, C_IG)).astype(BF16)
    gt_ref[...] = seg(C_IG, C_DQ) + bif_ref[...]
    dq_ref[...] = (seg(C_DQ, C_DK) * (DQK_D ** -0.5 * LOG2E)).astype(BF16)
    dk = seg(C_DK, C_DV)
    dkb_ref[...] = dk.astype(BF16)
    dv = seg(C_DV, C_GM)
    tm = dk.shape[0]
    for hh in range(H_D):
        dkf_ref[pl.ds(hh, tm, stride=H_D), :] = dk[:, hh * DV_D:(hh + 1) * DV_D]
        dvf_ref[pl.ds(hh, tm, stride=H_D), :] = dv[:, hh * DV_D:(hh + 1) * DV_D]
    if v_transposed:
        dvb_ref[0] = dv.T.astype(BF16)
    else:
        dvb_ref[...] = dv.astype(BF16)
    sgm_ref[...] = jax.nn.sigmoid(seg(C_GM, C_GD)).astype(BF16)
    sgd_ref[...] = jax.nn.sigmoid(seg(C_GD, D_IN_PAD)).astype(BF16)


def _mod_spec(tm, rows_per_mod):
    if rows_per_mod is None:
        return pl.BlockSpec((tm, D_MODEL), lambda i: (i, 0))
    tiles = rows_per_mod // tm
    return pl.BlockSpec((None, 1, D_MODEL), lambda i: (i // tiles, 0, 0))


def _inproj(x, shift, scale, g, w, bif, *, tm, rows_per_mod, v_transposed):
    n = x.shape[0]
    row = lambda width: pl.BlockSpec((tm, width), lambda i: (i, 0))
    sds = lambda width, dt: jax.ShapeDtypeStruct((n, width), dt)
    head_rows = pl.BlockSpec((tm * H_D, DV_D), lambda i: (i, 0))
    head_rows_sds = jax.ShapeDtypeStruct((n * H_D, DV_D), F32)
    if v_transposed:
        nb = n // rows_per_mod
        tiles = rows_per_mod // tm
        dvb_spec = pl.BlockSpec((1, W_D, tm), lambda i: (i // tiles, 0, i % tiles))
        dvb_sds = jax.ShapeDtypeStruct((nb, W_D, rows_per_mod), BF16)
    else:
        dvb_spec, dvb_sds = row(W_D), sds(W_D, BF16)
    return pl.pallas_call(
        functools.partial(_inproj_kernel, v_transposed=v_transposed),
        grid=(n // tm,),
        in_specs=[row(D_MODEL), _mod_spec(tm, rows_per_mod), _mod_spec(tm, rows_per_mod),
                  _const_spec((1, D_MODEL)), _const_spec((D_MODEL, D_IN_PAD)), _const_spec((1, 2 * LANES))],
        out_specs=[row(W_M), row(W_M), row(W_M), row(W_M), row(2 * LANES), row(W_D), head_rows, row(W_D),
                   head_rows, dvb_spec, row(D_MODEL), row(D_MODEL)],
        out_shape=[sds(W_M, BF16), sds(W_M, BF16), sds(W_M, BF16), sds(W_M, BF16), sds(2 * LANES, F32),
                   sds(W_D, BF16), head_rows_sds, sds(W_D, BF16), head_rows_sds, dvb_sds,
                   sds(D_MODEL, BF16), sds(D_MODEL, BF16)],
        compiler_params=_cparams(("arbitrary",)),
        name="inproj",
    )(x, shift, scale, g, w, bif)


def _mlstm_kernel(q_ref, k_ref, v_ref, so_ref, gt_ref, c0_ref, n0_ref, m0_ref, g_ref,
                  hm_ref, c_ref, n_ref, m_ref, cst, nscr, mscr, *, bb, L):
    j = pl.program_id(1)

    @pl.when(j == 0)
    def _init():
        for b in range(bb):
            for h in range(H_M):
                idx = b * H_M + h
                cst[idx] = c0_ref[b, h]
                nscr[idx] = jnp.broadcast_to(n0_ref[b, h:h + 1, :], (SUBLANES, LANES))
                mscr[idx] = jnp.broadcast_to(m0_ref[b, h:h + 1, :], (SUBLANES, LANES))

    rowi = lax.broadcasted_iota(jnp.int32, (L, L), 0)
    coli = lax.broadcasted_iota(jnp.int32, (L, L), 1)
    tril = rowi >= coli
    tril_b = jnp.where(tril, 1.0, 0.0).astype(BF16)
    ones3_b = jnp.ones((L, 3 * LANES), BF16)
    ones_b = jnp.ones((L, LANES), BF16)
    ones_sq_b = jnp.ones((DH_M, DH_M), BF16)
    lane = lax.broadcasted_iota(jnp.int32, (L, LANES), 1)

    chains = [(b, h) for b in range(bb) for h in range(H_M)]
    hsl = lambda h: slice(h * DH_M, (h + 1) * DH_M)
    rep = lambda x, h: jnp.broadcast_to(x[:, h:h + 1], (L, LANES))

    gts = [gt_ref[b] for b in range(bb)]
    logfs = [-(jnp.maximum(-g[:, LANES:], 0.0) + jnp.log(1.0 + jnp.exp(-jnp.abs(g[:, LANES:])))) for g in gts]
    i_rep = [rep(gts[b][:, 0:LANES], h) for b, h in chains]
    lf3 = [_split3(rep(logfs[b], h)) for b, h in chains]
    F = [_dot(tril_b, x[0]) + _dot(tril_b, x[1]) + _dot(tril_b, x[2]) for x in lf3]
    a = [i - f for i, f in zip(i_rep, F)]
    a3 = [_split3(jnp.where(lane == 0, x, 0.0)) for x in a]
    A = [_nt(ones3_b, jnp.concatenate(x, axis=1)) for x in a3]
    cm = [jnp.broadcast_to(jnp.max(jnp.where(tril, x, -jnp.inf), axis=1, keepdims=True), (L, LANES))
          for x in A]
    m_prev = [mscr[b * H_M + h][0:1, :] for b, h in chains]
    m_tok = [f + jnp.maximum(mp, c) for f, mp, c in zip(F, m_prev, cm)]
    dmat = [jnp.exp(jnp.where(tril, (f - m)[:, 0:L] + x, NEG_BIG)) for f, m, x in zip(F, m_tok, A)]
    q = [q_ref[b, :, hsl(h)] for b, h in chains]
    k = [k_ref[b, :, hsl(h)] for b, h in chains]
    v = [v_ref[b, :, hsl(h)] for b, h in chains]
    s = [(_nt(qq, kk) * d).astype(BF16) for qq, kk, d in zip(q, k, dmat)]
    nd = [_dot(ss, jnp.concatenate([vv, ones_b], axis=1)) for ss, vv in zip(s, v)]
    cn = [jnp.concatenate([cst[i].astype(BF16),
                           jnp.broadcast_to(nscr[i][0:1, :], (DH_M, DH_M)).astype(BF16)], axis=0)
          for i in range(len(chains))]
    qc = [_nt(qq, x) for qq, x in zip(q, cn)]
    inter = [jnp.exp(f + mp - m) for f, mp, m in zip(F, m_prev, m_tok)]
    num = [x[:, 0:DH_M] + w * y[:, 0:DH_M] for x, w, y in zip(nd, inter, qc)]
    den = [x[:, DH_M:] + w * y[:, DH_M:] for x, w, y in zip(nd, inter, qc)]
    hh = [x / jnp.maximum(jnp.abs(y), jnp.exp(-m)) for x, y, m in zip(num, den, m_tok)]
    ms = [_dot((x * x).astype(BF16), ones_sq_b) * (1.0 / DH_M) for x in hh]
    for (b, h), x, y in zip(chains, hh, ms):
        hn = x * lax.rsqrt(y + EPS) * g_ref[:, hsl(h)]
        hm_ref[b, :, hsl(h)] = (so_ref[b, :, hsl(h)].astype(F32) * hn).astype(BF16)
    m_end = [m[L - 1:L, :] for m in m_tok]
    f_end = [f[L - 1:L, :] for f in F]
    w_end = [jnp.exp(fe + x - me) for fe, x, me in zip(f_end, a, m_end)]
    decay = [jnp.exp(fe + mp - me) for fe, mp, me in zip(f_end, m_prev, m_end)]
    kw = [kk.astype(F32) * w for kk, w in zip(k, w_end)]
    upd = [_tn(vv, x.astype(BF16)) for vv, x in zip(v, kw)]
    for i in range(len(chains)):
        cst[i] = decay[i] * cst[i] + upd[i]
        n_new = decay[i] * nscr[i][0:1, :] + jnp.sum(kw[i], axis=0, keepdims=True)
        nscr[i] = jnp.broadcast_to(n_new, (SUBLANES, LANES))
        mscr[i] = jnp.broadcast_to(m_end[i], (SUBLANES, LANES))

    @pl.when(j == pl.num_programs(1) - 1)
    def _fin():
        for i, (b, h) in enumerate(chains):
            c_ref[b, h] = cst[i]
            n_ref[b, h:h + 1, :] = nscr[i][0:1, :]
            m_ref[b, h:h + 1, :] = mscr[i][0:1, :]


def _mlstm(q, k, v, so, gt, c0, n0, m0, g, *, bb, L):
    B, T, _ = q.shape
    tok = lambda width: pl.BlockSpec((bb, L, width), lambda i, j: (i, j, 0))
    st3 = pl.BlockSpec((bb, H_M, LANES), lambda i, j: (i, 0, 0))
    st4 = pl.BlockSpec((bb, H_M, DH_M, DH_M), lambda i, j: (i, 0, 0, 0))
    return pl.pallas_call(
        functools.partial(_mlstm_kernel, bb=bb, L=L),
        grid=(B // bb, T // L),
        in_specs=[tok(W_M), tok(W_M), tok(W_M), tok(W_M), tok(2 * LANES), st4, st3, st3,
                  pl.BlockSpec((1, W_M), lambda i, j: (0, 0))],
        out_specs=[tok(W_M), st4, st3, st3],
        out_shape=[jax.ShapeDtypeStruct((B, T, W_M), BF16),
                   jax.ShapeDtypeStruct((B, H_M, DH_M, DH_M), F32),
                   jax.ShapeDtypeStruct((B, H_M, LANES), F32),
                   jax.ShapeDtypeStruct((B, H_M, LANES), F32)],
        scratch_shapes=[pltpu.VMEM((bb * H_M, DH_M, DH_M), F32),
                        pltpu.VMEM((bb * H_M, SUBLANES, LANES), F32),
                        pltpu.VMEM((bb * H_M, SUBLANES, LANES), F32)],
        compiler_params=_cparams(("arbitrary", "arbitrary")),
        name="mlstm",
    )(q, k, v, so, gt, c0, n0, m0, g)


def _lambda(lam_ref, lam_init):
    p = lam_ref[...]
    l1 = jnp.sum(p[0:1, :] * p[1:2, :], axis=-1, keepdims=True)
    l2 = jnp.sum(p[2:3, :] * p[3:4, :], axis=-1, keepdims=True)
    return jnp.exp(l1) - jnp.exp(l2) + lam_init


def _attn_kernel(q_ref, k_ref, vt_ref, g_ref, lam_ref, o_ref, qm, acc, mrow, lrow, s_a, s_b, *, tq, tk,
                 lam_init):
    qi = pl.program_id(2)
    q = q_ref[0]
    lane = lax.broadcasted_iota(jnp.int32, (tq, 2 * DQK_D), 1)
    zero = jnp.zeros_like(q)
    qm[0:tq, :] = jnp.where(lane < DQK_D, q, zero)
    qm[tq:2 * tq, :] = jnp.where(lane >= DQK_D, q, zero)
    acc[...] = jnp.zeros_like(acc)
    mrow[...] = jnp.full_like(mrow, NEG_BIG)
    lrow[...] = jnp.zeros_like(lrow)

    def scores(kb, s_buf):
        k0 = pl.multiple_of(kb * tk, tk)
        s_buf[...] = _nt(k_ref[0, pl.ds(k0, tk), :], qm[...])

    def softmax_pv(kb, s_buf, masked):
        k0 = pl.multiple_of(kb * tk, tk)
        s = s_buf[...]
        if masked:
            kv_pos = k0 + lax.broadcasted_iota(jnp.int32, (tk, 2 * tq), 0)
            ql = lax.broadcasted_iota(jnp.int32, (tk, 2 * tq), 1)
            q_pos = qi * tq + jnp.where(ql >= tq, ql - tq, ql)
            s = jnp.where(kv_pos <= q_pos, s, NEG_BIG)
        m_old = mrow[...]
        m_new = jnp.maximum(m_old, jnp.max(s, axis=0, keepdims=True))
        alpha = jnp.exp2(m_old - m_new)
        p = jnp.exp2(s - m_new)
        lrow[...] = alpha * lrow[...] + jnp.sum(p, axis=0, keepdims=True)
        acc[...] = acc[...] * alpha + _dot(vt_ref[0, :, pl.ds(k0, tk)], p.astype(BF16))
        mrow[...] = m_new

    n_full = (qi * tq) // tk
    scores(0, s_a)

    def pair(j, carry):
        kb = 2 * j
        scores(kb + 1, s_b)
        softmax_pv(kb, s_a, False)
        scores(kb + 2, s_a)
        softmax_pv(kb + 1, s_b, False)
        return carry

    lax.fori_loop(0, n_full // 2, pair, 0)
    odd = n_full % 2 == 1

    @pl.when(odd)
    def _tail_odd():
        scores(n_full, s_b)
        softmax_pv(n_full - 1, s_a, False)
        softmax_pv(n_full, s_b, True)

    @pl.when(jnp.logical_not(odd))
    def _tail_even():
        softmax_pv(n_full, s_a, True)

    l = lrow[...]
    a = acc[...]
    lam = _lambda(lam_ref, lam_init)
    o = a[:, 0:tq] / l[:, 0:tq] - lam * (a[:, tq:2 * tq] / l[:, tq:2 * tq])
    ms = jnp.mean(o * o, axis=0, keepdims=True)
    on = o * lax.rsqrt(ms + EPS) * (g_ref[0] * (1.0 - lam_init))
    o_ref[0] = on.T.astype(BF16)


def _attn_prompt(q, k, vt, g3, lam_p, *, tq, tk, lam_init):
    B, T, _ = q.shape
    return pl.pallas_call(
        functools.partial(_attn_kernel, tq=tq, tk=tk, lam_init=lam_init),
        grid=(B, H_D, T // tq),
        in_specs=[pl.BlockSpec((1, tq, DV_D), lambda b, h, i: (b, i, h)),
                  pl.BlockSpec((1, T, DV_D), lambda b, h, i: (b, 0, h)),
                  pl.BlockSpec((1, DV_D, T), lambda b, h, i: (b, h, 0)),
                  pl.BlockSpec((1, DV_D, 1), lambda b, h, i: (h, 0, 0)),
                  pl.BlockSpec((4, DQK_D), lambda b, h, i: (0, 0))],
        out_specs=pl.BlockSpec((1, tq, DV_D), lambda b, h, i: (b, i, h)),
        out_shape=jax.ShapeDtypeStruct((B, T, W_D), BF16),
        scratch_shapes=[pltpu.VMEM((2 * tq, DV_D), BF16), pltpu.VMEM((DV_D, 2 * tq), F32),
                        pltpu.VMEM((1, 2 * tq), F32), pltpu.VMEM((1, 2 * tq), F32),
                        pltpu.VMEM((tk, 2 * tq), F32), pltpu.VMEM((tk, 2 * tq), F32)],
        compiler_params=_cparams(("arbitrary", "arbitrary", "arbitrary")),
        name="attn_prompt",
    )(q, k, vt, g3, lam_p)


def _decode_kernel(pt_ref, q_ref, kn_ref, vn_ref, g_ref, lam_ref, *rest, n_pages, page, t_new, lam_init):
    k_refs = rest[:n_pages]
    v_refs = rest[n_pages:2 * n_pages]
    o_ref = rest[2 * n_pages]
    nr = 2 * H_D * t_new
    pr = page * H_D
    nn = t_new * H_D
    del pt_ref
    q = q_ref[0]
    rq = lax.broadcasted_iota(jnp.int32, (nr, DV_D), 0)
    cq = lax.broadcasted_iota(jnp.int32, (nr, DV_D), 1)
    qb = jnp.where(_idiv(rq, H_D * t_new) == _idiv(cq, DQK_D), q, jnp.zeros_like(q))

    r = lax.broadcasted_iota(jnp.int32, (nr, pr), 0)
    c = lax.broadcasted_iota(jnp.int32, (nr, pr), 1)
    head_ok = _imod(_idiv(r, t_new), H_D) == _imod(c, H_D)
    s_old = jnp.concatenate(
        [jnp.where(head_ok, _nt(qb, k_refs[j][...].astype(BF16)), NEG_BIG) for j in range(n_pages)], axis=1)
    rn = lax.broadcasted_iota(jnp.int32, (nr, nn), 0)
    cn = lax.broadcasted_iota(jnp.int32, (nr, nn), 1)
    s_new = _nt(qb, kn_ref[0].astype(BF16))
    s_new = jnp.where(_imod(_idiv(rn, t_new), H_D) == _imod(cn, H_D), s_new, NEG_BIG)
    s_new = jnp.where(_idiv(cn, H_D) <= _imod(rn, t_new), s_new, NEG_BIG)
    m = jnp.maximum(jnp.max(s_old, axis=1, keepdims=True), jnp.max(s_new, axis=1, keepdims=True))
    p_old = jnp.exp2(s_old - m)
    p_new = jnp.exp2(s_new - m)
    l = jnp.sum(p_old, axis=1, keepdims=True) + jnp.sum(p_new, axis=1, keepdims=True)
    pb = p_old.astype(BF16)
    out = _dot(p_new.astype(BF16), vn_ref[0].astype(BF16))
    for j in range(n_pages):
        out = out + _dot(pb[:, j * pr:(j + 1) * pr], v_refs[j][...].astype(BF16))
    o_r = out / l
    half = H_D * t_new
    lam = _lambda(lam_ref, lam_init)
    o = o_r[0:half, :] - lam * o_r[half:2 * half, :]
    o_ref[0] = (_rms(o, g_ref[...]) * (1.0 - lam_init)).astype(BF16)


def _attn_decode(page_table, q_rep, k_new, v_new, cache_k2, cache_v2, g_rows, lam_p, *, page, t_new, lam_init):
    n_seq, n_pages = page_table.shape
    nr = 2 * H_D * t_new
    half = H_D * t_new

    def page_spec(j):
        return pl.BlockSpec((page * H_D, DV_D), lambda b, pt: (pt[b, j], 0))

    grid_spec = pltpu.PrefetchScalarGridSpec(
        num_scalar_prefetch=1,
        grid=(n_seq,),
        in_specs=[pl.BlockSpec((1, nr, DV_D), lambda b, pt: (b, 0, 0)),
                  pl.BlockSpec((1, half, DV_D), lambda b, pt: (b, 0, 0)),
                  pl.BlockSpec((1, half, DV_D), lambda b, pt: (b, 0, 0)),
                  pl.BlockSpec((half, DV_D), lambda b, pt: (0, 0)),
                  pl.BlockSpec((4, DQK_D), lambda b, pt: (0, 0))]
                 + [page_spec(j) for j in range(n_pages)] + [page_spec(j) for j in range(n_pages)],
        out_specs=pl.BlockSpec((1, half, DV_D), lambda b, pt: (b, 0, 0)),
    )
    return pl.pallas_call(
        functools.partial(_decode_kernel, n_pages=n_pages, page=page, t_new=t_new, lam_init=lam_init),
        grid_spec=grid_spec,
        out_shape=jax.ShapeDtypeStruct((n_seq, half, DV_D), BF16),
        compiler_params=_cparams(("arbitrary",)),
        name="attn_decode",
    )(page_table, q_rep, k_new, v_new, g_rows, lam_p, *([cache_k2] * n_pages), *([cache_v2] * n_pages))


def _merge_kernel(x_ref, hm_ref, hd_ref, sgm_ref, sgd_ref, gate1_ref, sh2_ref, sc2_ref, gpost_ref, gpre_ref,
                  wpm_ref, wpd_ref, wout_ref, x1_ref, h2_ref):
    pm = _dot(hm_ref[...], wpm_ref[...])
    pd = _dot(hd_ref[...], wpd_ref[...])
    merged = sgm_ref[...].astype(F32) * pm + sgd_ref[...].astype(F32) * pd
    y = _dot(merged.astype(BF16), wout_ref[...])
    x1 = x_ref[...] + gate1_ref[...] * _rms(y, gpost_ref[...])
    x1_ref[...] = x1
    h2_ref[...] = (_rms(x1, gpre_ref[...]) * (1.0 + sc2_ref[...]) + sh2_ref[...]).astype(BF16)


def _merge(x, hm, hd, sgm, sgd, gate1, sh2, sc2, gpost, gpre, wpm, wpd, wout, *, tm, rows_per_mod):
    n = x.shape[0]
    row = lambda width: pl.BlockSpec((tm, width), lambda i: (i, 0))
    mod = _mod_spec(tm, rows_per_mod)
    return pl.pallas_call(
        _merge_kernel,
        grid=(n // tm,),
        in_specs=[row(D_MODEL), row(W_M), row(W_D), row(D_MODEL), row(D_MODEL), mod, mod, mod,
                  _const_spec((1, D_MODEL)), _const_spec((1, D_MODEL)),
                  _const_spec((W_M, D_MODEL)), _const_spec((W_D, D_MODEL)), _const_spec((D_MODEL, D_MODEL))],
        out_specs=[row(D_MODEL), row(D_MODEL)],
        out_shape=[jax.ShapeDtypeStruct((n, D_MODEL), F32), jax.ShapeDtypeStruct((n, D_MODEL), BF16)],
        compiler_params=_cparams(("arbitrary",)),
        name="merge",
    )(x, hm, hd, sgm, sgd, gate1, sh2, sc2, gpost, gpre, wpm, wpd, wout)


def _gelu_tanh(x):
    return 0.5 * x * (1.0 + jnp.tanh(math.sqrt(2.0 / math.pi) * (x + 0.044715 * (x * x * x))))


def _ffn_kernel(*refs, tm, tiles_per_seq, t_seq):
    if tiles_per_seq is None:
        (h2_ref, x1_ref, gate2_ref, gpost_ref, wup_ref, cw_ref, cb_ref, wdn_ref, p1_ref, p2_ref,
         y_ref, a_ref, a_scr) = refs
        a_scr[0:SUBLANES, :] = jnp.zeros((SUBLANES, D_FF), F32)
    else:
        (h2_ref, x1_ref, gate2_ref, gpost_ref, wup_ref, cw_ref, cb_ref, wdn_ref,
         y_ref, tail_ref, a_scr) = refs
        @pl.when(pl.program_id(0) % tiles_per_seq == 0)
        def _zero():
            a_scr[0:SUBLANES, :] = jnp.zeros((SUBLANES, D_FF), F32)

    u = _dot(h2_ref[...], wup_ref[...])
    a = u[:, 0:D_FF]
    a_scr[SUBLANES:SUBLANES + tm, :] = a
    prev1 = a_scr[SUBLANES - 1:SUBLANES - 1 + tm, :]
    prev2 = a_scr[SUBLANES - 2:SUBLANES - 2 + tm, :]
    if tiles_per_seq is None:
        tpos = lax.broadcasted_iota(jnp.int32, (tm, D_FF), 0) % t_seq
        prev1 = jnp.where(tpos >= 1, prev1, p1_ref[...])
        prev2 = jnp.where(tpos >= 2, prev2, p2_ref[...])
        a_ref[...] = a
    else:
        tail = a_scr[tm:tm + SUBLANES, :]
        tail_ref[0] = tail
        a_scr[0:SUBLANES, :] = tail
    conv = cb_ref[...] + cw_ref[0:1, :] * prev2 + cw_ref[1:2, :] * prev1 + cw_ref[2:3, :] * a
    act = (_gelu_tanh(conv) * u[:, D_FF:2 * D_FF]).astype(BF16)
    f = _dot(act, wdn_ref[...])
    y_ref[...] = x1_ref[...] + gate2_ref[...] * _rms(f, gpost_ref[...])


def _ffn(h2, x1, gate2, gpost, wup, cw, cb, wdn, p1=None, p2=None, *, tm, rows_per_mod, t_seq):
    n = h2.shape[0]
    row = lambda width: pl.BlockSpec((tm, width), lambda i: (i, 0))
    in_specs = [row(D_MODEL), row(D_MODEL), _mod_spec(tm, rows_per_mod), _const_spec((1, D_MODEL)),
                _const_spec((D_MODEL, 2 * D_FF)), _const_spec((SUBLANES, D_FF)), _const_spec((1, D_FF)),
                _const_spec((D_FF, D_MODEL))]
    args = [h2, x1, gate2, gpost, wup, cw, cb, wdn]
    if rows_per_mod is None:
        tiles_per_seq = None
        in_specs += [row(D_FF), row(D_FF)]
        args += [p1, p2]
        out_specs = [row(D_MODEL), row(D_FF)]
        out_shape = [jax.ShapeDtypeStruct((n, D_MODEL), F32), jax.ShapeDtypeStruct((n, D_FF), F32)]
    else:
        tiles_per_seq = rows_per_mod // tm
        out_specs = [row(D_MODEL), pl.BlockSpec((1, SUBLANES, D_FF), lambda i: (i // tiles_per_seq, 0, 0))]
        out_shape = [jax.ShapeDtypeStruct((n, D_MODEL), F32),
                     jax.ShapeDtypeStruct((n // rows_per_mod, SUBLANES, D_FF), F32)]
    return pl.pallas_call(
        functools.partial(_ffn_kernel, tm=tm, tiles_per_seq=tiles_per_seq, t_seq=t_seq),
        grid=(n // tm,),
        in_specs=in_specs,
        out_specs=out_specs,
        out_shape=out_shape,
        scratch_shapes=[pltpu.VMEM((tm + SUBLANES, D_FF), F32)],
        compiler_params=_cparams(("arbitrary",)),
        name="ffn",
    )(*args)


def _prep_weights(w_in, b_if, w_proj_m, w_proj_d, w_out, w_up, conv_w, conv_b, w_down):
    off = [0]
    for s in (W_M, W_M, W_M, W_M, 2 * H_M, H_D * 2 * DQK_D, H_D * 2 * DQK_D, W_D, D_MODEL, D_MODEL):
        off.append(off[-1] + s)
    mq, mk, mv, mo, mif, dq, dk, dv, gm, gd = [w_in[:, off[i]:off[i + 1]] for i in range(10)]
    zpad = jnp.zeros((D_MODEL, LANES - H_M), F32)
    w_pad = jnp.concatenate([mq, mk, mv, mo, mif[:, :H_M], zpad, mif[:, H_M:], zpad, dq, dk, dv, gm, gd],
                            axis=1).astype(BF16)
    bpad = jnp.zeros((LANES - H_M,), F32)
    bif = jnp.concatenate([b_if[:H_M], bpad, b_if[H_M:], bpad]).reshape(1, 2 * LANES)
    cw = jnp.concatenate([conv_w, jnp.zeros((SUBLANES - CONV_W, D_FF), F32)], axis=0)
    return dict(w_in=w_pad, bif=bif, wpm=w_proj_m.astype(BF16), wpd=w_proj_d.astype(BF16),
                wout=w_out.astype(BF16), wup=w_up.astype(BF16), cw=cw, cb=conv_b.reshape(1, D_FF),
                wdn=w_down.astype(BF16))


def _layer(l, pw, gains, lam_p, ada_p, ada_s, x_prompt, x_sample, cache_k, cache_v, page_table,
           state_C, state_n, state_m, state_conv, *, tm_p, l_chunk, tq, tk):
    B, T, _ = x_prompt.shape
    S, Ts, _ = x_sample.shape
    lam_init = 0.8 - 0.6 * math.exp(-0.3 * l)
    g_pre_mix, g_post_mix, g_mlstm, g_diff, g_pre_ffn, g_post_ffn = gains
    row = lambda g: g.reshape(1, -1)

    xp = x_prompt.reshape(B * T, D_MODEL)
    modp = [a.reshape(B, 1, D_MODEL) for a in jnp.split(ada_p, 6, axis=-1)]
    (mq, mk, mv, so, gt, dq, dkf, dkb, dvf, dvt, sgm, sgd) = _inproj(
        xp, modp[0], modp[1], row(g_pre_mix), pw["w_in"], pw["bif"], tm=tm_p, rows_per_mod=T, v_transposed=True)
    tok = lambda a: a.reshape(B, T, -1)
    zc = jnp.zeros((B, H_M, DH_M, DH_M), F32)
    zn = jnp.zeros((B, H_M, LANES), F32)
    hm, Cp, n_p, m_p = _mlstm(tok(mq), tok(mk), tok(mv), tok(so), tok(gt), zc, zn, zn, row(g_mlstm),
                              bb=B, L=l_chunk)
    hd = _attn_prompt(tok(dq), tok(dkb), dvt, g_diff.reshape(H_D, DV_D, 1), lam_p, tq=tq, tk=tk,
                      lam_init=lam_init)
    x1, h2 = _merge(xp, hm.reshape(B * T, W_M), hd.reshape(B * T, W_D), sgm, sgd, modp[2], modp[3], modp[4],
                    row(g_post_mix), row(g_pre_ffn), pw["wpm"], pw["wpd"], pw["wout"], tm=tm_p, rows_per_mod=T)
    yp, tail = _ffn(h2, x1, modp[5], row(g_post_ffn), pw["wup"], pw["cw"], pw["cb"], pw["wdn"],
                    tm=tm_p // 2, rows_per_mod=T, t_seq=T)
    out_p = (yp.reshape(B, T, D_MODEL), dkf.reshape(B, T, H_D, 2 * DQK_D), dvf.reshape(B, T, H_D, DV_D),
             Cp, n_p, m_p[:, :, 0], tail[:, SUBLANES - (CONV_W - 1):, :])

    ns = S * Ts
    xs = x_sample.reshape(ns, D_MODEL)
    mods = [jnp.repeat(a, Ts, axis=0) for a in jnp.split(ada_s, 6, axis=-1)]
    (mq, mk, mv, so, gt, dq, dkf, dkb, dvf, dvb, sgm, sgd) = _inproj(
        xs, mods[0], mods[1], row(g_pre_mix), pw["w_in"], pw["bif"], tm=ns, rows_per_mod=None, v_transposed=False)
    Lp = 16
    padt = lambda a: jnp.pad(a.reshape(S, Ts, -1), ((0, 0), (0, Lp - Ts), (0, 0)))
    gt3 = gt.reshape(S, Ts, 2 * LANES)
    gpad = jnp.concatenate([jnp.full((S, Lp - Ts, LANES), NEG_BIG, F32),
                            jnp.full((S, Lp - Ts, LANES), -NEG_BIG, F32)], axis=-1)
    gt_p = jnp.concatenate([gt3, gpad], axis=1)
    m0 = jnp.broadcast_to(state_m[l][:, :, None], (S, H_M, LANES))
    hm, Cs, n_s, m_s = _mlstm(padt(mq), padt(mk), padt(mv), padt(so), gt_p, state_C[l], state_n[l], m0,
                              row(g_mlstm), bb=8, L=Lp)
    hm = hm[:, :Ts, :].reshape(ns, W_M)
    n_pool, page = cache_k.shape[1], cache_k.shape[2]
    ck2 = cache_k[l].reshape(n_pool * page * H_D, DV_D)
    cv2 = cache_v[l].reshape(n_pool * page * H_D, DV_D)
    q_rep = dq.reshape(S, Ts, H_D, DV_D).transpose(0, 2, 1, 3)
    q_rep = jnp.broadcast_to(q_rep[:, None], (S, 2, H_D, Ts, DV_D)).reshape(S, 2 * H_D * Ts, DV_D)
    new_rows = lambda a: a.reshape(S, Ts * H_D, DV_D)
    g_rows = jnp.repeat(g_diff.reshape(H_D, DV_D), Ts, axis=0)
    hd = _attn_decode(page_table, q_rep, new_rows(dkf), new_rows(dvf), ck2, cv2, g_rows, lam_p,
                      page=page, t_new=Ts, lam_init=lam_init)
    hd = hd.reshape(S, H_D, Ts, DV_D).transpose(0, 2, 1, 3).reshape(ns, W_D)
    x1, h2 = _merge(xs, hm, hd, sgm, sgd, mods[2], mods[3], mods[4], row(g_post_mix), row(g_pre_ffn),
                    pw["wpm"], pw["wpd"], pw["wout"], tm=ns, rows_per_mod=None)
    st = state_conv[l]
    zrow = jnp.zeros((S, Ts - 1, D_FF), F32)
    p1 = jnp.concatenate([st[:, 1:2], zrow], axis=1).reshape(ns, D_FF)
    p2 = jnp.concatenate([st, jnp.zeros((S, Ts - 2, D_FF), F32)], axis=1).reshape(ns, D_FF)
    ys, a_s = _ffn(h2, x1, mods[5], row(g_post_ffn), pw["wup"], pw["cw"], pw["cb"], pw["wdn"], p1, p2,
                   tm=ns, rows_per_mod=None, t_seq=Ts)
    out_s = (ys.reshape(S, Ts, D_MODEL), dkf.reshape(S, Ts, H_D, 2 * DQK_D), dvf.reshape(S, Ts, H_D, DV_D),
             Cs, n_s, m_s[:, :, 0], a_s.reshape(S, Ts, D_FF)[:, Ts - (CONV_W - 1):, :])
    return out_p, out_s


def kernel(x_prompt, x_sample, c_prompt, c_sample, cache_k, cache_v, page_table, state_C, state_n, state_m,
           state_conv, w_ada, b_ada, g_pre_mix, g_post_mix, w_in, b_if, g_mlstm, lambda_q1, lambda_k1,
           lambda_q2, lambda_k2, g_diff, w_proj_m, w_proj_d, w_out, g_pre_ffn, g_post_ffn, w_up, conv_w,
           conv_b, w_down):
    depth = w_in.shape[0]
    B = x_prompt.shape[0]
    S = x_sample.shape[0]
    pad = (-B) % SUBLANES
    c_all = jnp.concatenate([c_prompt, jnp.zeros((pad, D_MODEL), F32), c_sample], axis=0)
    xp, xs = x_prompt, x_sample
    outs_p, outs_s = [], []
    for l in range(depth):
        ada = _ada(c_all, w_ada[l], b_ada[l].reshape(1, -1))
        pw = _prep_weights(w_in[l], b_if[l], w_proj_m[l], w_proj_d[l], w_out[l], w_up[l], conv_w[l],
                           conv_b[l], w_down[l])
        gains = (g_pre_mix[l], g_post_mix[l], g_mlstm[l], g_diff[l], g_pre_ffn[l], g_post_ffn[l])
        lam_p = jnp.stack([lambda_q1[l], lambda_k1[l], lambda_q2[l], lambda_k2[l]], axis=0)
        op, os_ = _layer(l, pw, gains, lam_p, ada[:B], ada[B + pad:], xp, xs, cache_k, cache_v, page_table,
                         state_C, state_n, state_m, state_conv, tm_p=512, l_chunk=128, tq=512,
                         tk=min(1024, xp.shape[1]))
        xp, xs = op[0], os_[0]
        outs_p.append(op[1:])
        outs_s.append(os_[1:])
    stack = lambda outs, i: jnp.stack([o[i] for o in outs])
    return ((xp, xs) + tuple(stack(outs_p, i) for i in range(6)) + tuple(stack(outs_s, i) for i in range(6)))
```

```python
import functools
import math

import jax
import jax.numpy as jnp
from jax import lax
from jax.experimental import pallas as pl
from jax.experimental.pallas import tpu as pltpu

F32 = jnp.float32
BF16 = jnp.bfloat16

D_MODEL = 1024
H_M = 4
DH_M = 128
W_M = H_M * DH_M
H_D = 4
DQK_D = 64
DV_D = 2 * DQK_D
W_D = H_D * DV_D
D_FF = 2816
CONV_W = 3
EPS = 1e-6
LANES = 128
SUBLANES = 8
NEG_BIG = -1e30
LOG2E = 1.4426950408889634
FFN_CHUNKS = 1

C_MQ, C_MK, C_MV, C_MO = 0, 512, 1024, 1536
C_IG, C_FG = 2048, 2176
C_DQ, C_DK, C_DV = 2304, 2816, 3328
C_GM, C_GD = 3840, 4864
D_IN_PAD = 5888

VMEM_LIMIT = 56 * 1024 * 1024


def _cparams(sem):
    return pltpu.CompilerParams(dimension_semantics=sem, vmem_limit_bytes=VMEM_LIMIT)


def _const_spec(shape):
    nd = len(shape)
    return pl.BlockSpec(shape, lambda *_: (0,) * nd, pipeline_mode=pl.Buffered(1))


def _rms(x, g):
    ms = jnp.mean(x * x, axis=-1, keepdims=True)
    return x * lax.rsqrt(ms + EPS) * g


def _nt(a, b):
    return lax.dot_general(a, b, (((1,), (1,)), ((), ())), preferred_element_type=F32)


def _tn(a, b):
    return lax.dot_general(a, b, (((0,), (0,)), ((), ())), preferred_element_type=F32)


def _dot(a, b):
    return jnp.dot(a, b, preferred_element_type=F32)


def _idiv(x, n):
    assert n & (n - 1) == 0
    return lax.shift_right_logical(x, jnp.int32(n.bit_length() - 1))


def _imod(x, n):
    assert n & (n - 1) == 0
    return lax.bitwise_and(x, jnp.int32(n - 1))


def _split3(x):
    hi = x.astype(BF16)
    r1 = x - hi.astype(F32)
    mid = r1.astype(BF16)
    r2 = r1 - mid.astype(F32)
    return hi, mid, r2.astype(BF16)


def _ada_kernel(c_ref, w_ref, b_ref, o_ref):
    o_ref[...] = _dot(c_ref[...].astype(BF16), w_ref[...].astype(BF16)) + b_ref[...]


def _ada(c_all, w_ada, b_ada):
    n = c_all.shape[0]
    tn = 1024
    return pl.pallas_call(
        _ada_kernel,
        grid=(6 * D_MODEL // tn,),
        in_specs=[
            pl.BlockSpec((n, D_MODEL), lambda j: (0, 0)),
            pl.BlockSpec((D_MODEL, tn), lambda j: (0, j)),
            pl.BlockSpec((1, tn), lambda j: (0, j)),
        ],
        out_specs=pl.BlockSpec((n, tn), lambda j: (0, j)),
        out_shape=jax.ShapeDtypeStruct((n, 6 * D_MODEL), F32),
        compiler_params=_cparams(("arbitrary",)),
        name="adaln",
    )(c_all, w_ada, b_ada)


def _inproj_kernel(x_ref, sh_ref, sc_ref, g_ref, w_ref, bif_ref,
                   mq_ref, mk_ref, mv_ref, so_ref, gt_ref, dq_ref, dkf_ref, dkb_ref, dvf_ref, dvb_ref,
                   sgm_ref, sgd_ref, *, v_transposed):
    h = _rms(x_ref[...], g_ref[...]) * (1.0 + sc_ref[...]) + sh_ref[...]
    hb = h.astype(BF16)

    def seg(c0, c1):
        return _dot(hb, w_ref[:, c0:c1])

    mq_ref[...] = seg(C_MQ, C_MK).astype(BF16)
    mk_ref[...] = (seg(C_MK, C_MV) * (DH_M ** -0.5)).astype(BF16)
    mv_ref[...] = seg(C_MV, C_MO).astype(BF16)
    so_ref[...] = jax.nn.sigmoid(seg(C_MO, C_IG)).astype(BF16)
    gt_ref[...] = seg(C_IG, C_DQ) + bif_ref[...]
    dq_ref[...] = (seg(C_DQ, C_DK) * (DQK_D ** -0.5 * LOG2E)).astype(BF16)
    dk = seg(C_DK, C_DV)
    dkb_ref[...] = dk.astype(BF16)
    dv = seg(C_DV, C_GM)
    tm = dk.shape[0]
    for hh in range(H_D):
        dkf_ref[pl.ds(hh, tm, stride=H_D), :] = dk[:, hh * DV_D:(hh + 1) * DV_D]
        dvf_ref[pl.ds(hh, tm, stride=H_D), :] = dv[:, hh * DV_D:(hh + 1) * DV_D]
    if v_transposed:
        dvb_ref[0] = dv.T.astype(BF16)
    else:
        dvb_ref[...] = dv.astype(BF16)
    sgm_ref[...] = jax.nn.sigmoid(seg(C_GM, C_GD)).astype(BF16)
    sgd_ref[...] = jax.nn.sigmoid(seg(C_GD, D_IN_PAD)).astype(BF16)


def _mod_spec(tm, rows_per_mod):
    if rows_per_mod is None:
        return pl.BlockSpec((tm, D_MODEL), lambda i: (i, 0))
    tiles = rows_per_mod // tm
    return pl.BlockSpec((None, 1, D_MODEL), lambda i: (i // tiles, 0, 0))


def _inproj(x, shift, scale, g, w, bif, *, tm, rows_per_mod, v_transposed):
    n = x.shape[0]
    row = lambda width: pl.BlockSpec((tm, width), lambda i: (i, 0))
    sds = lambda width, dt: jax.ShapeDtypeStruct((n, width), dt)
    head_rows = pl.BlockSpec((tm * H_D, DV_D), lambda i: (i, 0))
    head_rows_sds = jax.ShapeDtypeStruct((n * H_D, DV_D), F32)
    if v_transposed:
        nb = n // rows_per_mod
        tiles = rows_per_mod // tm
        dvb_spec = pl.BlockSpec((1, W_D, tm), lambda i: (i // tiles, 0, i % tiles))
        dvb_sds = jax.ShapeDtypeStruct((nb, W_D, rows_per_mod), BF16)
    else:
        dvb_spec, dvb_sds = row(W_D), sds(W_D, BF16)
    return pl.pallas_call(
        functools.partial(_inproj_kernel, v_transposed=v_transposed),
        grid=(n // tm,),
        in_specs=[row(D_MODEL), _mod_spec(tm, rows_per_mod), _mod_spec(tm, rows_per_mod),
                  _const_spec((1, D_MODEL)), _const_spec((D_MODEL, D_IN_PAD)), _const_spec((1, 2 * LANES))],
        out_specs=[row(W_M), row(W_M), row(W_M), row(W_M), row(2 * LANES), row(W_D), head_rows, row(W_D),
                   head_rows, dvb_spec, row(D_MODEL), row(D_MODEL)],
        out_shape=[sds(W_M, BF16), sds(W_M, BF16), sds(W_M, BF16), sds(W_M, BF16), sds(2 * LANES, F32),
                   sds(W_D, BF16), head_rows_sds, sds(W_D, BF16), head_rows_sds, dvb_sds,
                   sds(D_MODEL, BF16), sds(D_MODEL, BF16)],
        compiler_params=_cparams(("arbitrary",)),
        name="inproj",
    )(x, shift, scale, g, w, bif)


def _mlstm_kernel(q_ref, k_ref, v_ref, so_ref, gt_ref, c0_ref, n0_ref, m0_ref, g_ref,
                  hm_ref, c_ref, n_ref, m_ref, cst, nscr, mscr, *, bb, L):
    j = pl.program_id(1)

    @pl.when(j == 0)
    def _init():
        for b in range(bb):
            for h in range(H_M):
                idx = b * H_M + h
                cst[idx] = c0_ref[b, h]
                nscr[idx] = jnp.broadcast_to(n0_ref[b, h:h + 1, :], (SUBLANES, LANES))
                mscr[idx] = jnp.broadcast_to(m0_ref[b, h:h + 1, :], (SUBLANES, LANES))

    rowi = lax.broadcasted_iota(jnp.int32, (L, L), 0)
    coli = lax.broadcasted_iota(jnp.int32, (L, L), 1)
    tril = rowi >= coli
    tril_b = jnp.where(tril, 1.0, 0.0).astype(BF16)
    ones3_b = jnp.ones((L, 3 * LANES), BF16)
    ones_b = jnp.ones((L, LANES), BF16)
    ones_sq_b = jnp.ones((DH_M, DH_M), BF16)
    lane = lax.broadcasted_iota(jnp.int32, (L, LANES), 1)

    chains = [(b, h) for b in range(bb) for h in range(H_M)]
    hsl = lambda h: slice(h * DH_M, (h + 1) * DH_M)
    rep = lambda x, h: jnp.broadcast_to(x[:, h:h + 1], (L, LANES))

    gts = [gt_ref[b] for b in range(bb)]
    logfs = [-(jnp.maximum(-g[:, LANES:], 0.0) + jnp.log(1.0 + jnp.exp(-jnp.abs(g[:, LANES:])))) for g in gts]
    i_rep = [rep(gts[b][:, 0:LANES], h) for b, h in chains]
    lf3 = [_split3(rep(logfs[b], h)) for b, h in chains]
    F = [_dot(tril_b, x[0]) + _dot(tril_b, x[1]) + _dot(tril_b, x[2]) for x in lf3]
    a = [i - f for i, f in zip(i_rep, F)]
    a3 = [_split3(jnp.where(lane == 0, x, 0.0)) for x in a]
    A = [_nt(ones3_b, jnp.concatenate(x, axis=1)) for x in a3]
    cm = [jnp.broadcast_to(jnp.max(jnp.where(tril, x, -jnp.inf), axis=1, keepdims=True), (L, LANES))
          for x in A]
    m_prev = [mscr[b * H_M + h][0:1, :] for b, h in chains]
    m_tok = [f + jnp.maximum(mp, c) for f, mp, c in zip(F, m_prev, cm)]
    dmat = [jnp.exp(jnp.where(tril, (f - m)[:, 0:L] + x, NEG_BIG)) for f, m, x in zip(F, m_tok, A)]
    q = [q_ref[b, :, hsl(h)] for b, h in chains]
    k = [k_ref[b, :, hsl(h)] for b, h in chains]
    v = [v_ref[b, :, hsl(h)] for b, h in chains]
    s = [(_nt(qq, kk) * d).astype(BF16) for qq, kk, d in zip(q, k, dmat)]
    nd = [_dot(ss, jnp.concatenate([vv, ones_b], axis=1)) for ss, vv in zip(s, v)]
    cn = [jnp.concatenate([cst[i].astype(BF16),
                           jnp.broadcast_to(nscr[i][0:1, :], (DH_M, DH_M)).astype(BF16)], axis=0)
          for i in range(len(chains))]
    qc = [_nt(qq, x) for qq, x in zip(q, cn)]
    inter = [jnp.exp(f + mp - m) for f, mp, m in zip(F, m_prev, m_tok)]
    num = [x[:, 0:DH_M] + w * y[:, 0:DH_M] for x, w, y in zip(nd, inter, qc)]
    den = [x[:, DH_M:] + w * y[:, DH_M:] for x, w, y in zip(nd, inter, qc)]
    hh = [x / jnp.maximum(jnp.abs(y), jnp.exp(-m)) for x, y, m in zip(num, den, m_tok)]
    ms = [_dot((x * x).astype(BF16), ones_sq_b) * (1.0 / DH_M) for x in hh]
    for (b, h), x, y in zip(chains, hh, ms):
        hn = x * lax.rsqrt(y + EPS) * g_ref[:, hsl(h)]
        hm_ref[b, :, hsl(h)] = (so_ref[b, :, hsl(h)].astype(F32) * hn).astype(BF16)
    m_end = [m[L - 1:L, :] for m in m_tok]
    f_end = [f[L - 1:L, :] for f in F]
    w_end = [jnp.exp(fe + x - me) for fe, x, me in zip(f_end, a, m_end)]
    decay = [jnp.exp(fe + mp - me) for fe, mp, me in zip(f_end, m_prev, m_end)]
    kw = [kk.astype(F32) * w for kk, w in zip(k, w_end)]
    upd = [_tn(vv, x.astype(BF16)) for vv, x in zip(v, kw)]
    for i in range(len(chains)):
        cst[i] = decay[i] * cst[i] + upd[i]
        n_new = decay[i] * nscr[i][0:1, :] + jnp.sum(kw[i], axis=0, keepdims=True)
        nscr[i] = jnp.broadcast_to(n_new, (SUBLANES, LANES))
        mscr[i] = jnp.broadcast_to(m_end[i], (SUBLANES, LANES))

    @pl.when(j == pl.num_programs(1) - 1)
    def _fin():
        for i, (b, h) in enumerate(chains):
            c_ref[b, h] = cst[i]
            n_ref[b, h:h + 1, :] = nscr[i][0:1, :]
            m_ref[b, h:h + 1, :] = mscr[i][0:1, :]


def _mlstm(q, k, v, so, gt, c0, n0, m0, g, *, bb, L):
    B, T, _ = q.shape
    tok = lambda width: pl.BlockSpec((bb, L, width), lambda i, j: (i, j, 0))
    st3 = pl.BlockSpec((bb, H_M, LANES), lambda i, j: (i, 0, 0))
    st4 = pl.BlockSpec((bb, H_M, DH_M, DH_M), lambda i, j: (i, 0, 0, 0))
    return pl.pallas_call(
        functools.partial(_mlstm_kernel, bb=bb, L=L),
        grid=(B // bb, T // L),
        in_specs=[tok(W_M), tok(W_M), tok(W_M), tok(W_M), tok(2 * LANES), st4, st3, st3,
                  pl.BlockSpec((1, W_M), lambda i, j: (0, 0))],
        out_specs=[tok(W_M), st4, st3, st3],
        out_shape=[jax.ShapeDtypeStruct((B, T, W_M), BF16),
                   jax.ShapeDtypeStruct((B, H_M, DH_M, DH_M), F32),
                   jax.ShapeDtypeStruct((B, H_M, LANES), F32),
                   jax.ShapeDtypeStruct((B, H_M, LANES), F32)],
        scratch_shapes=[pltpu.VMEM((bb * H_M, DH_M, DH_M), F32),
                        pltpu.VMEM((bb * H_M, SUBLANES, LANES), F32),
                        pltpu.VMEM((bb * H_M, SUBLANES, LANES), F32)],
        compiler_params=_cparams(("arbitrary", "arbitrary")),
        name="mlstm",
    )(q, k, v, so, gt, c0, n0, m0, g)


def _lambda(lam_ref, lam_init):
    p = lam_ref[...]
    l1 = jnp.sum(p[0:1, :] * p[1:2, :], axis=-1, keepdims=True)
    l2 = jnp.sum(p[2:3, :] * p[3:4, :], axis=-1, keepdims=True)
    return jnp.exp(l1) - jnp.exp(l2) + lam_init


def _attn_kernel(q_ref, k_ref, vt_ref, g_ref, lam_ref, o_ref, qm, acc, mrow, lrow, sa_ref, sb_ref, ca_ref,
                 cb_ref, *, tq, tk, lam_init):
    qi = pl.program_id(2)
    q = q_ref[0]
    lane = lax.broadcasted_iota(jnp.int32, (tq, 2 * DQK_D), 1)
    zero = jnp.zeros_like(q)
    qm[0:tq, :] = jnp.where(lane < DQK_D, q, zero)
    qm[tq:2 * tq, :] = jnp.where(lane >= DQK_D, q, zero)
    acc[...] = jnp.zeros_like(acc)
    mrow[...] = jnp.full_like(mrow, NEG_BIG)
    lrow[...] = jnp.zeros_like(lrow)

    def scores(kb, bufs):
        s_buf, c_buf = bufs
        k0 = pl.multiple_of(kb * tk, tk)
        s = _nt(k_ref[0, pl.ds(k0, tk), :], qm[...])
        s_buf[...] = s
        c_buf[...] = jnp.max(s, axis=0, keepdims=True)

    def softmax_pv(kb, bufs, masked):
        s_buf, c_buf = bufs
        k0 = pl.multiple_of(kb * tk, tk)
        s = s_buf[...]
        if masked:
            kv_pos = k0 + lax.broadcasted_iota(jnp.int32, (tk, 2 * tq), 0)
            ql = lax.broadcasted_iota(jnp.int32, (tk, 2 * tq), 1)
            q_pos = qi * tq + jnp.where(ql >= tq, ql - tq, ql)
            s = jnp.where(kv_pos <= q_pos, s, NEG_BIG)
            c = jnp.max(s, axis=0, keepdims=True)
        else:
            c = c_buf[...]
        m_old = mrow[...]
        m_new = jnp.maximum(m_old, c)
        alpha = jnp.exp2(m_old - m_new)
        p = jnp.exp2(s - m_new)
        lrow[...] = alpha * lrow[...] + jnp.sum(p, axis=0, keepdims=True)
        acc[...] = acc[...] * alpha + _dot(vt_ref[0, :, pl.ds(k0, tk)], p.astype(BF16))
        mrow[...] = m_new

    n_full = (qi * tq) // tk
    s_a, s_b = (sa_ref, ca_ref), (sb_ref, cb_ref)
    scores(0, s_a)

    def pair(j, carry):
        kb = 2 * j
        scores(kb + 1, s_b)
        softmax_pv(kb, s_a, False)
        scores(kb + 2, s_a)
        softmax_pv(kb + 1, s_b, False)
        return carry

    lax.fori_loop(0, n_full // 2, pair, 0)
    odd = n_full % 2 == 1

    @pl.when(odd)
    def _tail_odd():
        scores(n_full, s_b)
        softmax_pv(n_full - 1, s_a, False)
        softmax_pv(n_full, s_b, True)

    @pl.when(jnp.logical_not(odd))
    def _tail_even():
        softmax_pv(n_full, s_a, True)

    l = lrow[...]
    a = acc[...]
    lam = _lambda(lam_ref, lam_init)
    o = a[:, 0:tq] / l[:, 0:tq] - lam * (a[:, tq:2 * tq] / l[:, tq:2 * tq])
    ms = jnp.mean(o * o, axis=0, keepdims=True)
    on = o * lax.rsqrt(ms + EPS) * (g_ref[0] * (1.0 - lam_init))
    o_ref[0] = on.T.astype(BF16)


def _attn_prompt(q, k, vt, g3, lam_p, *, tq, tk, lam_init):
    B, T, _ = q.shape
    return pl.pallas_call(
        functools.partial(_attn_kernel, tq=tq, tk=tk, lam_init=lam_init),
        grid=(B, H_D, T // tq),
        in_specs=[pl.BlockSpec((1, tq, DV_D), lambda b, h, i: (b, i, h)),
                  pl.BlockSpec((1, T, DV_D), lambda b, h, i: (b, 0, h)),
                  pl.BlockSpec((1, DV_D, T), lambda b, h, i: (b, h, 0)),
                  pl.BlockSpec((1, DV_D, 1), lambda b, h, i: (h, 0, 0)),
                  pl.BlockSpec((4, DQK_D), lambda b, h, i: (0, 0))],
        out_specs=pl.BlockSpec((1, tq, DV_D), lambda b, h, i: (b, i, h)),
        out_shape=jax.ShapeDtypeStruct((B, T, W_D), BF16),
        scratch_shapes=[pltpu.VMEM((2 * tq, DV_D), BF16), pltpu.VMEM((DV_D, 2 * tq), F32),
                        pltpu.VMEM((1, 2 * tq), F32), pltpu.VMEM((1, 2 * tq), F32),
                        pltpu.VMEM((tk, 2 * tq), F32), pltpu.VMEM((tk, 2 * tq), F32),
                        pltpu.VMEM((1, 2 * tq), F32), pltpu.VMEM((1, 2 * tq), F32)],
        compiler_params=_cparams(("arbitrary", "arbitrary", "arbitrary")),
        name="attn_prompt",
    )(q, k, vt, g3, lam_p)


def _decode_kernel(pt_ref, q_ref, kn_ref, vn_ref, g_ref, lam_ref, *rest, n_pages, page, t_new, lam_init):
    k_refs = rest[:n_pages]
    v_refs = rest[n_pages:2 * n_pages]
    o_ref = rest[2 * n_pages]
    nr = 2 * H_D * t_new
    pr = page * H_D
    nn = t_new * H_D
    del pt_ref
    q = q_ref[0]
    rq = lax.broadcasted_iota(jnp.int32, (nr, DV_D), 0)
    cq = lax.broadcasted_iota(jnp.int32, (nr, DV_D), 1)
    qb = jnp.where(_idiv(rq, H_D * t_new) == _idiv(cq, DQK_D), q, jnp.zeros_like(q))

    r = lax.broadcasted_iota(jnp.int32, (nr, pr), 0)
    c = lax.broadcasted_iota(jnp.int32, (nr, pr), 1)
    head_ok = _imod(_idiv(r, t_new), H_D) == _imod(c, H_D)
    s_old = jnp.concatenate(
        [jnp.where(head_ok, _nt(qb, k_refs[j][...].astype(BF16)), NEG_BIG) for j in range(n_pages)], axis=1)
    rn = lax.broadcasted_iota(jnp.int32, (nr, nn), 0)
    cn = lax.broadcasted_iota(jnp.int32, (nr, nn), 1)
    s_new = _nt(qb, kn_ref[0].astype(BF16))
    s_new = jnp.where(_imod(_idiv(rn, t_new), H_D) == _imod(cn, H_D), s_new, NEG_BIG)
    s_new = jnp.where(_idiv(cn, H_D) <= _imod(rn, t_new), s_new, NEG_BIG)
    m = jnp.maximum(jnp.max(s_old, axis=1, keepdims=True), jnp.max(s_new, axis=1, keepdims=True))
    p_old = jnp.exp2(s_old - m)
    p_new = jnp.exp2(s_new - m)
    l = jnp.sum(p_old, axis=1, keepdims=True) + jnp.sum(p_new, axis=1, keepdims=True)
    pb = p_old.astype(BF16)
    out = _dot(p_new.astype(BF16), vn_ref[0].astype(BF16))
    for j in range(n_pages):
        out = out + _dot(pb[:, j * pr:(j + 1) * pr], v_refs[j][...].astype(BF16))
    o_r = out / l
    half = H_D * t_new
    lam = _lambda(lam_ref, lam_init)
    o = o_r[0:half, :] - lam * o_r[half:2 * half, :]
    o_ref[0] = (_rms(o, g_ref[...]) * (1.0 - lam_init)).astype(BF16)


def _attn_decode(page_table, q_rep, k_new, v_new, cache_k2, cache_v2, g_rows, lam_p, *, page, t_new, lam_init):
    n_seq, n_pages = page_table.shape
    nr = 2 * H_D * t_new
    half = H_D * t_new

    def page_spec(j):
        return pl.BlockSpec((page * H_D, DV_D), lambda b, pt: (pt[b, j], 0))

    grid_spec = pltpu.PrefetchScalarGridSpec(
        num_scalar_prefetch=1,
        grid=(n_seq,),
        in_specs=[pl.BlockSpec((1, nr, DV_D), lambda b, pt: (b, 0, 0)),
                  pl.BlockSpec((1, half, DV_D), lambda b, pt: (b, 0, 0)),
                  pl.BlockSpec((1, half, DV_D), lambda b, pt: (b, 0, 0)),
                  pl.BlockSpec((half, DV_D), lambda b, pt: (0, 0)),
                  pl.BlockSpec((4, DQK_D), lambda b, pt: (0, 0))]
                 + [page_spec(j) for j in range(n_pages)] + [page_spec(j) for j in range(n_pages)],
        out_specs=pl.BlockSpec((1, half, DV_D), lambda b, pt: (b, 0, 0)),
    )
    return pl.pallas_call(
        functools.partial(_decode_kernel, n_pages=n_pages, page=page, t_new=t_new, lam_init=lam_init),
        grid_spec=grid_spec,
        out_shape=jax.ShapeDtypeStruct((n_seq, half, DV_D), BF16),
        compiler_params=_cparams(("arbitrary",)),
        name="attn_decode",
    )(page_table, q_rep, k_new, v_new, g_rows, lam_p, *([cache_k2] * n_pages), *([cache_v2] * n_pages))


def _merge_kernel(x_ref, hm_ref, hd_ref, sgm_ref, sgd_ref, gate1_ref, sh2_ref, sc2_ref, gpost_ref, gpre_ref,
                  wpm_ref, wpd_ref, wout_ref, x1_ref, h2_ref):
    pm = _dot(hm_ref[...], wpm_ref[...])
    pd = _dot(hd_ref[...], wpd_ref[...])
    merged = sgm_ref[...].astype(F32) * pm + sgd_ref[...].astype(F32) * pd
    y = _dot(merged.astype(BF16), wout_ref[...])
    x1 = x_ref[...] + gate1_ref[...] * _rms(y, gpost_ref[...])
    x1_ref[...] = x1
    h2_ref[...] = (_rms(x1, gpre_ref[...]) * (1.0 + sc2_ref[...]) + sh2_ref[...]).astype(BF16)


def _merge(x, hm, hd, sgm, sgd, gate1, sh2, sc2, gpost, gpre, wpm, wpd, wout, *, tm, rows_per_mod):
    n = x.shape[0]
    row = lambda width: pl.BlockSpec((tm, width), lambda i: (i, 0))
    mod = _mod_spec(tm, rows_per_mod)
    return pl.pallas_call(
        _merge_kernel,
        grid=(n // tm,),
        in_specs=[row(D_MODEL), row(W_M), row(W_D), row(D_MODEL), row(D_MODEL), mod, mod, mod,
                  _const_spec((1, D_MODEL)), _const_spec((1, D_MODEL)),
                  _const_spec((W_M, D_MODEL)), _const_spec((W_D, D_MODEL)), _const_spec((D_MODEL, D_MODEL))],
        out_specs=[row(D_MODEL), row(D_MODEL)],
        out_shape=[jax.ShapeDtypeStruct((n, D_MODEL), F32), jax.ShapeDtypeStruct((n, D_MODEL), BF16)],
        compiler_params=_cparams(("arbitrary",)),
        name="merge",
    )(x, hm, hd, sgm, sgd, gate1, sh2, sc2, gpost, gpre, wpm, wpd, wout)


def _gelu_tanh(x):
    return 0.5 * x * (1.0 + jnp.tanh(math.sqrt(2.0 / math.pi) * (x + 0.044715 * (x * x * x))))


def _ffn_kernel(*refs, tm, tiles_per_seq, t_seq):
    if tiles_per_seq is None:
        (h2_ref, x1_ref, gate2_ref, gpost_ref, wup_ref, cw_ref, cb_ref, wdn_ref, p1_ref, p2_ref,
         y_ref, a_ref, a_scr) = refs
        a_scr[0:SUBLANES, :] = jnp.zeros((SUBLANES, D_FF), F32)
    else:
        (h2_ref, x1_ref, gate2_ref, gpost_ref, wup_ref, cw_ref, cb_ref, wdn_ref,
         y_ref, tail_ref, a_scr) = refs
        @pl.when(pl.program_id(0) % tiles_per_seq == 0)
        def _zero():
            a_scr[0:SUBLANES, :] = jnp.zeros((SUBLANES, D_FF), F32)

    h2 = h2_ref[...]
    ch = D_FF // FFN_CHUNKS
    f = None
    for c in range(FFN_CHUNKS):
        cs = slice(c * ch, (c + 1) * ch)
        a = _dot(h2, wup_ref[:, c * ch:(c + 1) * ch])
        b = _dot(h2, wup_ref[:, D_FF + c * ch:D_FF + (c + 1) * ch])
        a_scr[SUBLANES:SUBLANES + tm, cs] = a
        prev1 = a_scr[SUBLANES - 1:SUBLANES - 1 + tm, cs]
        prev2 = a_scr[SUBLANES - 2:SUBLANES - 2 + tm, cs]
        if tiles_per_seq is None:
            tpos = lax.broadcasted_iota(jnp.int32, (tm, ch), 0) % t_seq
            prev1 = jnp.where(tpos >= 1, prev1, p1_ref[:, cs])
            prev2 = jnp.where(tpos >= 2, prev2, p2_ref[:, cs])
            a_ref[:, cs] = a
        conv = cb_ref[:, cs] + cw_ref[0:1, cs] * prev2 + cw_ref[1:2, cs] * prev1 + cw_ref[2:3, cs] * a
        act = (_gelu_tanh(conv) * b).astype(BF16)
        fc = _dot(act, wdn_ref[cs, :])
        f = fc if f is None else f + fc
    if tiles_per_seq is not None:
        tail = a_scr[tm:tm + SUBLANES, :]
        tail_ref[0] = tail
        a_scr[0:SUBLANES, :] = tail
    y_ref[...] = x1_ref[...] + gate2_ref[...] * _rms(f, gpost_ref[...])


def _ffn(h2, x1, gate2, gpost, wup, cw, cb, wdn, p1=None, p2=None, *, tm, rows_per_mod, t_seq):
    n = h2.shape[0]
    row = lambda width: pl.BlockSpec((tm, width), lambda i: (i, 0))
    in_specs = [row(D_MODEL), row(D_MODEL), _mod_spec(tm, rows_per_mod), _const_spec((1, D_MODEL)),
                _const_spec((D_MODEL, 2 * D_FF)), _const_spec((SUBLANES, D_FF)), _const_spec((1, D_FF)),
                _const_spec((D_FF, D_MODEL))]
    args = [h2, x1, gate2, gpost, wup, cw, cb, wdn]
    if rows_per_mod is None:
        tiles_per_seq = None
        in_specs += [row(D_FF), row(D_FF)]
        args += [p1, p2]
        out_specs = [row(D_MODEL), row(D_FF)]
        out_shape = [jax.ShapeDtypeStruct((n, D_MODEL), F32), jax.ShapeDtypeStruct((n, D_FF), F32)]
    else:
        tiles_per_seq = rows_per_mod // tm
        out_specs = [row(D_MODEL), pl.BlockSpec((1, SUBLANES, D_FF), lambda i: (i // tiles_per_seq, 0, 0))]
        out_shape = [jax.ShapeDtypeStruct((n, D_MODEL), F32),
                     jax.ShapeDtypeStruct((n // rows_per_mod, SUBLANES, D_FF), F32)]
    return pl.pallas_call(
        functools.partial(_ffn_kernel, tm=tm, tiles_per_seq=tiles_per_seq, t_seq=t_seq),
        grid=(n // tm,),
        in_specs=in_specs,
        out_specs=out_specs,
        out_shape=out_shape,
        scratch_shapes=[pltpu.VMEM((tm + SUBLANES, D_FF), F32)],
        compiler_params=_cparams(("arbitrary",)),
        name="ffn",
    )(*args)


def _win_prep_kernel(w_ref, o_ref):
    w = w_ref[...]
    lane = lax.broadcasted_iota(jnp.int32, (w.shape[0], LANES), 1)
    win = w[:, C_IG:C_IG + LANES]
    ig = jnp.where(lane < H_M, win, 0.0)
    fg = jnp.where(lane < H_M, pltpu.roll(win, LANES - H_M, axis=1), 0.0)
    tail = w[:, C_IG + 2 * H_M:]
    o_ref[...] = jnp.concatenate([w[:, 0:C_IG], ig, fg, tail], axis=1).astype(BF16)


def _win_prep(w_in):
    d_in = w_in.shape[1]
    tr = 128
    return pl.pallas_call(
        _win_prep_kernel,
        grid=(D_MODEL // tr,),
        in_specs=[pl.BlockSpec((tr, d_in), lambda i: (i, 0))],
        out_specs=pl.BlockSpec((tr, D_IN_PAD), lambda i: (i, 0)),
        out_shape=jax.ShapeDtypeStruct((D_MODEL, D_IN_PAD), BF16),
        compiler_params=_cparams(("arbitrary",)),
        name="win_prep",
    )(w_in)


def _prep_weights(w_in, b_if, w_proj_m, w_proj_d, w_out, w_up, conv_w, conv_b, w_down):
    assert w_in.shape[1] == D_IN_PAD - 2 * (LANES - H_M)
    w_pad = _win_prep(w_in)
    bpad = jnp.zeros((LANES - H_M,), F32)
    bif = jnp.concatenate([b_if[:H_M], bpad, b_if[H_M:], bpad]).reshape(1, 2 * LANES)
    cw = jnp.concatenate([conv_w, jnp.zeros((SUBLANES - CONV_W, D_FF), F32)], axis=0)
    return dict(w_in=w_pad, bif=bif, wpm=w_proj_m.astype(BF16), wpd=w_proj_d.astype(BF16),
                wout=w_out.astype(BF16), wup=w_up.astype(BF16), cw=cw, cb=conv_b.reshape(1, D_FF),
                wdn=w_down.astype(BF16))


def _layer(l, pw, gains, lam_p, ada_p, ada_s, x_prompt, x_sample, cache_k, cache_v, page_table,
           state_C, state_n, state_m, state_conv, *, tm_p, l_chunk, tq, tk):
    B, T, _ = x_prompt.shape
    S, Ts, _ = x_sample.shape
    lam_init = 0.8 - 0.6 * math.exp(-0.3 * l)
    g_pre_mix, g_post_mix, g_mlstm, g_diff, g_pre_ffn, g_post_ffn = gains
    row = lambda g: g.reshape(1, -1)

    xp = x_prompt.reshape(B * T, D_MODEL)
    modp = [a.reshape(B, 1, D_MODEL) for a in jnp.split(ada_p, 6, axis=-1)]
    (mq, mk, mv, so, gt, dq, dkf, dkb, dvf, dvt, sgm, sgd) = _inproj(
        xp, modp[0], modp[1], row(g_pre_mix), pw["w_in"], pw["bif"], tm=tm_p, rows_per_mod=T, v_transposed=True)
    tok = lambda a: a.reshape(B, T, -1)
    zc = jnp.zeros((B, H_M, DH_M, DH_M), F32)
    zn = jnp.zeros((B, H_M, LANES), F32)
    hm, Cp, n_p, m_p = _mlstm(tok(mq), tok(mk), tok(mv), tok(so), tok(gt), zc, zn, zn, row(g_mlstm),
                              bb=B, L=l_chunk)
    hd = _attn_prompt(tok(dq), tok(dkb), dvt, g_diff.reshape(H_D, DV_D, 1), lam_p, tq=tq, tk=tk,
                      lam_init=lam_init)
    x1, h2 = _merge(xp, hm.reshape(B * T, W_M), hd.reshape(B * T, W_D), sgm, sgd, modp[2], modp[3], modp[4],
                    row(g_post_mix), row(g_pre_ffn), pw["wpm"], pw["wpd"], pw["wout"], tm=tm_p, rows_per_mod=T)
    yp, tail = _ffn(h2, x1, modp[5], row(g_post_ffn), pw["wup"], pw["cw"], pw["cb"], pw["wdn"],
                    tm=tm_p // 2, rows_per_mod=T, t_seq=T)
    out_p = (yp.reshape(B, T, D_MODEL), dkf.reshape(B, T, H_D, 2 * DQK_D), dvf.reshape(B, T, H_D, DV_D),
             Cp, n_p, m_p[:, :, 0], tail[:, SUBLANES - (CONV_W - 1):, :])

    ns = S * Ts
    xs = x_sample.reshape(ns, D_MODEL)
    mods = [jnp.repeat(a, Ts, axis=0) for a in jnp.split(ada_s, 6, axis=-1)]
    (mq, mk, mv, so, gt, dq, dkf, dkb, dvf, dvb, sgm, sgd) = _inproj(
        xs, mods[0], mods[1], row(g_pre_mix), pw["w_in"], pw["bif"], tm=ns, rows_per_mod=None, v_transposed=False)
    Lp = 16
    padt = lambda a: jnp.pad(a.reshape(S, Ts, -1), ((0, 0), (0, Lp - Ts), (0, 0)))
    gt3 = gt.reshape(S, Ts, 2 * LANES)
    gpad = jnp.concatenate([jnp.full((S, Lp - Ts, LANES), NEG_BIG, F32),
                            jnp.full((S, Lp - Ts, LANES), -NEG_BIG, F32)], axis=-1)
    gt_p = jnp.concatenate([gt3, gpad], axis=1)
    m0 = jnp.broadcast_to(state_m[l][:, :, None], (S, H_M, LANES))
    hm, Cs, n_s, m_s = _mlstm(padt(mq), padt(mk), padt(mv), padt(so), gt_p, state_C[l], state_n[l], m0,
                              row(g_mlstm), bb=8, L=Lp)
    hm = hm[:, :Ts, :].reshape(ns, W_M)
    n_pool, page = cache_k.shape[1], cache_k.shape[2]
    ck2 = cache_k[l].reshape(n_pool * page * H_D, DV_D)
    cv2 = cache_v[l].reshape(n_pool * page * H_D, DV_D)
    q_rep = dq.reshape(S, Ts, H_D, DV_D).transpose(0, 2, 1, 3)
    q_rep = jnp.broadcast_to(q_rep[:, None], (S, 2, H_D, Ts, DV_D)).reshape(S, 2 * H_D * Ts, DV_D)
    new_rows = lambda a: a.reshape(S, Ts * H_D, DV_D)
    g_rows = jnp.repeat(g_diff.reshape(H_D, DV_D), Ts, axis=0)
    hd = _attn_decode(page_table, q_rep, new_rows(dkf), new_rows(dvf), ck2, cv2, g_rows, lam_p,
                      page=page, t_new=Ts, lam_init=lam_init)
    hd = hd.reshape(S, H_D, Ts, DV_D).transpose(0, 2, 1, 3).reshape(ns, W_D)
    x1, h2 = _merge(xs, hm, hd, sgm, sgd, mods[2], mods[3], mods[4], row(g_post_mix), row(g_pre_ffn),
                    pw["wpm"], pw["wpd"], pw["wout"], tm=ns, rows_per_mod=None)
    st = state_conv[l]
    zrow = jnp.zeros((S, Ts - 1, D_FF), F32)
    p1 = jnp.concatenate([st[:, 1:2], zrow], axis=1).reshape(ns, D_FF)
    p2 = jnp.concatenate([st, jnp.zeros((S, Ts - 2, D_FF), F32)], axis=1).reshape(ns, D_FF)
    ys, a_s = _ffn(h2, x1, mods[5], row(g_post_ffn), pw["wup"], pw["cw"], pw["cb"], pw["wdn"], p1, p2,
                   tm=ns, rows_per_mod=None, t_seq=Ts)
    out_s = (ys.reshape(S, Ts, D_MODEL), dkf.reshape(S, Ts, H_D, 2 * DQK_D), dvf.reshape(S, Ts, H_D, DV_D),
             Cs, n_s, m_s[:, :, 0], a_s.reshape(S, Ts, D_FF)[:, Ts - (CONV_W - 1):, :])
    return out_p, out_s


def kernel(x_prompt, x_sample, c_prompt, c_sample, cache_k, cache_v, page_table, state_C, state_n, state_m,
           state_conv, w_ada, b_ada, g_pre_mix, g_post_mix, w_in, b_if, g_mlstm, lambda_q1, lambda_k1,
           lambda_q2, lambda_k2, g_diff, w_proj_m, w_proj_d, w_out, g_pre_ffn, g_post_ffn, w_up, conv_w,
           conv_b, w_down):
    depth = w_in.shape[0]
    B = x_prompt.shape[0]
    S = x_sample.shape[0]
    pad = (-B) % SUBLANES
    c_all = jnp.concatenate([c_prompt, jnp.zeros((pad, D_MODEL), F32), c_sample], axis=0)
    xp, xs = x_prompt, x_sample
    outs_p, outs_s = [], []
    for l in range(depth):
        ada = _ada(c_all, w_ada[l], b_ada[l].reshape(1, -1))
        pw = _prep_weights(w_in[l], b_if[l], w_proj_m[l], w_proj_d[l], w_out[l], w_up[l], conv_w[l],
                           conv_b[l], w_down[l])
        gains = (g_pre_mix[l], g_post_mix[l], g_mlstm[l], g_diff[l], g_pre_ffn[l], g_post_ffn[l])
        lam_p = jnp.stack([lambda_q1[l], lambda_k1[l], lambda_q2[l], lambda_k2[l]], axis=0)
        op, os_ = _layer(l, pw, gains, lam_p, ada[:B], ada[B + pad:], xp, xs, cache_k, cache_v, page_table,
                         state_C, state_n, state_m, state_conv, tm_p=512, l_chunk=128, tq=1024,
                         tk=min(1024, xp.shape[1]))
        xp, xs = op[0], os_[0]
        outs_p.append(op[1:])
        outs_s.append(os_[1:])
    stack = lambda outs, i: jnp.stack([o[i] for o in outs])
    return ((xp, xs) + tuple(stack(outs_p, i) for i in range(6)) + tuple(stack(outs_s, i) for i in range(6)))
```

```python
import functools
import math

import jax
import jax.numpy as jnp
from jax import lax
from jax.experimental import pallas as pl
from jax.experimental.pallas import tpu as pltpu

F32 = jnp.float32
BF16 = jnp.bfloat16

D_MODEL = 1024
H_M = 4
DH_M = 128
W_M = H_M * DH_M
H_D = 4
DQK_D = 64
DV_D = 2 * DQK_D
W_D = H_D * DV_D
D_FF = 2816
CONV_W = 3
EPS = 1e-6
LANES = 128
SUBLANES = 8
NEG_BIG = -1e30
LOG2E = 1.4426950408889634
ATTN_HEADS_PER_STEP = 1
MERGE_ROW_GROUPS = 4
FFN_CHUNKS = 1

C_MQ, C_MK, C_MV, C_MO = 0, 512, 1024, 1536
C_DQ, C_DK, C_DV = 2048, 2560, 3072
C_GM, C_GD = 3584, 4608
D_IN_ROWS = 5632
GATE_ROW0 = 2048

VMEM_LIMIT = 56 * 1024 * 1024


def _cparams(sem):
    return pltpu.CompilerParams(dimension_semantics=sem, vmem_limit_bytes=VMEM_LIMIT)


def _const_spec(shape):
    nd = len(shape)
    return pl.BlockSpec(shape, lambda *_: (0,) * nd, pipeline_mode=pl.Buffered(1))


def _rms(x, g):
    ms = jnp.mean(x * x, axis=-1, keepdims=True)
    return x * lax.rsqrt(ms + EPS) * g


def _nt(a, b):
    return lax.dot_general(a, b, (((1,), (1,)), ((), ())), preferred_element_type=F32)


def _tn(a, b):
    return lax.dot_general(a, b, (((0,), (0,)), ((), ())), preferred_element_type=F32)


def _dot(a, b):
    return jnp.dot(a, b, preferred_element_type=F32)


def _idiv(x, n):
    assert n & (n - 1) == 0
    return lax.shift_right_logical(x, jnp.int32(n.bit_length() - 1))


def _imod(x, n):
    assert n & (n - 1) == 0
    return lax.bitwise_and(x, jnp.int32(n - 1))


def _split3(x):
    hi = x.astype(BF16)
    r1 = x - hi.astype(F32)
    mid = r1.astype(BF16)
    r2 = r1 - mid.astype(F32)
    return hi, mid, r2.astype(BF16)


def _select_rows(sel_b, parts):
    return _dot(sel_b, parts[0]) + _dot(sel_b, parts[1]) + _dot(sel_b, parts[2])


def _ada_kernel(c_ref, w_ref, b_ref, o_ref):
    o_ref[...] = _dot(c_ref[...].astype(BF16), w_ref[...].astype(BF16)) + b_ref[...]


def _ada(c_all, w_ada, b_ada):
    n = c_all.shape[0]
    tn = 1024
    return pl.pallas_call(
        _ada_kernel,
        grid=(6 * D_MODEL // tn,),
        in_specs=[
            pl.BlockSpec((n, D_MODEL), lambda j: (0, 0)),
            pl.BlockSpec((D_MODEL, tn), lambda j: (0, j)),
            pl.BlockSpec((1, tn), lambda j: (0, j)),
        ],
        out_specs=pl.BlockSpec((n, tn), lambda j: (0, j)),
        out_shape=jax.ShapeDtypeStruct((n, 6 * D_MODEL), F32),
        compiler_params=_cparams(("arbitrary",)),
        name="adaln",
    )(c_all, w_ada, b_ada)


def _inproj_kernel(x_ref, sh_ref, sc_ref, g_ref, w_ref, wg_ref, bif_ref,
                   mq_ref, mk_ref, mv_ref, so_ref, gt_ref, dq_ref, dkf_ref, dkb_ref, dvf_ref, dvb_ref,
                   sgm_ref, sgd_ref, *, v_transposed):
    h = _rms(x_ref[...], g_ref[...]) * (1.0 + sc_ref[...]) + sh_ref[...]
    hb = h.astype(BF16)

    def seg(c0, c1):
        return _nt(hb, w_ref[c0:c1, :])

    mq_ref[...] = seg(C_MQ, C_MK).astype(BF16)
    mk_ref[...] = (seg(C_MK, C_MV) * (DH_M ** -0.5)).astype(BF16)
    mv_ref[...] = seg(C_MV, C_MO).astype(BF16)
    so_ref[...] = jax.nn.sigmoid(seg(C_MO, C_DQ)).astype(BF16)
    gt_ref[...] = _nt(hb, wg_ref[...]) + bif_ref[...]
    dq_ref[...] = (seg(C_DQ, C_DK) * (DQK_D ** -0.5 * LOG2E)).astype(BF16)
    dk = seg(C_DK, C_DV)
    dkb_ref[...] = dk.astype(BF16)
    dv = seg(C_DV, C_GM)
    tm = dk.shape[0]
    for hh in range(H_D):
        dkf_ref[pl.ds(hh, tm, stride=H_D), :] = dk[:, hh * DV_D:(hh + 1) * DV_D]
        dvf_ref[pl.ds(hh, tm, stride=H_D), :] = dv[:, hh * DV_D:(hh + 1) * DV_D]
    if v_transposed:
        dvb_ref[0] = dv.T.astype(BF16)
    else:
        dvb_ref[...] = dv.astype(BF16)
    sgm_ref[...] = jax.nn.sigmoid(seg(C_GM, C_GD)).astype(BF16)
    sgd_ref[...] = jax.nn.sigmoid(seg(C_GD, D_IN_ROWS)).astype(BF16)


def _mod_spec(tm, rows_per_mod):
    if rows_per_mod is None:
        return pl.BlockSpec((tm, D_MODEL), lambda i: (i, 0))
    tiles = rows_per_mod // tm
    return pl.BlockSpec((None, 1, D_MODEL), lambda i: (i // tiles, 0, 0))


def _inproj(x, shift, scale, g, w, wg, bif, *, tm, rows_per_mod, v_transposed):
    n = x.shape[0]
    row = lambda width: pl.BlockSpec((tm, width), lambda i: (i, 0))
    sds = lambda width, dt: jax.ShapeDtypeStruct((n, width), dt)
    head_rows = pl.BlockSpec((tm * H_D, DV_D), lambda i: (i, 0))
    head_rows_sds = jax.ShapeDtypeStruct((n * H_D, DV_D), F32)
    if v_transposed:
        nb = n // rows_per_mod
        tiles = rows_per_mod // tm
        dvb_spec = pl.BlockSpec((1, W_D, tm), lambda i: (i // tiles, 0, i % tiles))
        dvb_sds = jax.ShapeDtypeStruct((nb, W_D, rows_per_mod), BF16)
    else:
        dvb_spec, dvb_sds = row(W_D), sds(W_D, BF16)
    return pl.pallas_call(
        functools.partial(_inproj_kernel, v_transposed=v_transposed),
        grid=(n // tm,),
        in_specs=[row(D_MODEL), _mod_spec(tm, rows_per_mod), _mod_spec(tm, rows_per_mod),
                  _const_spec((1, D_MODEL)), _const_spec((D_IN_ROWS, D_MODEL)),
                  _const_spec((2 * LANES, D_MODEL)), _const_spec((1, 2 * LANES))],
        out_specs=[row(W_M), row(W_M), row(W_M), row(W_M), row(2 * LANES), row(W_D), head_rows, row(W_D),
                   head_rows, dvb_spec, row(D_MODEL), row(D_MODEL)],
        out_shape=[sds(W_M, BF16), sds(W_M, BF16), sds(W_M, BF16), sds(W_M, BF16), sds(2 * LANES, F32),
                   sds(W_D, BF16), head_rows_sds, sds(W_D, BF16), head_rows_sds, dvb_sds,
                   sds(D_MODEL, BF16), sds(D_MODEL, BF16)],
        compiler_params=_cparams(("arbitrary",)),
        name="inproj",
    )(x, shift, scale, g, w, wg, bif)


def _mlstm_kernel(q_ref, k_ref, v_ref, so_ref, gt_ref, c0_ref, n0_ref, m0_ref, g_ref,
                  hm_ref, c_ref, n_ref, m_ref, cst, nscr, mscr, *, bb, L):
    j = pl.program_id(1)

    @pl.when(j == 0)
    def _init():
        for b in range(bb):
            for h in range(H_M):
                idx = b * H_M + h
                cst[idx] = c0_ref[b, h]
                nscr[idx] = jnp.broadcast_to(n0_ref[b, h:h + 1, :], (SUBLANES, LANES))
                mscr[idx] = jnp.broadcast_to(m0_ref[b, h:h + 1, :], (SUBLANES, LANES))

    rowi = lax.broadcasted_iota(jnp.int32, (L, L), 0)
    coli = lax.broadcasted_iota(jnp.int32, (L, L), 1)
    tril = rowi >= coli
    tril_b = jnp.where(tril, 1.0, 0.0).astype(BF16)
    ones3_b = jnp.ones((L, 3 * LANES), BF16)
    ones_b = jnp.ones((L, LANES), BF16)
    ones_sq_b = jnp.ones((DH_M, DH_M), BF16)
    lane = lax.broadcasted_iota(jnp.int32, (L, LANES), 1)

    chains = [(b, h) for b in range(bb) for h in range(H_M)]
    hsl = lambda h: slice(h * DH_M, (h + 1) * DH_M)
    rep = lambda x, h: jnp.broadcast_to(x[:, h:h + 1], (L, LANES))

    gts = [gt_ref[b] for b in range(bb)]
    logfs = [-(jnp.maximum(-g[:, LANES:], 0.0) + jnp.log(1.0 + jnp.exp(-jnp.abs(g[:, LANES:])))) for g in gts]
    i_rep = [rep(gts[b][:, 0:LANES], h) for b, h in chains]
    lf3 = [_split3(rep(logfs[b], h)) for b, h in chains]
    F = [_dot(tril_b, x[0]) + _dot(tril_b, x[1]) + _dot(tril_b, x[2]) for x in lf3]
    a = [i - f for i, f in zip(i_rep, F)]
    a3 = [_split3(jnp.where(lane == 0, x, 0.0)) for x in a]
    A = [_nt(ones3_b, jnp.concatenate(x, axis=1)) for x in a3]
    cm = [jnp.broadcast_to(jnp.max(jnp.where(tril, x, -jnp.inf), axis=1, keepdims=True), (L, LANES))
          for x in A]
    m_prev = [mscr[b * H_M + h][0:1, :] for b, h in chains]
    m_tok = [f + jnp.maximum(mp, c) for f, mp, c in zip(F, m_prev, cm)]
    dmat = [jnp.exp(jnp.where(tril, (f - m)[:, 0:L] + x, NEG_BIG)) for f, m, x in zip(F, m_tok, A)]
    q = [q_ref[b, :, hsl(h)] for b, h in chains]
    k = [k_ref[b, :, hsl(h)] for b, h in chains]
    v = [v_ref[b, :, hsl(h)] for b, h in chains]
    s = [(_nt(qq, kk) * d).astype(BF16) for qq, kk, d in zip(q, k, dmat)]
    nd = [_dot(ss, jnp.concatenate([vv, ones_b], axis=1)) for ss, vv in zip(s, v)]
    cn = [jnp.concatenate([cst[i].astype(BF16),
                           jnp.broadcast_to(nscr[i][0:1, :], (DH_M, DH_M)).astype(BF16)], axis=0)
          for i in range(len(chains))]
    qc = [_nt(qq, x) for qq, x in zip(q, cn)]
    inter = [jnp.exp(f + mp - m) for f, mp, m in zip(F, m_prev, m_tok)]
    num = [x[:, 0:DH_M] + w * y[:, 0:DH_M] for x, w, y in zip(nd, inter, qc)]
    den = [x[:, DH_M:] + w * y[:, DH_M:] for x, w, y in zip(nd, inter, qc)]
    hh = [x / jnp.maximum(jnp.abs(y), jnp.exp(-m)) for x, y, m in zip(num, den, m_tok)]
    ms = [_dot((x * x).astype(BF16), ones_sq_b) * (1.0 / DH_M) for x in hh]
    for (b, h), x, y in zip(chains, hh, ms):
        hn = x * lax.rsqrt(y + EPS) * g_ref[:, hsl(h)]
        hm_ref[b, :, hsl(h)] = (so_ref[b, :, hsl(h)].astype(F32) * hn).astype(BF16)
    m_end = [m[L - 1:L, :] for m in m_tok]
    f_end = [f[L - 1:L, :] for f in F]
    w_end = [jnp.exp(fe + x - me) for fe, x, me in zip(f_end, a, m_end)]
    decay = [jnp.exp(fe + mp - me) for fe, mp, me in zip(f_end, m_prev, m_end)]
    kw = [kk.astype(F32) * w for kk, w in zip(k, w_end)]
    upd = [_tn(vv, x.astype(BF16)) for vv, x in zip(v, kw)]
    for i in range(len(chains)):
        cst[i] = decay[i] * cst[i] + upd[i]
        n_new = decay[i] * nscr[i][0:1, :] + jnp.sum(kw[i], axis=0, keepdims=True)
        nscr[i] = jnp.broadcast_to(n_new, (SUBLANES, LANES))
        mscr[i] = jnp.broadcast_to(m_end[i], (SUBLANES, LANES))

    @pl.when(j == pl.num_programs(1) - 1)
    def _fin():
        for i, (b, h) in enumerate(chains):
            c_ref[b, h] = cst[i]
            n_ref[b, h:h + 1, :] = nscr[i][0:1, :]
            m_ref[b, h:h + 1, :] = mscr[i][0:1, :]


def _mlstm(q, k, v, so, gt, c0, n0, m0, g, *, bb, L):
    B, T, _ = q.shape
    tok = lambda width: pl.BlockSpec((bb, L, width), lambda i, j: (i, j, 0))
    st3 = pl.BlockSpec((bb, H_M, LANES), lambda i, j: (i, 0, 0))
    st4 = pl.BlockSpec((bb, H_M, DH_M, DH_M), lambda i, j: (i, 0, 0, 0))
    return pl.pallas_call(
        functools.partial(_mlstm_kernel, bb=bb, L=L),
        grid=(B // bb, T // L),
        in_specs=[tok(W_M), tok(W_M), tok(W_M), tok(W_M), tok(2 * LANES), st4, st3, st3,
                  pl.BlockSpec((1, W_M), lambda i, j: (0, 0))],
        out_specs=[tok(W_M), st4, st3, st3],
        out_shape=[jax.ShapeDtypeStruct((B, T, W_M), BF16),
                   jax.ShapeDtypeStruct((B, H_M, DH_M, DH_M), F32),
                   jax.ShapeDtypeStruct((B, H_M, LANES), F32),
                   jax.ShapeDtypeStruct((B, H_M, LANES), F32)],
        scratch_shapes=[pltpu.VMEM((bb * H_M, DH_M, DH_M), F32),
                        pltpu.VMEM((bb * H_M, SUBLANES, LANES), F32),
                        pltpu.VMEM((bb * H_M, SUBLANES, LANES), F32)],
        compiler_params=_cparams(("arbitrary", "arbitrary")),
        name="mlstm",
    )(q, k, v, so, gt, c0, n0, m0, g)


def _lambda(lam_ref, lam_init):
    p = lam_ref[...]
    l1 = jnp.sum(p[0:1, :] * p[1:2, :], axis=-1, keepdims=True)
    l2 = jnp.sum(p[2:3, :] * p[3:4, :], axis=-1, keepdims=True)
    return jnp.exp(l1) - jnp.exp(l2) + lam_init


def _attn_kernel(q_ref, k_ref, vt_ref, g_ref, lam_ref, o_ref, qm, acc, mrow, lrow, sa_ref, sb_ref, ca_ref,
                 cb_ref, *, tq, tk, hps, lam_init):
    qi = pl.program_id(2)
    heads = range(hps)
    lane = lax.broadcasted_iota(jnp.int32, (tq, 2 * DQK_D), 1)
    hsl = lambda h: slice(h * DV_D, (h + 1) * DV_D)
    for h in heads:
        q = q_ref[0, :, hsl(h)]
        zero = jnp.zeros_like(q)
        qm[h, 0:tq, :] = jnp.where(lane < DQK_D, q, zero)
        qm[h, tq:2 * tq, :] = jnp.where(lane >= DQK_D, q, zero)
    acc[...] = jnp.zeros_like(acc)
    mrow[...] = jnp.full_like(mrow, NEG_BIG)
    lrow[...] = jnp.zeros_like(lrow)

    def scores(kb, s_buf, c_buf):
        k0 = pl.multiple_of(kb * tk, tk)
        for h in heads:
            s = _nt(k_ref[0, pl.ds(k0, tk), hsl(h)], qm[h])
            s_buf[h] = s
            c_buf[h] = jnp.max(s, axis=0, keepdims=True)

    def softmax_pv(kb, s_buf, c_buf, masked):
        k0 = pl.multiple_of(kb * tk, tk)
        for h in heads:
            s = s_buf[h]
            if masked:
                kv_pos = k0 + lax.broadcasted_iota(jnp.int32, (tk, 2 * tq), 0)
                ql = lax.broadcasted_iota(jnp.int32, (tk, 2 * tq), 1)
                q_pos = qi * tq + jnp.where(ql >= tq, ql - tq, ql)
                s = jnp.where(kv_pos <= q_pos, s, NEG_BIG)
                c = jnp.max(s, axis=0, keepdims=True)
            else:
                c = c_buf[h]
            m_old = mrow[h]
            m_new = jnp.maximum(m_old, c)
            alpha = jnp.exp2(m_old - m_new)
            p = jnp.exp2(s - m_new)
            lrow[h] = alpha * lrow[h] + jnp.sum(p, axis=0, keepdims=True)
            acc[h] = acc[h] * alpha + _dot(vt_ref[0, hsl(h), pl.ds(k0, tk)], p.astype(BF16))
            mrow[h] = m_new

    n_full = (qi * tq) // tk
    scores(0, sa_ref, ca_ref)

    def pair(j, carry):
        kb = 2 * j
        scores(kb + 1, sb_ref, cb_ref)
        softmax_pv(kb, sa_ref, ca_ref, False)
        scores(kb + 2, sa_ref, ca_ref)
        softmax_pv(kb + 1, sb_ref, cb_ref, False)
        return carry

    lax.fori_loop(0, n_full // 2, pair, 0)
    odd = n_full % 2 == 1

    @pl.when(odd)
    def _tail_odd():
        scores(n_full, sb_ref, cb_ref)
        softmax_pv(n_full - 1, sa_ref, ca_ref, False)
        softmax_pv(n_full, sb_ref, cb_ref, True)

    @pl.when(jnp.logical_not(odd))
    def _tail_even():
        softmax_pv(n_full, sa_ref, ca_ref, True)

    lam = _lambda(lam_ref, lam_init)
    for h in heads:
        l = lrow[h]
        a = acc[h]
        o = a[:, 0:tq] / l[:, 0:tq] - lam * (a[:, tq:2 * tq] / l[:, tq:2 * tq])
        ms = jnp.mean(o * o, axis=0, keepdims=True)
        on = o * lax.rsqrt(ms + EPS) * (g_ref[h] * (1.0 - lam_init))
        o_ref[0, :, hsl(h)] = on.T.astype(BF16)


def _attn_prompt(q, k, vt, g3, lam_p, *, tq, tk, hps, lam_init):
    B, T, _ = q.shape
    w = hps * DV_D
    vec = lambda: pltpu.VMEM((hps, 1, 2 * tq), F32)
    return pl.pallas_call(
        functools.partial(_attn_kernel, tq=tq, tk=tk, hps=hps, lam_init=lam_init),
        grid=(B, H_D // hps, T // tq),
        in_specs=[pl.BlockSpec((1, tq, w), lambda b, h, i: (b, i, h)),
                  pl.BlockSpec((1, T, w), lambda b, h, i: (b, 0, h)),
                  pl.BlockSpec((1, w, T), lambda b, h, i: (b, h, 0)),
                  pl.BlockSpec((hps, DV_D, 1), lambda b, h, i: (h, 0, 0)),
                  pl.BlockSpec((4, DQK_D), lambda b, h, i: (0, 0))],
        out_specs=pl.BlockSpec((1, tq, w), lambda b, h, i: (b, i, h)),
        out_shape=jax.ShapeDtypeStruct((B, T, W_D), BF16),
        scratch_shapes=[pltpu.VMEM((hps, 2 * tq, DV_D), BF16), pltpu.VMEM((hps, DV_D, 2 * tq), F32),
                        vec(), vec(),
                        pltpu.VMEM((hps, tk, 2 * tq), F32), pltpu.VMEM((hps, tk, 2 * tq), F32),
                        vec(), vec()],
        compiler_params=_cparams(("arbitrary", "arbitrary", "arbitrary")),
        name="attn_prompt",
    )(q, k, vt, g3, lam_p)


def _decode_kernel(pt_ref, q_ref, kn_ref, vn_ref, g_ref, lam_ref, *rest, n_pages, page, t_new, lam_init):
    k_refs = rest[:n_pages]
    v_refs = rest[n_pages:2 * n_pages]
    o_ref = rest[2 * n_pages]
    nr = 2 * H_D * t_new
    pr = page * H_D
    nn = t_new * H_D
    del pt_ref
    q = q_ref[0]
    rq = lax.broadcasted_iota(jnp.int32, (nr, DV_D), 0)
    cq = lax.broadcasted_iota(jnp.int32, (nr, DV_D), 1)
    qb = jnp.where(_idiv(rq, H_D * t_new) == _idiv(cq, DQK_D), q, jnp.zeros_like(q))

    r = lax.broadcasted_iota(jnp.int32, (nr, pr), 0)
    c = lax.broadcasted_iota(jnp.int32, (nr, pr), 1)
    head_ok = _imod(_idiv(r, t_new), H_D) == _imod(c, H_D)
    s_old = jnp.concatenate(
        [jnp.where(head_ok, _nt(qb, k_refs[j][...].astype(BF16)), NEG_BIG) for j in range(n_pages)], axis=1)
    rn = lax.broadcasted_iota(jnp.int32, (nr, nn), 0)
    cn = lax.broadcasted_iota(jnp.int32, (nr, nn), 1)
    s_new = _nt(qb, kn_ref[0].astype(BF16))
    s_new = jnp.where(_imod(_idiv(rn, t_new), H_D) == _imod(cn, H_D), s_new, NEG_BIG)
    s_new = jnp.where(_idiv(cn, H_D) <= _imod(rn, t_new), s_new, NEG_BIG)
    m = jnp.maximum(jnp.max(s_old, axis=1, keepdims=True), jnp.max(s_new, axis=1, keepdims=True))
    p_old = jnp.exp2(s_old - m)
    p_new = jnp.exp2(s_new - m)
    l = jnp.sum(p_old, axis=1, keepdims=True) + jnp.sum(p_new, axis=1, keepdims=True)
    pb = p_old.astype(BF16)
    out = _dot(p_new.astype(BF16), vn_ref[0].astype(BF16))
    for j in range(n_pages):
        out = out + _dot(pb[:, j * pr:(j + 1) * pr], v_refs[j][...].astype(BF16))
    o_r = out / l
    half = H_D * t_new
    lam = _lambda(lam_ref, lam_init)
    o = o_r[0:half, :] - lam * o_r[half:2 * half, :]
    o_ref[0] = (_rms(o, g_ref[...]) * (1.0 - lam_init)).astype(BF16)


def _attn_decode(page_table, q_rep, k_new, v_new, cache_k2, cache_v2, g_rows, lam_p, *, page, t_new, lam_init):
    n_seq, n_pages = page_table.shape
    nr = 2 * H_D * t_new
    half = H_D * t_new

    def page_spec(j):
        return pl.BlockSpec((page * H_D, DV_D), lambda b, pt: (pt[b, j], 0))

    grid_spec = pltpu.PrefetchScalarGridSpec(
        num_scalar_prefetch=1,
        grid=(n_seq,),
        in_specs=[pl.BlockSpec((1, nr, DV_D), lambda b, pt: (b, 0, 0)),
                  pl.BlockSpec((1, half, DV_D), lambda b, pt: (b, 0, 0)),
                  pl.BlockSpec((1, half, DV_D), lambda b, pt: (b, 0, 0)),
                  pl.BlockSpec((half, DV_D), lambda b, pt: (0, 0)),
                  pl.BlockSpec((4, DQK_D), lambda b, pt: (0, 0))]
                 + [page_spec(j) for j in range(n_pages)] + [page_spec(j) for j in range(n_pages)],
        out_specs=pl.BlockSpec((1, half, DV_D), lambda b, pt: (b, 0, 0)),
    )
    return pl.pallas_call(
        functools.partial(_decode_kernel, n_pages=n_pages, page=page, t_new=t_new, lam_init=lam_init),
        grid_spec=grid_spec,
        out_shape=jax.ShapeDtypeStruct((n_seq, half, DV_D), BF16),
        compiler_params=_cparams(("arbitrary",)),
        name="attn_decode",
    )(page_table, q_rep, k_new, v_new, g_rows, lam_p, *([cache_k2] * n_pages), *([cache_v2] * n_pages))


def _merge_kernel(x_ref, hm_ref, hd_ref, sgm_ref, sgd_ref, gate1_ref, sh2_ref, sc2_ref, gpost_ref, gpre_ref,
                  wpm_ref, wpd_ref, wout_ref, x1_ref, h2_ref):
    tm = x_ref.shape[0]
    rows = [slice(i * tm // MERGE_ROW_GROUPS, (i + 1) * tm // MERGE_ROW_GROUPS) for i in range(MERGE_ROW_GROUPS)]
    mod = lambda ref, r: ref[...] if ref.shape[0] == 1 else ref[r, :]
    pm = [_dot(hm_ref[r, :], wpm_ref[...]) for r in rows]
    pd = [_dot(hd_ref[r, :], wpd_ref[...]) for r in rows]
    merged = [(sgm_ref[r, :].astype(F32) * a + sgd_ref[r, :].astype(F32) * b).astype(BF16)
              for r, a, b in zip(rows, pm, pd)]
    y = [_dot(m, wout_ref[...]) for m in merged]
    for r, yy in zip(rows, y):
        x1 = x_ref[r, :] + mod(gate1_ref, r) * _rms(yy, gpost_ref[...])
        x1_ref[r, :] = x1
        h2_ref[r, :] = (_rms(x1, gpre_ref[...]) * (1.0 + mod(sc2_ref, r)) + mod(sh2_ref, r)).astype(BF16)


def _merge(x, hm, hd, sgm, sgd, gate1, sh2, sc2, gpost, gpre, wpm, wpd, wout, *, tm, rows_per_mod):
    n = x.shape[0]
    row = lambda width: pl.BlockSpec((tm, width), lambda i: (i, 0))
    mod = _mod_spec(tm, rows_per_mod)
    return pl.pallas_call(
        _merge_kernel,
        grid=(n // tm,),
        in_specs=[row(D_MODEL), row(W_M), row(W_D), row(D_MODEL), row(D_MODEL), mod, mod, mod,
                  _const_spec((1, D_MODEL)), _const_spec((1, D_MODEL)),
                  _const_spec((W_M, D_MODEL)), _const_spec((W_D, D_MODEL)), _const_spec((D_MODEL, D_MODEL))],
        out_specs=[row(D_MODEL), row(D_MODEL)],
        out_shape=[jax.ShapeDtypeStruct((n, D_MODEL), F32), jax.ShapeDtypeStruct((n, D_MODEL), BF16)],
        compiler_params=_cparams(("arbitrary",)),
        name="merge",
    )(x, hm, hd, sgm, sgd, gate1, sh2, sc2, gpost, gpre, wpm, wpd, wout)


def _gelu_tanh(x):
    return 0.5 * x * (1.0 + jnp.tanh(math.sqrt(2.0 / math.pi) * (x + 0.044715 * (x * x * x))))


def _ffn_kernel(*refs, tm, tiles_per_seq, t_seq):
    if tiles_per_seq is None:
        (h2_ref, x1_ref, gate2_ref, gpost_ref, wup_ref, cw_ref, cb_ref, wdn_ref, st_ref,
         y_ref, tail_ref, a_scr, p1_scr, p2_scr) = refs
        a_scr[0:SUBLANES, :] = jnp.zeros((SUBLANES, D_FF), F32)
        n_st = st_ref.shape[0]
        r = lax.broadcasted_iota(jnp.int32, (tm, n_st), 0)
        j = lax.broadcasted_iota(jnp.int32, (tm, n_st), 1)
        same_seq = _idiv(r, t_seq) == _idiv(j, CONV_W - 1)
        t_r, s_j = _imod(r, t_seq), _imod(j, CONV_W - 1)
        sel1 = jnp.where(same_seq, jnp.where(t_r == 0, jnp.where(s_j == 1, 1.0, 0.0), 0.0), 0.0)
        sel2 = jnp.where(same_seq, jnp.where(t_r == s_j, 1.0, 0.0), 0.0)
        st3 = _split3(st_ref[...])
        p1_scr[...] = _select_rows(sel1.astype(BF16), st3)
        p2_scr[...] = _select_rows(sel2.astype(BF16), st3)
    else:
        (h2_ref, x1_ref, gate2_ref, gpost_ref, wup_ref, cw_ref, cb_ref, wdn_ref,
         y_ref, tail_ref, a_scr) = refs
        @pl.when(pl.program_id(0) % tiles_per_seq == 0)
        def _zero():
            a_scr[0:SUBLANES, :] = jnp.zeros((SUBLANES, D_FF), F32)

    h2 = h2_ref[...]
    ch = D_FF // FFN_CHUNKS
    f = None
    for c in range(FFN_CHUNKS):
        cs = slice(c * ch, (c + 1) * ch)
        a = _dot(h2, wup_ref[:, c * ch:(c + 1) * ch])
        b = _dot(h2, wup_ref[:, D_FF + c * ch:D_FF + (c + 1) * ch])
        a_scr[SUBLANES:SUBLANES + tm, cs] = a
        prev1 = a_scr[SUBLANES - 1:SUBLANES - 1 + tm, cs]
        prev2 = a_scr[SUBLANES - 2:SUBLANES - 2 + tm, cs]
        if tiles_per_seq is None:
            tpos = lax.broadcasted_iota(jnp.int32, (tm, ch), 0) % t_seq
            prev1 = jnp.where(tpos >= 1, prev1, p1_scr[:, cs])
            prev2 = jnp.where(tpos >= 2, prev2, p2_scr[:, cs])
        conv = cb_ref[:, cs] + cw_ref[0:1, cs] * prev2 + cw_ref[1:2, cs] * prev1 + cw_ref[2:3, cs] * a
        act = (_gelu_tanh(conv) * b).astype(BF16)
        fc = _dot(act, wdn_ref[cs, :])
        f = fc if f is None else f + fc
    if tiles_per_seq is None:
        sel_t = jnp.where(_idiv(r, t_seq) == _idiv(j, CONV_W - 1),
                          jnp.where(t_r == s_j + (t_seq - (CONV_W - 1)), 1.0, 0.0), 0.0).astype(BF16)
        a3 = _split3(a_scr[SUBLANES:SUBLANES + tm, :])
        tail_ref[...] = _tn(sel_t, a3[0]) + _tn(sel_t, a3[1]) + _tn(sel_t, a3[2])
    else:
        tail = a_scr[tm:tm + SUBLANES, :]
        tail_ref[0] = tail
        a_scr[0:SUBLANES, :] = tail
    y_ref[...] = x1_ref[...] + gate2_ref[...] * _rms(f, gpost_ref[...])


def _ffn(h2, x1, gate2, gpost, wup, cw, cb, wdn, st=None, *, tm, rows_per_mod, t_seq):
    n = h2.shape[0]
    row = lambda width: pl.BlockSpec((tm, width), lambda i: (i, 0))
    in_specs = [row(D_MODEL), row(D_MODEL), _mod_spec(tm, rows_per_mod), _const_spec((1, D_MODEL)),
                _const_spec((D_MODEL, 2 * D_FF)), _const_spec((SUBLANES, D_FF)), _const_spec((1, D_FF)),
                _const_spec((D_FF, D_MODEL))]
    args = [h2, x1, gate2, gpost, wup, cw, cb, wdn]
    scratch = [pltpu.VMEM((tm + SUBLANES, D_FF), F32)]
    if rows_per_mod is None:
        assert n == tm and t_seq >= CONV_W - 1
        tiles_per_seq = None
        st_rows = (n // t_seq) * (CONV_W - 1)
        st_spec = pl.BlockSpec((st_rows, D_FF), lambda i: (0, 0))
        in_specs += [st_spec]
        args += [st]
        out_specs = [row(D_MODEL), st_spec]
        out_shape = [jax.ShapeDtypeStruct((n, D_MODEL), F32), jax.ShapeDtypeStruct((st_rows, D_FF), F32)]
        scratch += [pltpu.VMEM((tm, D_FF), F32), pltpu.VMEM((tm, D_FF), F32)]
    else:
        tiles_per_seq = rows_per_mod // tm
        out_specs = [row(D_MODEL), pl.BlockSpec((1, SUBLANES, D_FF), lambda i: (i // tiles_per_seq, 0, 0))]
        out_shape = [jax.ShapeDtypeStruct((n, D_MODEL), F32),
                     jax.ShapeDtypeStruct((n // rows_per_mod, SUBLANES, D_FF), F32)]
    return pl.pallas_call(
        functools.partial(_ffn_kernel, tm=tm, tiles_per_seq=tiles_per_seq, t_seq=t_seq),
        grid=(n // tm,),
        in_specs=in_specs,
        out_specs=out_specs,
        out_shape=out_shape,
        scratch_shapes=scratch,
        compiler_params=_cparams(("arbitrary",)),
        name="ffn",
    )(*args)


def _win_prep_kernel(w_ref, o_ref):
    o_ref[0:GATE_ROW0, :] = w_ref[0:GATE_ROW0, :].astype(BF16)
    o_ref[GATE_ROW0:, :] = w_ref[GATE_ROW0 + 2 * H_M:, :].astype(BF16)


def _win_prep(w_t):
    d_in = w_t.shape[0]
    tc = 256
    return pl.pallas_call(
        _win_prep_kernel,
        grid=(D_MODEL // tc,),
        in_specs=[pl.BlockSpec((d_in, tc), lambda i: (0, i))],
        out_specs=pl.BlockSpec((D_IN_ROWS, tc), lambda i: (0, i)),
        out_shape=jax.ShapeDtypeStruct((D_IN_ROWS, D_MODEL), BF16),
        compiler_params=_cparams(("arbitrary",)),
        name="win_prep",
    )(w_t)


def _prep_weights(w_in, b_if, w_proj_m, w_proj_d, w_out, w_up, conv_w, conv_b, w_down):
    assert w_in.shape[1] == D_IN_ROWS + 2 * H_M
    w_t = w_in.T
    w_rows = _win_prep(w_t)
    gpad = jnp.zeros((LANES - H_M, D_MODEL), F32)
    gates = w_t[GATE_ROW0:GATE_ROW0 + 2 * H_M]
    wg = jnp.concatenate([gates[:H_M], gpad, gates[H_M:], gpad], axis=0).astype(BF16)
    bpad = jnp.zeros((LANES - H_M,), F32)
    bif = jnp.concatenate([b_if[:H_M], bpad, b_if[H_M:], bpad]).reshape(1, 2 * LANES)
    cw = jnp.concatenate([conv_w, jnp.zeros((SUBLANES - CONV_W, D_FF), F32)], axis=0)
    return dict(w_in=w_rows, wg=wg, bif=bif, wpm=w_proj_m.astype(BF16), wpd=w_proj_d.astype(BF16),
                wout=w_out.astype(BF16), wup=w_up.astype(BF16), cw=cw, cb=conv_b.reshape(1, D_FF),
                wdn=w_down.astype(BF16))


def _layer(l, pw, gains, lam_p, ada_p, ada_s, x_prompt, x_sample, cache_k, cache_v, page_table,
           state_C, state_n, state_m, state_conv, *, tm_p, l_chunk, tq, tk):
    B, T, _ = x_prompt.shape
    S, Ts, _ = x_sample.shape
    lam_init = 0.8 - 0.6 * math.exp(-0.3 * l)
    g_pre_mix, g_post_mix, g_mlstm, g_diff, g_pre_ffn, g_post_ffn = gains
    row = lambda g: g.reshape(1, -1)

    xp = x_prompt.reshape(B * T, D_MODEL)
    modp = [a.reshape(B, 1, D_MODEL) for a in jnp.split(ada_p, 6, axis=-1)]
    (mq, mk, mv, so, gt, dq, dkf, dkb, dvf, dvt, sgm, sgd) = _inproj(
        xp, modp[0], modp[1], row(g_pre_mix), pw["w_in"], pw["wg"], pw["bif"], tm=tm_p, rows_per_mod=T, v_transposed=True)
    tok = lambda a: a.reshape(B, T, -1)
    zc = jnp.zeros((B, H_M, DH_M, DH_M), F32)
    zn = jnp.zeros((B, H_M, LANES), F32)
    hm, Cp, n_p, m_p = _mlstm(tok(mq), tok(mk), tok(mv), tok(so), tok(gt), zc, zn, zn, row(g_mlstm),
                              bb=B, L=l_chunk)
    hd = _attn_prompt(tok(dq), tok(dkb), dvt, g_diff.reshape(H_D, DV_D, 1), lam_p, tq=tq, tk=tk,
                      hps=ATTN_HEADS_PER_STEP, lam_init=lam_init)
    x1, h2 = _merge(xp, hm.reshape(B * T, W_M), hd.reshape(B * T, W_D), sgm, sgd, modp[2], modp[3], modp[4],
                    row(g_post_mix), row(g_pre_ffn), pw["wpm"], pw["wpd"], pw["wout"], tm=tm_p, rows_per_mod=T)
    yp, tail = _ffn(h2, x1, modp[5], row(g_post_ffn), pw["wup"], pw["cw"], pw["cb"], pw["wdn"],
                    tm=tm_p // 2, rows_per_mod=T, t_seq=T)
    out_p = (yp.reshape(B, T, D_MODEL), dkf.reshape(B, T, H_D, 2 * DQK_D), dvf.reshape(B, T, H_D, DV_D),
             Cp, n_p, m_p[:, :, 0], tail[:, SUBLANES - (CONV_W - 1):, :])

    ns = S * Ts
    xs = x_sample.reshape(ns, D_MODEL)
    mods = [jnp.repeat(a, Ts, axis=0) for a in jnp.split(ada_s, 6, axis=-1)]
    (mq, mk, mv, so, gt, dq, dkf, dkb, dvf, dvb, sgm, sgd) = _inproj(
        xs, mods[0], mods[1], row(g_pre_mix), pw["w_in"], pw["wg"], pw["bif"], tm=ns, rows_per_mod=None, v_transposed=False)
    Lp = 16
    padt = lambda a: jnp.pad(a.reshape(S, Ts, -1), ((0, 0), (0, Lp - Ts), (0, 0)))
    gt3 = gt.reshape(S, Ts, 2 * LANES)
    gpad = jnp.concatenate([jnp.full((S, Lp - Ts, LANES), NEG_BIG, F32),
                            jnp.full((S, Lp - Ts, LANES), -NEG_BIG, F32)], axis=-1)
    gt_p = jnp.concatenate([gt3, gpad], axis=1)
    m0 = jnp.broadcast_to(state_m[l][:, :, None], (S, H_M, LANES))
    hm, Cs, n_s, m_s = _mlstm(padt(mq), padt(mk), padt(mv), padt(so), gt_p, state_C[l], state_n[l], m0,
                              row(g_mlstm), bb=8, L=Lp)
    hm = hm[:, :Ts, :].reshape(ns, W_M)
    n_pool, page = cache_k.shape[1], cache_k.shape[2]
    ck2 = cache_k[l].reshape(n_pool * page * H_D, DV_D)
    cv2 = cache_v[l].reshape(n_pool * page * H_D, DV_D)
    q_rep = dq.reshape(S, Ts, H_D, DV_D).transpose(0, 2, 1, 3)
    q_rep = jnp.broadcast_to(q_rep[:, None], (S, 2, H_D, Ts, DV_D)).reshape(S, 2 * H_D * Ts, DV_D)
    new_rows = lambda a: a.reshape(S, Ts * H_D, DV_D)
    g_rows = jnp.repeat(g_diff.reshape(H_D, DV_D), Ts, axis=0)
    hd = _attn_decode(page_table, q_rep, new_rows(dkf), new_rows(dvf), ck2, cv2, g_rows, lam_p,
                      page=page, t_new=Ts, lam_init=lam_init)
    hd = hd.reshape(S, H_D, Ts, DV_D).transpose(0, 2, 1, 3).reshape(ns, W_D)
    x1, h2 = _merge(xs, hm, hd, sgm, sgd, mods[2], mods[3], mods[4], row(g_post_mix), row(g_pre_ffn),
                    pw["wpm"], pw["wpd"], pw["wout"], tm=ns, rows_per_mod=None)
    st = state_conv[l].reshape(S * (CONV_W - 1), D_FF)
    ys, conv_s = _ffn(h2, x1, mods[5], row(g_post_ffn), pw["wup"], pw["cw"], pw["cb"], pw["wdn"], st,
                      tm=ns, rows_per_mod=None, t_seq=Ts)
    out_s = (ys.reshape(S, Ts, D_MODEL), dkf.reshape(S, Ts, H_D, 2 * DQK_D), dvf.reshape(S, Ts, H_D, DV_D),
             Cs, n_s, m_s[:, :, 0], conv_s.reshape(S, CONV_W - 1, D_FF))
    return out_p, out_s


def kernel(x_prompt, x_sample, c_prompt, c_sample, cache_k, cache_v, page_table, state_C, state_n, state_m,
           state_conv, w_ada, b_ada, g_pre_mix, g_post_mix, w_in, b_if, g_mlstm, lambda_q1, lambda_k1,
           lambda_q2, lambda_k2, g_diff, w_proj_m, w_proj_d, w_out, g_pre_ffn, g_post_ffn, w_up, conv_w,
           conv_b, w_down):
    depth = w_in.shape[0]
    B = x_prompt.shape[0]
    S = x_sample.shape[0]
    pad = (-B) % SUBLANES
    c_all = jnp.concatenate([c_prompt, jnp.zeros((pad, D_MODEL), F32), c_sample], axis=0)
    xp, xs = x_prompt, x_sample
    outs_p, outs_s = [], []
    for l in range(depth):
        ada = _ada(c_all, w_ada[l], b_ada[l].reshape(1, -1))
        pw = _prep_weights(w_in[l], b_if[l], w_proj_m[l], w_proj_d[l], w_out[l], w_up[l], conv_w[l],
                           conv_b[l], w_down[l])
        gains = (g_pre_mix[l], g_post_mix[l], g_mlstm[l], g_diff[l], g_pre_ffn[l], g_post_ffn[l])
        lam_p = jnp.stack([lambda_q1[l], lambda_k1[l], lambda_q2[l], lambda_k2[l]], axis=0)
        op, os_ = _layer(l, pw, gains, lam_p, ada[:B], ada[B + pad:], xp, xs, cache_k, cache_v, page_table,
                         state_C, state_n, state_m, state_conv, tm_p=512, l_chunk=128, tq=1024,
                         tk=min(1024, xp.shape[1]))
        xp, xs = op[0], os_[0]
        outs_p.append(op[1:])
        outs_s.append(os_[1:])
    stack = lambda outs, i: jnp.stack([o[i] for o in outs])
    return ((xp, xs) + tuple(stack(outs_p, i) for i in range(6)) + tuple(stack(outs_s, i) for i in range(6)))
```

```python
import functools
import math

import jax
import jax.numpy as jnp
from jax import lax
from jax.experimental import pallas as pl
from jax.experimental.pallas import tpu as pltpu

F32 = jnp.float32
BF16 = jnp.bfloat16

D_MODEL = 1024
H_M = 4
DH_M = 128
W_M = H_M * DH_M
H_D = 4
DQK_D = 64
DV_D = 2 * DQK_D
W_D = H_D * DV_D
D_FF = 2816
CONV_W = 3
EPS = 1e-6
LANES = 128
SUBLANES = 8
NEG_BIG = -1e30
LOG2E = 1.4426950408889634
ONES_ROWS = 16
MERGE_ROW_GROUPS = 4
FFN_CHUNKS = 1

C_MQ, C_MK, C_MV, C_MO = 0, 512, 1024, 1536
C_DQ, C_DK, C_DV = 2048, 2560, 3072
C_GM, C_GD = 3584, 4608
D_IN_ROWS = 5632
GATE_ROW0 = 2048

VMEM_LIMIT = 56 * 1024 * 1024


def _cparams(sem):
    return pltpu.CompilerParams(dimension_semantics=sem, vmem_limit_bytes=VMEM_LIMIT)


def _const_spec(shape):
    nd = len(shape)
    return pl.BlockSpec(shape, lambda *_: (0,) * nd, pipeline_mode=pl.Buffered(1))


def _rms(x, g):
    ms = jnp.mean(x * x, axis=-1, keepdims=True)
    return x * lax.rsqrt(ms + EPS) * g


def _nt(a, b):
    return lax.dot_general(a, b, (((1,), (1,)), ((), ())), preferred_element_type=F32)


def _tn(a, b):
    return lax.dot_general(a, b, (((0,), (0,)), ((), ())), preferred_element_type=F32)


def _dot(a, b):
    return jnp.dot(a, b, preferred_element_type=F32)


def _idiv(x, n):
    assert n & (n - 1) == 0
    return lax.shift_right_logical(x, jnp.int32(n.bit_length() - 1))


def _imod(x, n):
    assert n & (n - 1) == 0
    return lax.bitwise_and(x, jnp.int32(n - 1))


def _split3(x):
    hi = x.astype(BF16)
    r1 = x - hi.astype(F32)
    mid = r1.astype(BF16)
    r2 = r1 - mid.astype(F32)
    return hi, mid, r2.astype(BF16)


def _select_rows(sel_b, parts):
    return _dot(sel_b, parts[0]) + _dot(sel_b, parts[1]) + _dot(sel_b, parts[2])


def _ada_kernel(c_ref, w_ref, b_ref, o_ref):
    o_ref[...] = _dot(c_ref[...].astype(BF16), w_ref[...].astype(BF16)) + b_ref[...]


def _ada(c_all, w_ada, b_ada):
    n = c_all.shape[0]
    tn = 1024
    return pl.pallas_call(
        _ada_kernel,
        grid=(6 * D_MODEL // tn,),
        in_specs=[
            pl.BlockSpec((n, D_MODEL), lambda j: (0, 0)),
            pl.BlockSpec((D_MODEL, tn), lambda j: (0, j)),
            pl.BlockSpec((1, tn), lambda j: (0, j)),
        ],
        out_specs=pl.BlockSpec((n, tn), lambda j: (0, j)),
        out_shape=jax.ShapeDtypeStruct((n, 6 * D_MODEL), F32),
        compiler_params=_cparams(("arbitrary",)),
        name="adaln",
    )(c_all, w_ada, b_ada)


def _inproj_kernel(x_ref, sh_ref, sc_ref, g_ref, w_ref, wg_ref, bif_ref,
                   mq_ref, mk_ref, mv_ref, so_ref, gt_ref, dq_ref, dkf_ref, dkb_ref, dvf_ref, dvb_ref,
                   sgm_ref, sgd_ref, *, v_transposed):
    h = _rms(x_ref[...], g_ref[...]) * (1.0 + sc_ref[...]) + sh_ref[...]
    hb = h.astype(BF16)

    def seg(c0, c1):
        return _nt(hb, w_ref[c0:c1, :])

    mq_ref[...] = seg(C_MQ, C_MK).astype(BF16)
    mk_ref[...] = (seg(C_MK, C_MV) * (DH_M ** -0.5)).astype(BF16)
    mv_ref[...] = seg(C_MV, C_MO).astype(BF16)
    so_ref[...] = jax.nn.sigmoid(seg(C_MO, C_DQ)).astype(BF16)
    gt_ref[...] = _nt(hb, wg_ref[...]) + bif_ref[...]
    dq_ref[...] = (seg(C_DQ, C_DK) * (DQK_D ** -0.5 * LOG2E)).astype(BF16)
    dk = seg(C_DK, C_DV)
    dkb_ref[...] = dk.astype(BF16)
    dv = seg(C_DV, C_GM)
    tm = dk.shape[0]
    for hh in range(H_D):
        dkf_ref[pl.ds(hh, tm, stride=H_D), :] = dk[:, hh * DV_D:(hh + 1) * DV_D]
        dvf_ref[pl.ds(hh, tm, stride=H_D), :] = dv[:, hh * DV_D:(hh + 1) * DV_D]
    if v_transposed:
        dvt = dv.T.astype(BF16)
        for hh in range(H_D):
            dvb_ref[0, hh, 0:DV_D, :] = dvt[hh * DV_D:(hh + 1) * DV_D, :]
            dvb_ref[0, hh, DV_D:DV_D + ONES_ROWS, :] = jnp.ones((ONES_ROWS, tm), BF16)
    else:
        dvb_ref[...] = dv.astype(BF16)
    sgm_ref[...] = jax.nn.sigmoid(seg(C_GM, C_GD)).astype(BF16)
    sgd_ref[...] = jax.nn.sigmoid(seg(C_GD, D_IN_ROWS)).astype(BF16)


def _mod_spec(tm, rows_per_mod):
    if rows_per_mod is None:
        return pl.BlockSpec((tm, D_MODEL), lambda i: (i, 0))
    tiles = rows_per_mod // tm
    return pl.BlockSpec((None, 1, D_MODEL), lambda i: (i // tiles, 0, 0))


def _inproj(x, shift, scale, g, w, wg, bif, *, tm, rows_per_mod, v_transposed):
    n = x.shape[0]
    row = lambda width: pl.BlockSpec((tm, width), lambda i: (i, 0))
    sds = lambda width, dt: jax.ShapeDtypeStruct((n, width), dt)
    head_rows = pl.BlockSpec((tm * H_D, DV_D), lambda i: (i, 0))
    head_rows_sds = jax.ShapeDtypeStruct((n * H_D, DV_D), F32)
    if v_transposed:
        nb = n // rows_per_mod
        tiles = rows_per_mod // tm
        dvb_spec = pl.BlockSpec((1, H_D, DV_D + ONES_ROWS, tm), lambda i: (i // tiles, 0, 0, i % tiles))
        dvb_sds = jax.ShapeDtypeStruct((nb, H_D, DV_D + ONES_ROWS, rows_per_mod), BF16)
    else:
        dvb_spec, dvb_sds = row(W_D), sds(W_D, BF16)
    return pl.pallas_call(
        functools.partial(_inproj_kernel, v_transposed=v_transposed),
        grid=(n // tm,),
        in_specs=[row(D_MODEL), _mod_spec(tm, rows_per_mod), _mod_spec(tm, rows_per_mod),
                  _const_spec((1, D_MODEL)), _const_spec((D_IN_ROWS, D_MODEL)),
                  _const_spec((2 * LANES, D_MODEL)), _const_spec((1, 2 * LANES))],
        out_specs=[row(W_M), row(W_M), row(W_M), row(W_M), row(2 * LANES), row(W_D), head_rows, row(W_D),
                   head_rows, dvb_spec, row(D_MODEL), row(D_MODEL)],
        out_shape=[sds(W_M, BF16), sds(W_M, BF16), sds(W_M, BF16), sds(W_M, BF16), sds(2 * LANES, F32),
                   sds(W_D, BF16), head_rows_sds, sds(W_D, BF16), head_rows_sds, dvb_sds,
                   sds(D_MODEL, BF16), sds(D_MODEL, BF16)],
        compiler_params=_cparams(("arbitrary",)),
        name="inproj",
    )(x, shift, scale, g, w, wg, bif)


def _mlstm_kernel(q_ref, k_ref, v_ref, so_ref, gt_ref, c0_ref, n0_ref, m0_ref, g_ref,
                  hm_ref, c_ref, n_ref, m_ref, cst, nscr, mscr, *, bb, L):
    j = pl.program_id(1)

    @pl.when(j == 0)
    def _init():
        for b in range(bb):
            for h in range(H_M):
                idx = b * H_M + h
                cst[idx] = c0_ref[b, h]
                nscr[idx] = jnp.broadcast_to(n0_ref[b, h:h + 1, :], (SUBLANES, LANES))
                mscr[idx] = jnp.broadcast_to(m0_ref[b, h:h + 1, :], (SUBLANES, LANES))

    rowi = lax.broadcasted_iota(jnp.int32, (L, L), 0)
    coli = lax.broadcasted_iota(jnp.int32, (L, L), 1)
    tril = rowi >= coli
    tril_b = jnp.where(tril, 1.0, 0.0).astype(BF16)
    ones3_b = jnp.ones((L, 3 * LANES), BF16)
    ones_b = jnp.ones((L, LANES), BF16)
    ones_sq_b = jnp.ones((DH_M, DH_M), BF16)
    lane = lax.broadcasted_iota(jnp.int32, (L, LANES), 1)

    chains = [(b, h) for b in range(bb) for h in range(H_M)]
    hsl = lambda h: slice(h * DH_M, (h + 1) * DH_M)
    rep = lambda x, h: jnp.broadcast_to(x[:, h:h + 1], (L, LANES))

    gts = [gt_ref[b] for b in range(bb)]
    logfs = [-(jnp.maximum(-g[:, LANES:], 0.0) + jnp.log(1.0 + jnp.exp(-jnp.abs(g[:, LANES:])))) for g in gts]
    i_rep = [rep(gts[b][:, 0:LANES], h) for b, h in chains]
    lf3 = [_split3(x) for x in logfs]
    f_all = [_dot(tril_b, x[0]) + _dot(tril_b, x[1]) + _dot(tril_b, x[2]) for x in lf3]
    F = [rep(f_all[b], h) for b, h in chains]
    a = [i - f for i, f in zip(i_rep, F)]
    a3 = [_split3(jnp.where(lane == 0, x, 0.0)) for x in a]
    A = [_nt(ones3_b, jnp.concatenate(x, axis=1)) for x in a3]
    cm = [jnp.broadcast_to(jnp.max(jnp.where(tril, x, -jnp.inf), axis=1, keepdims=True), (L, LANES))
          for x in A]
    m_prev = [mscr[b * H_M + h][0:1, :] for b, h in chains]
    m_tok = [f + jnp.maximum(mp, c) for f, mp, c in zip(F, m_prev, cm)]
    dmat = [jnp.exp(jnp.where(tril, (f - m)[:, 0:L] + x, NEG_BIG)) for f, m, x in zip(F, m_tok, A)]
    q = [q_ref[b, :, hsl(h)] for b, h in chains]
    k = [k_ref[b, :, hsl(h)] for b, h in chains]
    v = [v_ref[b, :, hsl(h)] for b, h in chains]
    s = [(_nt(qq, kk) * d).astype(BF16) for qq, kk, d in zip(q, k, dmat)]
    nd = [_dot(ss, jnp.concatenate([vv, ones_b], axis=1)) for ss, vv in zip(s, v)]
    cn = [jnp.concatenate([cst[i].astype(BF16),
                           jnp.broadcast_to(nscr[i][0:1, :], (DH_M, DH_M)).astype(BF16)], axis=0)
          for i in range(len(chains))]
    qc = [_nt(qq, x) for qq, x in zip(q, cn)]
    inter = [jnp.exp(f + mp - m) for f, mp, m in zip(F, m_prev, m_tok)]
    num = [x[:, 0:DH_M] + w * y[:, 0:DH_M] for x, w, y in zip(nd, inter, qc)]
    den = [x[:, DH_M:] + w * y[:, DH_M:] for x, w, y in zip(nd, inter, qc)]
    hh = [x / jnp.maximum(jnp.abs(y), jnp.exp(-m)) for x, y, m in zip(num, den, m_tok)]
    ms = [_dot((x * x).astype(BF16), ones_sq_b) * (1.0 / DH_M) for x in hh]
    for (b, h), x, y in zip(chains, hh, ms):
        hn = x * lax.rsqrt(y + EPS) * g_ref[:, hsl(h)]
        hm_ref[b, :, hsl(h)] = (so_ref[b, :, hsl(h)].astype(F32) * hn).astype(BF16)
    m_end = [m[L - 1:L, :] for m in m_tok]
    f_end = [f[L - 1:L, :] for f in F]
    w_end = [jnp.exp(fe + x - me) for fe, x, me in zip(f_end, a, m_end)]
    decay = [jnp.exp(fe + mp - me) for fe, mp, me in zip(f_end, m_prev, m_end)]
    kw = [kk.astype(F32) * w for kk, w in zip(k, w_end)]
    upd = [_tn(vv, x.astype(BF16)) for vv, x in zip(v, kw)]
    for i in range(len(chains)):
        cst[i] = decay[i] * cst[i] + upd[i]
        n_new = decay[i] * nscr[i][0:1, :] + jnp.sum(kw[i], axis=0, keepdims=True)
        nscr[i] = jnp.broadcast_to(n_new, (SUBLANES, LANES))
        mscr[i] = jnp.broadcast_to(m_end[i], (SUBLANES, LANES))

    @pl.when(j == pl.num_programs(1) - 1)
    def _fin():
        for i, (b, h) in enumerate(chains):
            c_ref[b, h] = cst[i]
            n_ref[b, h:h + 1, :] = nscr[i][0:1, :]
            m_ref[b, h:h + 1, :] = mscr[i][0:1, :]


def _mlstm(q, k, v, so, gt, c0, n0, m0, g, *, bb, L):
    B, T, _ = q.shape
    tok = lambda width: pl.BlockSpec((bb, L, width), lambda i, j: (i, j, 0))
    st3 = pl.BlockSpec((bb, H_M, LANES), lambda i, j: (i, 0, 0))
    st4 = pl.BlockSpec((bb, H_M, DH_M, DH_M), lambda i, j: (i, 0, 0, 0))
    return pl.pallas_call(
        functools.partial(_mlstm_kernel, bb=bb, L=L),
        grid=(B // bb, T // L),
        in_specs=[tok(W_M), tok(W_M), tok(W_M), tok(W_M), tok(2 * LANES), st4, st3, st3,
                  pl.BlockSpec((1, W_M), lambda i, j: (0, 0))],
        out_specs=[tok(W_M), st4, st3, st3],
        out_shape=[jax.ShapeDtypeStruct((B, T, W_M), BF16),
                   jax.ShapeDtypeStruct((B, H_M, DH_M, DH_M), F32),
                   jax.ShapeDtypeStruct((B, H_M, LANES), F32),
                   jax.ShapeDtypeStruct((B, H_M, LANES), F32)],
        scratch_shapes=[pltpu.VMEM((bb * H_M, DH_M, DH_M), F32),
                        pltpu.VMEM((bb * H_M, SUBLANES, LANES), F32),
                        pltpu.VMEM((bb * H_M, SUBLANES, LANES), F32)],
        compiler_params=_cparams(("arbitrary", "arbitrary")),
        name="mlstm",
    )(q, k, v, so, gt, c0, n0, m0, g)


def _lambda(lam_ref, lam_init):
    p = lam_ref[...]
    l1 = jnp.sum(p[0:1, :] * p[1:2, :], axis=-1, keepdims=True)
    l2 = jnp.sum(p[2:3, :] * p[3:4, :], axis=-1, keepdims=True)
    return jnp.exp(l1) - jnp.exp(l2) + lam_init


def _attn_kernel(q_ref, k_ref, vt_ref, g_ref, lam_ref, o_ref, qm, acc, mrow, sa_ref, sb_ref, ca_ref,
                 cb_ref, bias_ref, *, tq, tk, lam_init):
    g = pl.program_id(2)
    lane = lax.broadcasted_iota(jnp.int32, (tq, 2 * DQK_D), 1)
    for w in range(2):
        q = q_ref[0, w * tq:(w + 1) * tq, :]
        zero = jnp.zeros_like(q)
        qm[w, 0:tq, :] = jnp.where(lane < DQK_D, q, zero)
        qm[w, tq:2 * tq, :] = jnp.where(lane >= DQK_D, q, zero)
    acc[...] = jnp.zeros_like(acc)
    mrow[...] = jnp.full_like(mrow, NEG_BIG)
    buf_a, buf_b = (sa_ref, ca_ref), (sb_ref, cb_ref)

    @pl.when(g == 0)
    def _causal_bias():
        r = lax.broadcasted_iota(jnp.int32, (tk, 2 * tq), 0)
        c = lax.broadcasted_iota(jnp.int32, (tk, 2 * tq), 1)
        bias_ref[...] = jnp.where(r <= jnp.where(c >= tq, c - tq, c), 0.0, NEG_BIG)

    def scores(kb, w, buf):
        s_buf, c_buf = buf
        k0 = pl.multiple_of(kb * tk, tk)
        s = _nt(k_ref[0, pl.ds(k0, tk), :], qm[w])
        s_buf[...] = s
        c_buf[...] = jnp.max(s, axis=0, keepdims=True)

    def softmax_pv(kb, w, buf, masked):
        s_buf, c_buf = buf
        k0 = pl.multiple_of(kb * tk, tk)
        s = s_buf[...]
        if masked:
            s = s + bias_ref[...]
            c = jnp.max(s, axis=0, keepdims=True)
        else:
            c = c_buf[...]
        m_old = mrow[w]
        m_new = jnp.maximum(m_old, c)
        alpha = jnp.exp2(m_old - m_new)
        p = jnp.exp2(s - m_new)
        acc[w] = acc[w] * alpha + _dot(vt_ref[0, 0, :, pl.ds(k0, tk)], p.astype(BF16))
        mrow[w] = m_new

    def finalize(w):
        l = acc[w, DV_D:DV_D + 1, :]
        a = acc[w, 0:DV_D, :]
        lam = _lambda(lam_ref, lam_init)
        o = a[:, 0:tq] / l[:, 0:tq] - lam * (a[:, tq:2 * tq] / l[:, tq:2 * tq])
        ms = jnp.mean(o * o, axis=0, keepdims=True)
        on = o * lax.rsqrt(ms + EPS) * (g_ref[0] * (1.0 - lam_init))
        o_ref[0, w * tq:(w + 1) * tq, :] = on.T.astype(BF16)

    def pairs(w, first, second):
        def body(j, carry):
            kb = 2 * j
            scores(kb + 1, w, second)
            softmax_pv(kb, w, first, False)
            scores(kb + 2, w, first)
            softmax_pv(kb + 1, w, second, False)
            return carry

        lax.fori_loop(0, g, body, 0)

    scores(0, 0, buf_a)
    pairs(0, buf_a, buf_b)
    scores(0, 1, buf_b)
    softmax_pv(2 * g, 0, buf_a, True)
    finalize(0)
    pairs(1, buf_b, buf_a)
    scores(2 * g + 1, 1, buf_a)
    softmax_pv(2 * g, 1, buf_b, False)
    softmax_pv(2 * g + 1, 1, buf_a, True)
    finalize(1)


def _attn_prompt(q, k, vt, g3, lam_p, *, tq, lam_init):
    B, T, _ = q.shape
    tk = tq
    vec = lambda: pltpu.VMEM((2, 1, 2 * tq), F32)
    buf = lambda: pltpu.VMEM((tk, 2 * tq), F32)
    cmax = lambda: pltpu.VMEM((1, 2 * tq), F32)
    return pl.pallas_call(
        functools.partial(_attn_kernel, tq=tq, tk=tk, lam_init=lam_init),
        grid=(B, H_D, T // (2 * tq)),
        in_specs=[pl.BlockSpec((1, 2 * tq, DV_D), lambda b, h, i: (b, i, h)),
                  pl.BlockSpec((1, T, DV_D), lambda b, h, i: (b, 0, h)),
                  pl.BlockSpec((1, 1, DV_D + ONES_ROWS, T), lambda b, h, i: (b, h, 0, 0)),
                  pl.BlockSpec((1, DV_D, 1), lambda b, h, i: (h, 0, 0)),
                  pl.BlockSpec((4, DQK_D), lambda b, h, i: (0, 0))],
        out_specs=pl.BlockSpec((1, 2 * tq, DV_D), lambda b, h, i: (b, i, h)),
        out_shape=jax.ShapeDtypeStruct((B, T, W_D), BF16),
        scratch_shapes=[pltpu.VMEM((2, 2 * tq, DV_D), BF16), pltpu.VMEM((2, DV_D + ONES_ROWS, 2 * tq), F32),
                        vec(), buf(), buf(), cmax(), cmax(), buf()],
        compiler_params=_cparams(("arbitrary", "arbitrary", "arbitrary")),
        name="attn_prompt",
    )(q, k, vt, g3, lam_p)


def _decode_kernel(pt_ref, q_ref, kn_ref, vn_ref, g_ref, lam_ref, *rest, n_pages, page, t_new, lam_init):
    k_refs = rest[:n_pages]
    v_refs = rest[n_pages:2 * n_pages]
    o_ref = rest[2 * n_pages]
    nr = 2 * H_D * t_new
    pr = page * H_D
    nn = t_new * H_D
    del pt_ref
    q = q_ref[0]
    rq = lax.broadcasted_iota(jnp.int32, (nr, DV_D), 0)
    cq = lax.broadcasted_iota(jnp.int32, (nr, DV_D), 1)
    qb = jnp.where(_idiv(rq, H_D * t_new) == _idiv(cq, DQK_D), q, jnp.zeros_like(q))

    r = lax.broadcasted_iota(jnp.int32, (nr, pr), 0)
    c = lax.broadcasted_iota(jnp.int32, (nr, pr), 1)
    head_ok = _imod(_idiv(r, t_new), H_D) == _imod(c, H_D)
    s_old = jnp.concatenate(
        [jnp.where(head_ok, _nt(qb, k_refs[j][...].astype(BF16)), NEG_BIG) for j in range(n_pages)], axis=1)
    rn = lax.broadcasted_iota(jnp.int32, (nr, nn), 0)
    cn = lax.broadcasted_iota(jnp.int32, (nr, nn), 1)
    s_new = _nt(qb, kn_ref[0].astype(BF16))
    s_new = jnp.where(_imod(_idiv(rn, t_new), H_D) == _imod(cn, H_D), s_new, NEG_BIG)
    s_new = jnp.where(_idiv(cn, H_D) <= _imod(rn, t_new), s_new, NEG_BIG)
    m = jnp.maximum(jnp.max(s_old, axis=1, keepdims=True), jnp.max(s_new, axis=1, keepdims=True))
    p_old = jnp.exp2(s_old - m)
    p_new = jnp.exp2(s_new - m)
    l = jnp.sum(p_old, axis=1, keepdims=True) + jnp.sum(p_new, axis=1, keepdims=True)
    pb = p_old.astype(BF16)
    out = _dot(p_new.astype(BF16), vn_ref[0].astype(BF16))
    for j in range(n_pages):
        out = out + _dot(pb[:, j * pr:(j + 1) * pr], v_refs[j][...].astype(BF16))
    o_r = out / l
    half = H_D * t_new
    lam = _lambda(lam_ref, lam_init)
    o = o_r[0:half, :] - lam * o_r[half:2 * half, :]
    o_ref[0] = (_rms(o, g_ref[...]) * (1.0 - lam_init)).astype(BF16)


def _attn_decode(page_table, q_rep, k_new, v_new, cache_k2, cache_v2, g_rows, lam_p, *, page, t_new, lam_init):
    n_seq, n_pages = page_table.shape
    nr = 2 * H_D * t_new
    half = H_D * t_new

    def page_spec(j):
        return pl.BlockSpec((page * H_D, DV_D), lambda b, pt: (pt[b, j], 0))

    grid_spec = pltpu.PrefetchScalarGridSpec(
        num_scalar_prefetch=1,
        grid=(n_seq,),
        in_specs=[pl.BlockSpec((1, nr, DV_D), lambda b, pt: (b, 0, 0)),
                  pl.BlockSpec((1, half, DV_D), lambda b, pt: (b, 0, 0)),
                  pl.BlockSpec((1, half, DV_D), lambda b, pt: (b, 0, 0)),
                  pl.BlockSpec((half, DV_D), lambda b, pt: (0, 0)),
                  pl.BlockSpec((4, DQK_D), lambda b, pt: (0, 0))]
                 + [page_spec(j) for j in range(n_pages)] + [page_spec(j) for j in range(n_pages)],
        out_specs=pl.BlockSpec((1, half, DV_D), lambda b, pt: (b, 0, 0)),
    )
    return pl.pallas_call(
        functools.partial(_decode_kernel, n_pages=n_pages, page=page, t_new=t_new, lam_init=lam_init),
        grid_spec=grid_spec,
        out_shape=jax.ShapeDtypeStruct((n_seq, half, DV_D), BF16),
        compiler_params=_cparams(("arbitrary",)),
        name="attn_decode",
    )(page_table, q_rep, k_new, v_new, g_rows, lam_p, *([cache_k2] * n_pages), *([cache_v2] * n_pages))


def _merge_kernel(x_ref, hm_ref, hd_ref, sgm_ref, sgd_ref, gate1_ref, sh2_ref, sc2_ref, gpost_ref, gpre_ref,
                  wpm_ref, wpd_ref, wout_ref, x1_ref, h2_ref):
    tm = x_ref.shape[0]
    rows = [slice(i * tm // MERGE_ROW_GROUPS, (i + 1) * tm // MERGE_ROW_GROUPS) for i in range(MERGE_ROW_GROUPS)]
    mod = lambda ref, r: ref[...] if ref.shape[0] == 1 else ref[r, :]
    pm = [_dot(hm_ref[r, :], wpm_ref[...]) for r in rows]
    pd = [_dot(hd_ref[r, :], wpd_ref[...]) for r in rows]
    merged = [(sgm_ref[r, :].astype(F32) * a + sgd_ref[r, :].astype(F32) * b).astype(BF16)
              for r, a, b in zip(rows, pm, pd)]
    y = [_dot(m, wout_ref[...]) for m in merged]
    for r, yy in zip(rows, y):
        x1 = x_ref[r, :] + mod(gate1_ref, r) * _rms(yy, gpost_ref[...])
        x1_ref[r, :] = x1
        h2_ref[r, :] = (_rms(x1, gpre_ref[...]) * (1.0 + mod(sc2_ref, r)) + mod(sh2_ref, r)).astype(BF16)


def _merge(x, hm, hd, sgm, sgd, gate1, sh2, sc2, gpost, gpre, wpm, wpd, wout, *, tm, rows_per_mod):
    n = x.shape[0]
    row = lambda width: pl.BlockSpec((tm, width), lambda i: (i, 0))
    mod = _mod_spec(tm, rows_per_mod)
    return pl.pallas_call(
        _merge_kernel,
        grid=(n // tm,),
        in_specs=[row(D_MODEL), row(W_M), row(W_D), row(D_MODEL), row(D_MODEL), mod, mod, mod,
                  _const_spec((1, D_MODEL)), _const_spec((1, D_MODEL)),
                  _const_spec((W_M, D_MODEL)), _const_spec((W_D, D_MODEL)), _const_spec((D_MODEL, D_MODEL))],
        out_specs=[row(D_MODEL), row(D_MODEL)],
        out_shape=[jax.ShapeDtypeStruct((n, D_MODEL), F32), jax.ShapeDtypeStruct((n, D_MODEL), BF16)],
        compiler_params=_cparams(("arbitrary",)),
        name="merge",
    )(x, hm, hd, sgm, sgd, gate1, sh2, sc2, gpost, gpre, wpm, wpd, wout)


def _gelu_tanh(x):
    return 0.5 * x * (1.0 + jnp.tanh(math.sqrt(2.0 / math.pi) * (x + 0.044715 * (x * x * x))))


def _ffn_kernel(*refs, tm, tiles_per_seq, t_seq):
    if tiles_per_seq is None:
        (h2_ref, x1_ref, gate2_ref, gpost_ref, wup_ref, cw_ref, cb_ref, wdn_ref, st_ref,
         y_ref, tail_ref, a_scr, p1_scr, p2_scr) = refs
        a_scr[0:SUBLANES, :] = jnp.zeros((SUBLANES, D_FF), F32)
        n_st = st_ref.shape[0]
        r = lax.broadcasted_iota(jnp.int32, (tm, n_st), 0)
        j = lax.broadcasted_iota(jnp.int32, (tm, n_st), 1)
        same_seq = _idiv(r, t_seq) == _idiv(j, CONV_W - 1)
        t_r, s_j = _imod(r, t_seq), _imod(j, CONV_W - 1)
        sel1 = jnp.where(same_seq, jnp.where(t_r == 0, jnp.where(s_j == 1, 1.0, 0.0), 0.0), 0.0)
        sel2 = jnp.where(same_seq, jnp.where(t_r == s_j, 1.0, 0.0), 0.0)
        st3 = _split3(st_ref[...])
        p1_scr[...] = _select_rows(sel1.astype(BF16), st3)
        p2_scr[...] = _select_rows(sel2.astype(BF16), st3)
    else:
        (h2_ref, x1_ref, gate2_ref, gpost_ref, wup_ref, cw_ref, cb_ref, wdn_ref,
         y_ref, tail_ref, a_scr) = refs
        @pl.when(pl.program_id(0) % tiles_per_seq == 0)
        def _zero():
            a_scr[0:SUBLANES, :] = jnp.zeros((SUBLANES, D_FF), F32)

    h2 = h2_ref[...]
    ch = D_FF // FFN_CHUNKS
    f = None
    for c in range(FFN_CHUNKS):
        cs = slice(c * ch, (c + 1) * ch)
        a = _dot(h2, wup_ref[:, c * ch:(c + 1) * ch])
        b = _dot(h2, wup_ref[:, D_FF + c * ch:D_FF + (c + 1) * ch])
        a_scr[SUBLANES:SUBLANES + tm, cs] = a
        prev1 = a_scr[SUBLANES - 1:SUBLANES - 1 + tm, cs]
        prev2 = a_scr[SUBLANES - 2:SUBLANES - 2 + tm, cs]
        if tiles_per_seq is None:
            tpos = lax.broadcasted_iota(jnp.int32, (tm, ch), 0) % t_seq
            prev1 = jnp.where(tpos >= 1, prev1, p1_scr[:, cs])
            prev2 = jnp.where(tpos >= 2, prev2, p2_scr[:, cs])
        conv = cb_ref[:, cs] + cw_ref[0:1, cs] * prev2 + cw_ref[1:2, cs] * prev1 + cw_ref[2:3, cs] * a
        act = (_gelu_tanh(conv) * b).astype(BF16)
        fc = _dot(act, wdn_ref[cs, :])
        f = fc if f is None else f + fc
    if tiles_per_seq is None:
        sel_t = jnp.where(_idiv(r, t_seq) == _idiv(j, CONV_W - 1),
                          jnp.where(t_r == s_j + (t_seq - (CONV_W - 1)), 1.0, 0.0), 0.0).astype(BF16)
        a3 = _split3(a_scr[SUBLANES:SUBLANES + tm, :])
        tail_ref[...] = _tn(sel_t, a3[0]) + _tn(sel_t, a3[1]) + _tn(sel_t, a3[2])
    else:
        tail = a_scr[tm:tm + SUBLANES, :]
        tail_ref[0] = tail
        a_scr[0:SUBLANES, :] = tail
    y_ref[...] = x1_ref[...] + gate2_ref[...] * _rms(f, gpost_ref[...])


def _ffn(h2, x1, gate2, gpost, wup, cw, cb, wdn, st=None, *, tm, rows_per_mod, t_seq):
    n = h2.shape[0]
    row = lambda width: pl.BlockSpec((tm, width), lambda i: (i, 0))
    in_specs = [row(D_MODEL), row(D_MODEL), _mod_spec(tm, rows_per_mod), _const_spec((1, D_MODEL)),
                _const_spec((D_MODEL, 2 * D_FF)), _const_spec((SUBLANES, D_FF)), _const_spec((1, D_FF)),
                _const_spec((D_FF, D_MODEL))]
    args = [h2, x1, gate2, gpost, wup, cw, cb, wdn]
    scratch = [pltpu.VMEM((tm + SUBLANES, D_FF), F32)]
    if rows_per_mod is None:
        assert n == tm and t_seq >= CONV_W - 1
        tiles_per_seq = None
        st_rows = (n // t_seq) * (CONV_W - 1)
        st_spec = pl.BlockSpec((st_rows, D_FF), lambda i: (0, 0))
        in_specs += [st_spec]
        args += [st]
        out_specs = [row(D_MODEL), st_spec]
        out_shape = [jax.ShapeDtypeStruct((n, D_MODEL), F32), jax.ShapeDtypeStruct((st_rows, D_FF), F32)]
        scratch += [pltpu.VMEM((tm, D_FF), F32), pltpu.VMEM((tm, D_FF), F32)]
    else:
        tiles_per_seq = rows_per_mod // tm
        out_specs = [row(D_MODEL), pl.BlockSpec((1, SUBLANES, D_FF), lambda i: (i // tiles_per_seq, 0, 0))]
        out_shape = [jax.ShapeDtypeStruct((n, D_MODEL), F32),
                     jax.ShapeDtypeStruct((n // rows_per_mod, SUBLANES, D_FF), F32)]
    return pl.pallas_call(
        functools.partial(_ffn_kernel, tm=tm, tiles_per_seq=tiles_per_seq, t_seq=t_seq),
        grid=(n // tm,),
        in_specs=in_specs,
        out_specs=out_specs,
        out_shape=out_shape,
        scratch_shapes=scratch,
        compiler_params=_cparams(("arbitrary",)),
        name="ffn",
    )(*args)


def _win_prep_kernel(w_ref, o_ref):
    o_ref[0:GATE_ROW0, :] = w_ref[0:GATE_ROW0, :].astype(BF16)
    o_ref[GATE_ROW0:, :] = w_ref[GATE_ROW0 + 2 * H_M:, :].astype(BF16)


def _win_prep(w_t):
    d_in = w_t.shape[0]
    tc = 256
    return pl.pallas_call(
        _win_prep_kernel,
        grid=(D_MODEL // tc,),
        in_specs=[pl.BlockSpec((d_in, tc), lambda i: (0, i))],
        out_specs=pl.BlockSpec((D_IN_ROWS, tc), lambda i: (0, i)),
        out_shape=jax.ShapeDtypeStruct((D_IN_ROWS, D_MODEL), BF16),
        compiler_params=_cparams(("arbitrary",)),
        name="win_prep",
    )(w_t)


def _prep_weights(w_in, b_if, w_proj_m, w_proj_d, w_out, w_up, conv_w, conv_b, w_down):
    assert w_in.shape[1] == D_IN_ROWS + 2 * H_M
    w_t = w_in.T
    w_rows = _win_prep(w_t)
    gpad = jnp.zeros((LANES - H_M, D_MODEL), F32)
    gates = w_t[GATE_ROW0:GATE_ROW0 + 2 * H_M]
    wg = jnp.concatenate([gates[:H_M], gpad, gates[H_M:], gpad], axis=0).astype(BF16)
    bpad = jnp.zeros((LANES - H_M,), F32)
    bif = jnp.concatenate([b_if[:H_M], bpad, b_if[H_M:], bpad]).reshape(1, 2 * LANES)
    cw = jnp.concatenate([conv_w, jnp.zeros((SUBLANES - CONV_W, D_FF), F32)], axis=0)
    return dict(w_in=w_rows, wg=wg, bif=bif, wpm=w_proj_m.astype(BF16), wpd=w_proj_d.astype(BF16),
                wout=w_out.astype(BF16), wup=w_up.astype(BF16), cw=cw, cb=conv_b.reshape(1, D_FF),
                wdn=w_down.astype(BF16))


def _layer(l, pw, gains, lam_p, ada_p, ada_s, x_prompt, x_sample, cache_k, cache_v, page_table,
           state_C, state_n, state_m, state_conv, *, tm_p, l_chunk, tq):
    B, T, _ = x_prompt.shape
    S, Ts, _ = x_sample.shape
    lam_init = 0.8 - 0.6 * math.exp(-0.3 * l)
    g_pre_mix, g_post_mix, g_mlstm, g_diff, g_pre_ffn, g_post_ffn = gains
    row = lambda g: g.reshape(1, -1)

    xp = x_prompt.reshape(B * T, D_MODEL)
    modp = [a.reshape(B, 1, D_MODEL) for a in jnp.split(ada_p, 6, axis=-1)]
    (mq, mk, mv, so, gt, dq, dkf, dkb, dvf, dvt, sgm, sgd) = _inproj(
        xp, modp[0], modp[1], row(g_pre_mix), pw["w_in"], pw["wg"], pw["bif"], tm=tm_p, rows_per_mod=T, v_transposed=True)
    tok = lambda a: a.reshape(B, T, -1)
    zc = jnp.zeros((B, H_M, DH_M, DH_M), F32)
    zn = jnp.zeros((B, H_M, LANES), F32)
    hm, Cp, n_p, m_p = _mlstm(tok(mq), tok(mk), tok(mv), tok(so), tok(gt), zc, zn, zn, row(g_mlstm),
                              bb=B, L=l_chunk)
    hd = _attn_prompt(tok(dq), tok(dkb), dvt, g_diff.reshape(H_D, DV_D, 1), lam_p, tq=tq, lam_init=lam_init)
    x1, h2 = _merge(xp, hm.reshape(B * T, W_M), hd.reshape(B * T, W_D), sgm, sgd, modp[2], modp[3], modp[4],
                    row(g_post_mix), row(g_pre_ffn), pw["wpm"], pw["wpd"], pw["wout"], tm=tm_p, rows_per_mod=T)
    yp, tail = _ffn(h2, x1, modp[5], row(g_post_ffn), pw["wup"], pw["cw"], pw["cb"], pw["wdn"],
                    tm=tm_p // 2, rows_per_mod=T, t_seq=T)
    out_p = (yp.reshape(B, T, D_MODEL), dkf.reshape(B, T, H_D, 2 * DQK_D), dvf.reshape(B, T, H_D, DV_D),
             Cp, n_p, m_p[:, :, 0], tail[:, SUBLANES - (CONV_W - 1):, :])

    ns = S * Ts
    xs = x_sample.reshape(ns, D_MODEL)
    mods = [jnp.repeat(a, Ts, axis=0) for a in jnp.split(ada_s, 6, axis=-1)]
    (mq, mk, mv, so, gt, dq, dkf, dkb, dvf, dvb, sgm, sgd) = _inproj(
        xs, mods[0], mods[1], row(g_pre_mix), pw["w_in"], pw["wg"], pw["bif"], tm=ns, rows_per_mod=None, v_transposed=False)
    Lp = 16
    padt = lambda a: jnp.pad(a.reshape(S, Ts, -1), ((0, 0), (0, Lp - Ts), (0, 0)))
    gt3 = gt.reshape(S, Ts, 2 * LANES)
    gpad = jnp.concatenate([jnp.full((S, Lp - Ts, LANES), NEG_BIG, F32),
                            jnp.full((S, Lp - Ts, LANES), -NEG_BIG, F32)], axis=-1)
    gt_p = jnp.concatenate([gt3, gpad], axis=1)
    m0 = jnp.broadcast_to(state_m[l][:, :, None], (S, H_M, LANES))
    hm, Cs, n_s, m_s = _mlstm(padt(mq), padt(mk), padt(mv), padt(so), gt_p, state_C[l], state_n[l], m0,
                              row(g_mlstm), bb=8, L=Lp)
    hm = hm[:, :Ts, :].reshape(ns, W_M)
    n_pool, page = cache_k.shape[1], cache_k.shape[2]
    ck2 = cache_k[l].reshape(n_pool * page * H_D, DV_D)
    cv2 = cache_v[l].reshape(n_pool * page * H_D, DV_D)
    q_rep = dq.reshape(S, Ts, H_D, DV_D).transpose(0, 2, 1, 3)
    q_rep = jnp.broadcast_to(q_rep[:, None], (S, 2, H_D, Ts, DV_D)).reshape(S, 2 * H_D * Ts, DV_D)
    new_rows = lambda a: a.reshape(S, Ts * H_D, DV_D)
    g_rows = jnp.repeat(g_diff.reshape(H_D, DV_D), Ts, axis=0)
    hd = _attn_decode(page_table, q_rep, new_rows(dkf), new_rows(dvf), ck2, cv2, g_rows, lam_p,
                      page=page, t_new=Ts, lam_init=lam_init)
    hd = hd.reshape(S, H_D, Ts, DV_D).transpose(0, 2, 1, 3).reshape(ns, W_D)
    x1, h2 = _merge(xs, hm, hd, sgm, sgd, mods[2], mods[3], mods[4], row(g_post_mix), row(g_pre_ffn),
                    pw["wpm"], pw["wpd"], pw["wout"], tm=ns, rows_per_mod=None)
    st = state_conv[l].reshape(S * (CONV_W - 1), D_FF)
    ys, conv_s = _ffn(h2, x1, mods[5], row(g_post_ffn), pw["wup"], pw["cw"], pw["cb"], pw["wdn"], st,
                      tm=ns, rows_per_mod=None, t_seq=Ts)
    out_s = (ys.reshape(S, Ts, D_MODEL), dkf.reshape(S, Ts, H_D, 2 * DQK_D), dvf.reshape(S, Ts, H_D, DV_D),
             Cs, n_s, m_s[:, :, 0], conv_s.reshape(S, CONV_W - 1, D_FF))
    return out_p, out_s


def kernel(x_prompt, x_sample, c_prompt, c_sample, cache_k, cache_v, page_table, state_C, state_n, state_m,
           state_conv, w_ada, b_ada, g_pre_mix, g_post_mix, w_in, b_if, g_mlstm, lambda_q1, lambda_k1,
           lambda_q2, lambda_k2, g_diff, w_proj_m, w_proj_d, w_out, g_pre_ffn, g_post_ffn, w_up, conv_w,
           conv_b, w_down):
    depth = w_in.shape[0]
    B = x_prompt.shape[0]
    S = x_sample.shape[0]
    pad = (-B) % SUBLANES
    c_all = jnp.concatenate([c_prompt, jnp.zeros((pad, D_MODEL), F32), c_sample], axis=0)
    xp, xs = x_prompt, x_sample
    outs_p, outs_s = [], []
    for l in range(depth):
        ada = _ada(c_all, w_ada[l], b_ada[l].reshape(1, -1))
        pw = _prep_weights(w_in[l], b_if[l], w_proj_m[l], w_proj_d[l], w_out[l], w_up[l], conv_w[l],
                           conv_b[l], w_down[l])
        gains = (g_pre_mix[l], g_post_mix[l], g_mlstm[l], g_diff[l], g_pre_ffn[l], g_post_ffn[l])
        lam_p = jnp.stack([lambda_q1[l], lambda_k1[l], lambda_q2[l], lambda_k2[l]], axis=0)
        op, os_ = _layer(l, pw, gains, lam_p, ada[:B], ada[B + pad:], xp, xs, cache_k, cache_v, page_table,
                         state_C, state_n, state_m, state_conv, tm_p=512, l_chunk=128,
                         tq=min(1024, xp.shape[1] // 2))
        xp, xs = op[0], os_[0]
        outs_p.append(op[1:])
        outs_s.append(os_[1:])
    stack = lambda outs, i: jnp.stack([o[i] for o in outs])
    return ((xp, xs) + tuple(stack(outs_p, i) for i in range(6)) + tuple(stack(outs_s, i) for i in range(6)))
```

```python
import functools
import math

import jax
import jax.numpy as jnp
from jax import lax
from jax.experimental import pallas as pl
from jax.experimental.pallas import tpu as pltpu

F32 = jnp.float32
BF16 = jnp.bfloat16

D_MODEL = 1024
H_M = 4
DH_M = 128
W_M = H_M * DH_M
H_D = 4
DQK_D = 64
DV_D = 2 * DQK_D
W_D = H_D * DV_D
D_FF = 2816
CONV_W = 3
EPS = 1e-6
LANES = 128
SUBLANES = 8
NEG_BIG = -1e30
LOG2E = 1.4426950408889634
DECODE_SLOTS = 3
ONES_ROWS = 16
MERGE_ROW_GROUPS = 4
FFN_CHUNKS = 1

C_MQ, C_MK, C_MV, C_MO = 0, 512, 1024, 1536
C_DQ, C_DK, C_DV = 2048, 2560, 3072
C_GM, C_GD = 3584, 4608
D_IN_ROWS = 5632
GATE_ROW0 = 2048

VMEM_LIMIT = 56 * 1024 * 1024


def _cparams(sem):
    return pltpu.CompilerParams(dimension_semantics=sem, vmem_limit_bytes=VMEM_LIMIT)


def _const_spec(shape):
    nd = len(shape)
    return pl.BlockSpec(shape, lambda *_: (0,) * nd, pipeline_mode=pl.Buffered(1))


def _rms(x, g):
    ms = jnp.mean(x * x, axis=-1, keepdims=True)
    return x * lax.rsqrt(ms + EPS) * g


def _nt(a, b):
    return lax.dot_general(a, b, (((1,), (1,)), ((), ())), preferred_element_type=F32)


def _tn(a, b):
    return lax.dot_general(a, b, (((0,), (0,)), ((), ())), preferred_element_type=F32)


def _dot(a, b):
    return jnp.dot(a, b, preferred_element_type=F32)


def _idiv(x, n):
    assert n & (n - 1) == 0
    return lax.shift_right_logical(x, jnp.int32(n.bit_length() - 1))


def _imod(x, n):
    assert n & (n - 1) == 0
    return lax.bitwise_and(x, jnp.int32(n - 1))


def _split3(x):
    hi = x.astype(BF16)
    r1 = x - hi.astype(F32)
    mid = r1.astype(BF16)
    r2 = r1 - mid.astype(F32)
    return hi, mid, r2.astype(BF16)


def _select_rows(sel_b, parts):
    return _dot(sel_b, parts[0]) + _dot(sel_b, parts[1]) + _dot(sel_b, parts[2])


def _ada_kernel(c_ref, w_ref, b_ref, o_ref):
    o_ref[...] = _dot(c_ref[...].astype(BF16), w_ref[...].astype(BF16)) + b_ref[...]


def _ada(c_all, w_ada, b_ada):
    n = c_all.shape[0]
    tn = 1024
    return pl.pallas_call(
        _ada_kernel,
        grid=(6 * D_MODEL // tn,),
        in_specs=[
            pl.BlockSpec((n, D_MODEL), lambda j: (0, 0)),
            pl.BlockSpec((D_MODEL, tn), lambda j: (0, j)),
            pl.BlockSpec((1, tn), lambda j: (0, j)),
        ],
        out_specs=pl.BlockSpec((n, tn), lambda j: (0, j)),
        out_shape=jax.ShapeDtypeStruct((n, 6 * D_MODEL), F32),
        compiler_params=_cparams(("arbitrary",)),
        name="adaln",
    )(c_all, w_ada, b_ada)


def _inproj_kernel(x_ref, sh_ref, sc_ref, g_ref, w_ref, wg_ref, bif_ref,
                   mq_ref, mk_ref, mv_ref, so_ref, gt_ref, dq_ref, dkf_ref, dkb_ref, dvf_ref, dvb_ref,
                   sgm_ref, sgd_ref, *, v_transposed):
    h = _rms(x_ref[...], g_ref[...]) * (1.0 + sc_ref[...]) + sh_ref[...]
    hb = h.astype(BF16)

    def seg(c0, c1):
        return _nt(hb, w_ref[c0:c1, :])

    mq_ref[...] = seg(C_MQ, C_MK).astype(BF16)
    mk_ref[...] = (seg(C_MK, C_MV) * (DH_M ** -0.5)).astype(BF16)
    mv_ref[...] = seg(C_MV, C_MO).astype(BF16)
    so_ref[...] = jax.nn.sigmoid(seg(C_MO, C_DQ)).astype(BF16)
    gt_ref[...] = _nt(hb, wg_ref[...]) + bif_ref[...]
    dq_ref[...] = (seg(C_DQ, C_DK) * (DQK_D ** -0.5 * LOG2E)).astype(BF16)
    dk = seg(C_DK, C_DV)
    dkb_ref[...] = dk.astype(BF16)
    dv = seg(C_DV, C_GM)
    tm = dk.shape[0]
    for hh in range(H_D):
        dkf_ref[pl.ds(hh, tm, stride=H_D), :] = dk[:, hh * DV_D:(hh + 1) * DV_D]
        dvf_ref[pl.ds(hh, tm, stride=H_D), :] = dv[:, hh * DV_D:(hh + 1) * DV_D]
    if v_transposed:
        dvt = dv.T.astype(BF16)
        for hh in range(H_D):
            dvb_ref[0, hh, 0:DV_D, :] = dvt[hh * DV_D:(hh + 1) * DV_D, :]
            dvb_ref[0, hh, DV_D:DV_D + ONES_ROWS, :] = jnp.ones((ONES_ROWS, tm), BF16)
    else:
        dvb_ref[...] = dv.astype(BF16)
    sgm_ref[...] = jax.nn.sigmoid(seg(C_GM, C_GD)).astype(BF16)
    sgd_ref[...] = jax.nn.sigmoid(seg(C_GD, D_IN_ROWS)).astype(BF16)


def _mod_spec(tm, rows_per_mod):
    if rows_per_mod is None:
        return pl.BlockSpec((tm, D_MODEL), lambda i: (i, 0))
    tiles = rows_per_mod // tm
    return pl.BlockSpec((None, 1, D_MODEL), lambda i: (i // tiles, 0, 0))


def _inproj(x, shift, scale, g, w, wg, bif, *, tm, rows_per_mod, v_transposed):
    n = x.shape[0]
    row = lambda width: pl.BlockSpec((tm, width), lambda i: (i, 0))
    sds = lambda width, dt: jax.ShapeDtypeStruct((n, width), dt)
    head_rows = pl.BlockSpec((tm * H_D, DV_D), lambda i: (i, 0))
    head_rows_sds = jax.ShapeDtypeStruct((n * H_D, DV_D), F32)
    if v_transposed:
        nb = n // rows_per_mod
        tiles = rows_per_mod // tm
        dvb_spec = pl.BlockSpec((1, H_D, DV_D + ONES_ROWS, tm), lambda i: (i // tiles, 0, 0, i % tiles))
        dvb_sds = jax.ShapeDtypeStruct((nb, H_D, DV_D + ONES_ROWS, rows_per_mod), BF16)
    else:
        dvb_spec, dvb_sds = row(W_D), sds(W_D, BF16)
    return pl.pallas_call(
        functools.partial(_inproj_kernel, v_transposed=v_transposed),
        grid=(n // tm,),
        in_specs=[row(D_MODEL), _mod_spec(tm, rows_per_mod), _mod_spec(tm, rows_per_mod),
                  _const_spec((1, D_MODEL)), _const_spec((D_IN_ROWS, D_MODEL)),
                  _const_spec((2 * LANES, D_MODEL)), _const_spec((1, 2 * LANES))],
        out_specs=[row(W_M), row(W_M), row(W_M), row(W_M), row(2 * LANES), row(W_D), head_rows, row(W_D),
                   head_rows, dvb_spec, row(D_MODEL), row(D_MODEL)],
        out_shape=[sds(W_M, BF16), sds(W_M, BF16), sds(W_M, BF16), sds(W_M, BF16), sds(2 * LANES, F32),
                   sds(W_D, BF16), head_rows_sds, sds(W_D, BF16), head_rows_sds, dvb_sds,
                   sds(D_MODEL, BF16), sds(D_MODEL, BF16)],
        compiler_params=_cparams(("arbitrary",)),
        name="inproj",
    )(x, shift, scale, g, w, wg, bif)


def _mlstm_kernel(q_ref, k_ref, v_ref, so_ref, gt_ref, c0_ref, n0_ref, m0_ref, g_ref,
                  hm_ref, c_ref, n_ref, m_ref, cst, nscr, mscr, *, bb, L):
    j = pl.program_id(1)

    @pl.when(j == 0)
    def _init():
        for b in range(bb):
            for h in range(H_M):
                idx = b * H_M + h
                cst[idx] = c0_ref[b, h]
                nscr[idx] = jnp.broadcast_to(n0_ref[b, h:h + 1, :], (SUBLANES, LANES))
                mscr[idx] = jnp.broadcast_to(m0_ref[b, h:h + 1, :], (SUBLANES, LANES))

    rowi = lax.broadcasted_iota(jnp.int32, (L, L), 0)
    coli = lax.broadcasted_iota(jnp.int32, (L, L), 1)
    tril = rowi >= coli
    tril_b = jnp.where(tril, 1.0, 0.0).astype(BF16)
    ones3_b = jnp.ones((L, 3 * LANES), BF16)
    ones_b = jnp.ones((L, LANES), BF16)
    ones_sq_b = jnp.ones((DH_M, DH_M), BF16)
    lane = lax.broadcasted_iota(jnp.int32, (L, LANES), 1)

    chains = [(b, h) for b in range(bb) for h in range(H_M)]
    hsl = lambda h: slice(h * DH_M, (h + 1) * DH_M)
    rep = lambda x, h: jnp.broadcast_to(x[:, h:h + 1], (L, LANES))

    gts = [gt_ref[b] for b in range(bb)]
    logfs = [-(jnp.maximum(-g[:, LANES:], 0.0) + jnp.log(1.0 + jnp.exp(-jnp.abs(g[:, LANES:])))) for g in gts]
    i_rep = [rep(gts[b][:, 0:LANES], h) for b, h in chains]
    lf3 = [_split3(x) for x in logfs]
    f_all = [_dot(tril_b, x[0]) + _dot(tril_b, x[1]) + _dot(tril_b, x[2]) for x in lf3]
    F = [rep(f_all[b], h) for b, h in chains]
    a = [i - f for i, f in zip(i_rep, F)]
    a3 = [_split3(jnp.where(lane == 0, x, 0.0)) for x in a]
    A = [_nt(ones3_b, jnp.concatenate(x, axis=1)) for x in a3]
    cm = [jnp.broadcast_to(jnp.max(jnp.where(tril, x, -jnp.inf), axis=1, keepdims=True), (L, LANES))
          for x in A]
    m_prev = [mscr[b * H_M + h][0:1, :] for b, h in chains]
    m_tok = [f + jnp.maximum(mp, c) for f, mp, c in zip(F, m_prev, cm)]
    dmat = [jnp.exp(jnp.where(tril, (f - m)[:, 0:L] + x, NEG_BIG)) for f, m, x in zip(F, m_tok, A)]
    q = [q_ref[b, :, hsl(h)] for b, h in chains]
    k = [k_ref[b, :, hsl(h)] for b, h in chains]
    v = [v_ref[b, :, hsl(h)] for b, h in chains]
    s = [(_nt(qq, kk) * d).astype(BF16) for qq, kk, d in zip(q, k, dmat)]
    nd = [_dot(ss, jnp.concatenate([vv, ones_b], axis=1)) for ss, vv in zip(s, v)]
    cn = [jnp.concatenate([cst[i].astype(BF16),
                           jnp.broadcast_to(nscr[i][0:1, :], (DH_M, DH_M)).astype(BF16)], axis=0)
          for i in range(len(chains))]
    qc = [_nt(qq, x) for qq, x in zip(q, cn)]
    inter = [jnp.exp(f + mp - m) for f, mp, m in zip(F, m_prev, m_tok)]
    num = [x[:, 0:DH_M] + w * y[:, 0:DH_M] for x, w, y in zip(nd, inter, qc)]
    den = [x[:, DH_M:] + w * y[:, DH_M:] for x, w, y in zip(nd, inter, qc)]
    hh = [x / jnp.maximum(jnp.abs(y), jnp.exp(-m)) for x, y, m in zip(num, den, m_tok)]
    ms = [_dot((x * x).astype(BF16), ones_sq_b) * (1.0 / DH_M) for x in hh]
    for (b, h), x, y in zip(chains, hh, ms):
        hn = x * lax.rsqrt(y + EPS) * g_ref[:, hsl(h)]
        hm_ref[b, :, hsl(h)] = (so_ref[b, :, hsl(h)].astype(F32) * hn).astype(BF16)
    m_end = [m[L - 1:L, :] for m in m_tok]
    f_end = [f[L - 1:L, :] for f in F]
    w_end = [jnp.exp(fe + x - me) for fe, x, me in zip(f_end, a, m_end)]
    decay = [jnp.exp(fe + mp - me) for fe, mp, me in zip(f_end, m_prev, m_end)]
    kw = [kk.astype(F32) * w for kk, w in zip(k, w_end)]
    upd = [_tn(vv, x.astype(BF16)) for vv, x in zip(v, kw)]
    for i in range(len(chains)):
        cst[i] = decay[i] * cst[i] + upd[i]
        n_new = decay[i] * nscr[i][0:1, :] + jnp.sum(kw[i], axis=0, keepdims=True)
        nscr[i] = jnp.broadcast_to(n_new, (SUBLANES, LANES))
        mscr[i] = jnp.broadcast_to(m_end[i], (SUBLANES, LANES))

    @pl.when(j == pl.num_programs(1) - 1)
    def _fin():
        for i, (b, h) in enumerate(chains):
            c_ref[b, h] = cst[i]
            n_ref[b, h:h + 1, :] = nscr[i][0:1, :]
            m_ref[b, h:h + 1, :] = mscr[i][0:1, :]


def _mlstm(q, k, v, so, gt, c0, n0, m0, g, *, bb, L):
    B, T, _ = q.shape
    tok = lambda width: pl.BlockSpec((bb, L, width), lambda i, j: (i, j, 0))
    st3 = pl.BlockSpec((bb, H_M, LANES), lambda i, j: (i, 0, 0))
    st4 = pl.BlockSpec((bb, H_M, DH_M, DH_M), lambda i, j: (i, 0, 0, 0))
    return pl.pallas_call(
        functools.partial(_mlstm_kernel, bb=bb, L=L),
        grid=(B // bb, T // L),
        in_specs=[tok(W_M), tok(W_M), tok(W_M), tok(W_M), tok(2 * LANES), st4, st3, st3,
                  pl.BlockSpec((1, W_M), lambda i, j: (0, 0))],
        out_specs=[tok(W_M), st4, st3, st3],
        out_shape=[jax.ShapeDtypeStruct((B, T, W_M), BF16),
                   jax.ShapeDtypeStruct((B, H_M, DH_M, DH_M), F32),
                   jax.ShapeDtypeStruct((B, H_M, LANES), F32),
                   jax.ShapeDtypeStruct((B, H_M, LANES), F32)],
        scratch_shapes=[pltpu.VMEM((bb * H_M, DH_M, DH_M), F32),
                        pltpu.VMEM((bb * H_M, SUBLANES, LANES), F32),
                        pltpu.VMEM((bb * H_M, SUBLANES, LANES), F32)],
        compiler_params=_cparams(("arbitrary", "arbitrary")),
        name="mlstm",
    )(q, k, v, so, gt, c0, n0, m0, g)


def _lambda(lam_ref, lam_init):
    p = lam_ref[...]
    l1 = jnp.sum(p[0:1, :] * p[1:2, :], axis=-1, keepdims=True)
    l2 = jnp.sum(p[2:3, :] * p[3:4, :], axis=-1, keepdims=True)
    return jnp.exp(l1) - jnp.exp(l2) + lam_init


def _attn_kernel(q_ref, k_ref, vt_ref, g_ref, lam_ref, o_ref, qm, acc, mrow, sa_ref, sb_ref, ca_ref,
                 cb_ref, bias_ref, *, tq, tk, lam_init):
    g = pl.program_id(2)
    lane = lax.broadcasted_iota(jnp.int32, (tq, 2 * DQK_D), 1)
    for w in range(2):
        q = q_ref[0, w * tq:(w + 1) * tq, :]
        zero = jnp.zeros_like(q)
        qm[w, 0:tq, :] = jnp.where(lane < DQK_D, q, zero)
        qm[w, tq:2 * tq, :] = jnp.where(lane >= DQK_D, q, zero)
    acc[...] = jnp.zeros_like(acc)
    mrow[...] = jnp.full_like(mrow, NEG_BIG)
    buf_a, buf_b = (sa_ref, ca_ref), (sb_ref, cb_ref)

    @pl.when(g == 0)
    def _causal_bias():
        r = lax.broadcasted_iota(jnp.int32, (tk, 2 * tq), 0)
        c = lax.broadcasted_iota(jnp.int32, (tk, 2 * tq), 1)
        bias_ref[...] = jnp.where(r <= jnp.where(c >= tq, c - tq, c), 0.0, NEG_BIG)

    def scores(kb, w, buf):
        s_buf, c_buf = buf
        k0 = pl.multiple_of(kb * tk, tk)
        s = _nt(k_ref[0, pl.ds(k0, tk), :], qm[w])
        s_buf[...] = s
        c_buf[...] = jnp.max(s, axis=0, keepdims=True)

    def softmax_pv(kb, w, buf, masked):
        s_buf, c_buf = buf
        k0 = pl.multiple_of(kb * tk, tk)
        s = s_buf[...]
        if masked:
            s = s + bias_ref[...]
            c = jnp.max(s, axis=0, keepdims=True)
        else:
            c = c_buf[...]
        m_old = mrow[w]
        m_new = jnp.maximum(m_old, c)
        alpha = jnp.exp2(m_old - m_new)
        p = jnp.exp2(s - m_new)
        acc[w] = acc[w] * alpha + _dot(vt_ref[0, 0, :, pl.ds(k0, tk)], p.astype(BF16))
        mrow[w] = m_new

    def finalize(w):
        l = acc[w, DV_D:DV_D + 1, :]
        a = acc[w, 0:DV_D, :]
        lam = _lambda(lam_ref, lam_init)
        o = a[:, 0:tq] / l[:, 0:tq] - lam * (a[:, tq:2 * tq] / l[:, tq:2 * tq])
        ms = jnp.mean(o * o, axis=0, keepdims=True)
        on = o * lax.rsqrt(ms + EPS) * (g_ref[0] * (1.0 - lam_init))
        o_ref[0, w * tq:(w + 1) * tq, :] = on.T.astype(BF16)

    def pairs(w, first, second):
        def body(j, carry):
            kb = 2 * j
            scores(kb + 1, w, second)
            softmax_pv(kb, w, first, False)
            scores(kb + 2, w, first)
            softmax_pv(kb + 1, w, second, False)
            return carry

        lax.fori_loop(0, g, body, 0)

    scores(0, 0, buf_a)
    pairs(0, buf_a, buf_b)
    scores(0, 1, buf_b)
    softmax_pv(2 * g, 0, buf_a, True)
    finalize(0)
    pairs(1, buf_b, buf_a)
    scores(2 * g + 1, 1, buf_a)
    softmax_pv(2 * g, 1, buf_b, False)
    softmax_pv(2 * g + 1, 1, buf_a, True)
    finalize(1)


def _attn_prompt(q, k, vt, g3, lam_p, *, tq, lam_init):
    B, T, _ = q.shape
    tk = tq
    vec = lambda: pltpu.VMEM((2, 1, 2 * tq), F32)
    buf = lambda: pltpu.VMEM((tk, 2 * tq), F32)
    cmax = lambda: pltpu.VMEM((1, 2 * tq), F32)
    return pl.pallas_call(
        functools.partial(_attn_kernel, tq=tq, tk=tk, lam_init=lam_init),
        grid=(B, H_D, T // (2 * tq)),
        in_specs=[pl.BlockSpec((1, 2 * tq, DV_D), lambda b, h, i: (b, i, h)),
                  pl.BlockSpec((1, T, DV_D), lambda b, h, i: (b, 0, h)),
                  pl.BlockSpec((1, 1, DV_D + ONES_ROWS, T), lambda b, h, i: (b, h, 0, 0)),
                  pl.BlockSpec((1, DV_D, 1), lambda b, h, i: (h, 0, 0)),
                  pl.BlockSpec((4, DQK_D), lambda b, h, i: (0, 0))],
        out_specs=pl.BlockSpec((1, 2 * tq, DV_D), lambda b, h, i: (b, i, h)),
        out_shape=jax.ShapeDtypeStruct((B, T, W_D), BF16),
        scratch_shapes=[pltpu.VMEM((2, 2 * tq, DV_D), BF16), pltpu.VMEM((2, DV_D + ONES_ROWS, 2 * tq), F32),
                        vec(), buf(), buf(), cmax(), cmax(), buf()],
        compiler_params=_cparams(("arbitrary", "arbitrary", "arbitrary")),
        name="attn_prompt",
    )(q, k, vt, g3, lam_p)


def _decode_kernel(pt_ref, q_ref, kn_ref, vn_ref, g_ref, lam_ref, ck_hbm, cv_hbm, o_ref, kbuf, vbuf, sems, *,
                   n_pages, page, t_new, lam_init):
    nr = 2 * H_D * t_new
    pr = page * H_D
    nn = t_new * H_D
    b = pl.program_id(0)
    n_seq = pl.num_programs(0)
    ahead = DECODE_SLOTS - 1

    def page_copies(seq, slot):
        cps = []
        for j in range(n_pages):
            row0 = pl.multiple_of(pt_ref[seq, j] * pr, pr)
            cps.append(pltpu.make_async_copy(ck_hbm.at[pl.ds(row0, pr), :], kbuf.at[slot, j], sems.at[0, slot]))
            cps.append(pltpu.make_async_copy(cv_hbm.at[pl.ds(row0, pr), :], vbuf.at[slot, j], sems.at[1, slot]))
        return cps

    @pl.when(b == 0)
    def _prime():
        for s in range(ahead):
            for cp in page_copies(s, s):
                cp.start()

    @pl.when(b + ahead < n_seq)
    def _prefetch():
        for cp in page_copies(b + ahead, lax.rem(b + ahead, DECODE_SLOTS)):
            cp.start()

    slot = lax.rem(b, DECODE_SLOTS)
    for cp in page_copies(b, slot):
        cp.wait()
    k_refs = [kbuf.at[slot, j] for j in range(n_pages)]
    v_refs = [vbuf.at[slot, j] for j in range(n_pages)]
    q = q_ref[0]
    rq = lax.broadcasted_iota(jnp.int32, (nr, DV_D), 0)
    cq = lax.broadcasted_iota(jnp.int32, (nr, DV_D), 1)
    qb = jnp.where(_idiv(rq, H_D * t_new) == _idiv(cq, DQK_D), q, jnp.zeros_like(q))

    r = lax.broadcasted_iota(jnp.int32, (nr, pr), 0)
    c = lax.broadcasted_iota(jnp.int32, (nr, pr), 1)
    head_ok = _imod(_idiv(r, t_new), H_D) == _imod(c, H_D)
    s_old = jnp.concatenate(
        [jnp.where(head_ok, _nt(qb, k_refs[j][...].astype(BF16)), NEG_BIG) for j in range(n_pages)], axis=1)
    rn = lax.broadcasted_iota(jnp.int32, (nr, nn), 0)
    cn = lax.broadcasted_iota(jnp.int32, (nr, nn), 1)
    s_new = _nt(qb, kn_ref[0].astype(BF16))
    s_new = jnp.where(_imod(_idiv(rn, t_new), H_D) == _imod(cn, H_D), s_new, NEG_BIG)
    s_new = jnp.where(_idiv(cn, H_D) <= _imod(rn, t_new), s_new, NEG_BIG)
    m = jnp.maximum(jnp.max(s_old, axis=1, keepdims=True), jnp.max(s_new, axis=1, keepdims=True))
    p_old = jnp.exp2(s_old - m)
    p_new = jnp.exp2(s_new - m)
    l = jnp.sum(p_old, axis=1, keepdims=True) + jnp.sum(p_new, axis=1, keepdims=True)
    pb = p_old.astype(BF16)
    out = _dot(p_new.astype(BF16), vn_ref[0].astype(BF16))
    for j in range(n_pages):
        out = out + _dot(pb[:, j * pr:(j + 1) * pr], v_refs[j][...].astype(BF16))
    o_r = out / l
    half = H_D * t_new
    lam = _lambda(lam_ref, lam_init)
    o = o_r[0:half, :] - lam * o_r[half:2 * half, :]
    o_ref[0] = (_rms(o, g_ref[...]) * (1.0 - lam_init)).astype(BF16)


def _attn_decode(page_table, q_rep, k_new, v_new, cache_k2, cache_v2, g_rows, lam_p, *, page, t_new, lam_init):
    n_seq, n_pages = page_table.shape
    nr = 2 * H_D * t_new
    half = H_D * t_new

    assert n_seq >= DECODE_SLOTS - 1
    page_set = lambda: pltpu.VMEM((DECODE_SLOTS, n_pages, page * H_D, DV_D), F32)
    grid_spec = pltpu.PrefetchScalarGridSpec(
        num_scalar_prefetch=1,
        grid=(n_seq,),
        in_specs=[pl.BlockSpec((1, nr, DV_D), lambda b, pt: (b, 0, 0)),
                  pl.BlockSpec((1, half, DV_D), lambda b, pt: (b, 0, 0)),
                  pl.BlockSpec((1, half, DV_D), lambda b, pt: (b, 0, 0)),
                  pl.BlockSpec((half, DV_D), lambda b, pt: (0, 0)),
                  pl.BlockSpec((4, DQK_D), lambda b, pt: (0, 0)),
                  pl.BlockSpec(memory_space=pl.ANY), pl.BlockSpec(memory_space=pl.ANY)],
        out_specs=pl.BlockSpec((1, half, DV_D), lambda b, pt: (b, 0, 0)),
        scratch_shapes=[page_set(), page_set(), pltpu.SemaphoreType.DMA((2, DECODE_SLOTS))],
    )
    return pl.pallas_call(
        functools.partial(_decode_kernel, n_pages=n_pages, page=page, t_new=t_new, lam_init=lam_init),
        grid_spec=grid_spec,
        out_shape=jax.ShapeDtypeStruct((n_seq, half, DV_D), BF16),
        compiler_params=_cparams(("arbitrary",)),
        name="attn_decode",
    )(page_table, q_rep, k_new, v_new, g_rows, lam_p, cache_k2, cache_v2)


def _merge_kernel(x_ref, hm_ref, hd_ref, sgm_ref, sgd_ref, gate1_ref, sh2_ref, sc2_ref, gpost_ref, gpre_ref,
                  wpm_ref, wpd_ref, wout_ref, x1_ref, h2_ref):
    tm = x_ref.shape[0]
    rows = [slice(i * tm // MERGE_ROW_GROUPS, (i + 1) * tm // MERGE_ROW_GROUPS) for i in range(MERGE_ROW_GROUPS)]
    mod = lambda ref, r: ref[...] if ref.shape[0] == 1 else ref[r, :]
    pm = [_dot(hm_ref[r, :], wpm_ref[...]) for r in rows]
    pd = [_dot(hd_ref[r, :], wpd_ref[...]) for r in rows]
    merged = [(sgm_ref[r, :].astype(F32) * a + sgd_ref[r, :].astype(F32) * b).astype(BF16)
              for r, a, b in zip(rows, pm, pd)]
    y = [_dot(m, wout_ref[...]) for m in merged]
    for r, yy in zip(rows, y):
        x1 = x_ref[r, :] + mod(gate1_ref, r) * _rms(yy, gpost_ref[...])
        x1_ref[r, :] = x1
        h2_ref[r, :] = (_rms(x1, gpre_ref[...]) * (1.0 + mod(sc2_ref, r)) + mod(sh2_ref, r)).astype(BF16)


def _merge(x, hm, hd, sgm, sgd, gate1, sh2, sc2, gpost, gpre, wpm, wpd, wout, *, tm, rows_per_mod):
    n = x.shape[0]
    row = lambda width: pl.BlockSpec((tm, width), lambda i: (i, 0))
    mod = _mod_spec(tm, rows_per_mod)
    return pl.pallas_call(
        _merge_kernel,
        grid=(n // tm,),
        in_specs=[row(D_MODEL), row(W_M), row(W_D), row(D_MODEL), row(D_MODEL), mod, mod, mod,
                  _const_spec((1, D_MODEL)), _const_spec((1, D_MODEL)),
                  _const_spec((W_M, D_MODEL)), _const_spec((W_D, D_MODEL)), _const_spec((D_MODEL, D_MODEL))],
        out_specs=[row(D_MODEL), row(D_MODEL)],
        out_shape=[jax.ShapeDtypeStruct((n, D_MODEL), F32), jax.ShapeDtypeStruct((n, D_MODEL), BF16)],
        compiler_params=_cparams(("arbitrary",)),
        name="merge",
    )(x, hm, hd, sgm, sgd, gate1, sh2, sc2, gpost, gpre, wpm, wpd, wout)


def _gelu_tanh(x):
    return 0.5 * x * (1.0 + jnp.tanh(math.sqrt(2.0 / math.pi) * (x + 0.044715 * (x * x * x))))


def _ffn_kernel(*refs, tm, tiles_per_seq, t_seq):
    if tiles_per_seq is None:
        (h2_ref, x1_ref, gate2_ref, gpost_ref, wup_ref, cw_ref, cb_ref, wdn_ref, st_ref,
         y_ref, tail_ref, a_scr, p1_scr, p2_scr) = refs
        a_scr[0:SUBLANES, :] = jnp.zeros((SUBLANES, D_FF), F32)
        n_st = st_ref.shape[0]
        r = lax.broadcasted_iota(jnp.int32, (tm, n_st), 0)
        j = lax.broadcasted_iota(jnp.int32, (tm, n_st), 1)
        same_seq = _idiv(r, t_seq) == _idiv(j, CONV_W - 1)
        t_r, s_j = _imod(r, t_seq), _imod(j, CONV_W - 1)
        sel1 = jnp.where(same_seq, jnp.where(t_r == 0, jnp.where(s_j == 1, 1.0, 0.0), 0.0), 0.0)
        sel2 = jnp.where(same_seq, jnp.where(t_r == s_j, 1.0, 0.0), 0.0)
        st3 = _split3(st_ref[...])
        p1_scr[...] = _select_rows(sel1.astype(BF16), st3)
        p2_scr[...] = _select_rows(sel2.astype(BF16), st3)
    else:
        (h2_ref, x1_ref, gate2_ref, gpost_ref, wup_ref, cw_ref, cb_ref, wdn_ref,
         y_ref, tail_ref, a_scr) = refs
        @pl.when(pl.program_id(0) % tiles_per_seq == 0)
        def _zero():
            a_scr[0:SUBLANES, :] = jnp.zeros((SUBLANES, D_FF), F32)

    h2 = h2_ref[...]
    ch = D_FF // FFN_CHUNKS
    f = None
    for c in range(FFN_CHUNKS):
        cs = slice(c * ch, (c + 1) * ch)
        a = _dot(h2, wup_ref[:, c * ch:(c + 1) * ch])
        b = _dot(h2, wup_ref[:, D_FF + c * ch:D_FF + (c + 1) * ch])
        a_scr[SUBLANES:SUBLANES + tm, cs] = a
        prev1 = a_scr[SUBLANES - 1:SUBLANES - 1 + tm, cs]
        prev2 = a_scr[SUBLANES - 2:SUBLANES - 2 + tm, cs]
        if tiles_per_seq is None:
            tpos = lax.broadcasted_iota(jnp.int32, (tm, ch), 0) % t_seq
            prev1 = jnp.where(tpos >= 1, prev1, p1_scr[:, cs])
            prev2 = jnp.where(tpos >= 2, prev2, p2_scr[:, cs])
        conv = cb_ref[:, cs] + cw_ref[0:1, cs] * prev2 + cw_ref[1:2, cs] * prev1 + cw_ref[2:3, cs] * a
        act = (_gelu_tanh(conv) * b).astype(BF16)
        fc = _dot(act, wdn_ref[cs, :])
        f = fc if f is None else f + fc
    if tiles_per_seq is None:
        sel_t = jnp.where(_idiv(r, t_seq) == _idiv(j, CONV_W - 1),
                          jnp.where(t_r == s_j + (t_seq - (CONV_W - 1)), 1.0, 0.0), 0.0).astype(BF16)
        a3 = _split3(a_scr[SUBLANES:SUBLANES + tm, :])
        tail_ref[...] = _tn(sel_t, a3[0]) + _tn(sel_t, a3[1]) + _tn(sel_t, a3[2])
    else:
        tail = a_scr[tm:tm + SUBLANES, :]
        tail_ref[0] = tail
        a_scr[0:SUBLANES, :] = tail
    y_ref[...] = x1_ref[...] + gate2_ref[...] * _rms(f, gpost_ref[...])


def _ffn(h2, x1, gate2, gpost, wup, cw, cb, wdn, st=None, *, tm, rows_per_mod, t_seq):
    n = h2.shape[0]
    row = lambda width: pl.BlockSpec((tm, width), lambda i: (i, 0))
    in_specs = [row(D_MODEL), row(D_MODEL), _mod_spec(tm, rows_per_mod), _const_spec((1, D_MODEL)),
                _const_spec((D_MODEL, 2 * D_FF)), _const_spec((SUBLANES, D_FF)), _const_spec((1, D_FF)),
                _const_spec((D_FF, D_MODEL))]
    args = [h2, x1, gate2, gpost, wup, cw, cb, wdn]
    scratch = [pltpu.VMEM((tm + SUBLANES, D_FF), F32)]
    if rows_per_mod is None:
        assert n == tm and t_seq >= CONV_W - 1
        tiles_per_seq = None
        st_rows = (n // t_seq) * (CONV_W - 1)
        st_spec = pl.BlockSpec((st_rows, D_FF), lambda i: (0, 0))
        in_specs += [st_spec]
        args += [st]
        out_specs = [row(D_MODEL), st_spec]
        out_shape = [jax.ShapeDtypeStruct((n, D_MODEL), F32), jax.ShapeDtypeStruct((st_rows, D_FF), F32)]
        scratch += [pltpu.VMEM((tm, D_FF), F32), pltpu.VMEM((tm, D_FF), F32)]
    else:
        tiles_per_seq = rows_per_mod // tm
        out_specs = [row(D_MODEL), pl.BlockSpec((1, SUBLANES, D_FF), lambda i: (i // tiles_per_seq, 0, 0))]
        out_shape = [jax.ShapeDtypeStruct((n, D_MODEL), F32),
                     jax.ShapeDtypeStruct((n // rows_per_mod, SUBLANES, D_FF), F32)]
    return pl.pallas_call(
        functools.partial(_ffn_kernel, tm=tm, tiles_per_seq=tiles_per_seq, t_seq=t_seq),
        grid=(n // tm,),
        in_specs=in_specs,
        out_specs=out_specs,
        out_shape=out_shape,
        scratch_shapes=scratch,
        compiler_params=_cparams(("arbitrary",)),
        name="ffn",
    )(*args)


def _win_prep_kernel(w_ref, o_ref):
    o_ref[0:GATE_ROW0, :] = w_ref[0:GATE_ROW0, :].astype(BF16)
    o_ref[GATE_ROW0:, :] = w_ref[GATE_ROW0 + 2 * H_M:, :].astype(BF16)


def _win_prep(w_t):
    d_in = w_t.shape[0]
    tc = 256
    return pl.pallas_call(
        _win_prep_kernel,
        grid=(D_MODEL // tc,),
        in_specs=[pl.BlockSpec((d_in, tc), lambda i: (0, i))],
        out_specs=pl.BlockSpec((D_IN_ROWS, tc), lambda i: (0, i)),
        out_shape=jax.ShapeDtypeStruct((D_IN_ROWS, D_MODEL), BF16),
        compiler_params=_cparams(("arbitrary",)),
        name="win_prep",
    )(w_t)


def _prep_weights(w_in, b_if, w_proj_m, w_proj_d, w_out, w_up, conv_w, conv_b, w_down):
    assert w_in.shape[1] == D_IN_ROWS + 2 * H_M
    w_t = w_in.T
    w_rows = _win_prep(w_t)
    gpad = jnp.zeros((LANES - H_M, D_MODEL), F32)
    gates = w_t[GATE_ROW0:GATE_ROW0 + 2 * H_M]
    wg = jnp.concatenate([gates[:H_M], gpad, gates[H_M:], gpad], axis=0).astype(BF16)
    bpad = jnp.zeros((LANES - H_M,), F32)
    bif = jnp.concatenate([b_if[:H_M], bpad, b_if[H_M:], bpad]).reshape(1, 2 * LANES)
    cw = jnp.concatenate([conv_w, jnp.zeros((SUBLANES - CONV_W, D_FF), F32)], axis=0)
    return dict(w_in=w_rows, wg=wg, bif=bif, wpm=w_proj_m.astype(BF16), wpd=w_proj_d.astype(BF16),
                wout=w_out.astype(BF16), wup=w_up.astype(BF16), cw=cw, cb=conv_b.reshape(1, D_FF),
                wdn=w_down.astype(BF16))


def _layer(l, pw, gains, lam_p, ada_p, ada_s, x_prompt, x_sample, cache_k, cache_v, page_table,
           state_C, state_n, state_m, state_conv, *, tm_p, l_chunk, tq):
    B, T, _ = x_prompt.shape
    S, Ts, _ = x_sample.shape
    lam_init = 0.8 - 0.6 * math.exp(-0.3 * l)
    g_pre_mix, g_post_mix, g_mlstm, g_diff, g_pre_ffn, g_post_ffn = gains
    row = lambda g: g.reshape(1, -1)

    xp = x_prompt.reshape(B * T, D_MODEL)
    modp = [a.reshape(B, 1, D_MODEL) for a in jnp.split(ada_p, 6, axis=-1)]
    (mq, mk, mv, so, gt, dq, dkf, dkb, dvf, dvt, sgm, sgd) = _inproj(
        xp, modp[0], modp[1], row(g_pre_mix), pw["w_in"], pw["wg"], pw["bif"], tm=tm_p, rows_per_mod=T, v_transposed=True)
    tok = lambda a: a.reshape(B, T, -1)
    zc = jnp.zeros((B, H_M, DH_M, DH_M), F32)
    zn = jnp.zeros((B, H_M, LANES), F32)
    hm, Cp, n_p, m_p = _mlstm(tok(mq), tok(mk), tok(mv), tok(so), tok(gt), zc, zn, zn, row(g_mlstm),
                              bb=B, L=l_chunk)
    hd = _attn_prompt(tok(dq), tok(dkb), dvt, g_diff.reshape(H_D, DV_D, 1), lam_p, tq=tq, lam_init=lam_init)
    x1, h2 = _merge(xp, hm.reshape(B * T, W_M), hd.reshape(B * T, W_D), sgm, sgd, modp[2], modp[3], modp[4],
                    row(g_post_mix), row(g_pre_ffn), pw["wpm"], pw["wpd"], pw["wout"], tm=tm_p, rows_per_mod=T)
    yp, tail = _ffn(h2, x1, modp[5], row(g_post_ffn), pw["wup"], pw["cw"], pw["cb"], pw["wdn"],
                    tm=tm_p // 2, rows_per_mod=T, t_seq=T)
    out_p = (yp.reshape(B, T, D_MODEL), dkf.reshape(B, T, H_D, 2 * DQK_D), dvf.reshape(B, T, H_D, DV_D),
             Cp, n_p, m_p[:, :, 0], tail[:, SUBLANES - (CONV_W - 1):, :])

    ns = S * Ts
    xs = x_sample.reshape(ns, D_MODEL)
    mods = [jnp.repeat(a, Ts, axis=0) for a in jnp.split(ada_s, 6, axis=-1)]
    (mq, mk, mv, so, gt, dq, dkf, dkb, dvf, dvb, sgm, sgd) = _inproj(
        xs, mods[0], mods[1], row(g_pre_mix), pw["w_in"], pw["wg"], pw["bif"], tm=ns, rows_per_mod=None, v_transposed=False)
    Lp = 16
    padt = lambda a: jnp.pad(a.reshape(S, Ts, -1), ((0, 0), (0, Lp - Ts), (0, 0)))
    gt3 = gt.reshape(S, Ts, 2 * LANES)
    gpad = jnp.concatenate([jnp.full((S, Lp - Ts, LANES), NEG_BIG, F32),
                            jnp.full((S, Lp - Ts, LANES), -NEG_BIG, F32)], axis=-1)
    gt_p = jnp.concatenate([gt3, gpad], axis=1)
    m0 = jnp.broadcast_to(state_m[l][:, :, None], (S, H_M, LANES))
    hm, Cs, n_s, m_s = _mlstm(padt(mq), padt(mk), padt(mv), padt(so), gt_p, state_C[l], state_n[l], m0,
                              row(g_mlstm), bb=8, L=Lp)
    hm = hm[:, :Ts, :].reshape(ns, W_M)
    n_pool, page = cache_k.shape[1], cache_k.shape[2]
    ck2 = cache_k[l].reshape(n_pool * page * H_D, DV_D)
    cv2 = cache_v[l].reshape(n_pool * page * H_D, DV_D)
    q_rep = dq.reshape(S, Ts, H_D, DV_D).transpose(0, 2, 1, 3)
    q_rep = jnp.broadcast_to(q_rep[:, None], (S, 2, H_D, Ts, DV_D)).reshape(S, 2 * H_D * Ts, DV_D)
    new_rows = lambda a: a.reshape(S, Ts * H_D, DV_D)
    g_rows = jnp.repeat(g_diff.reshape(H_D, DV_D), Ts, axis=0)
    hd = _attn_decode(page_table, q_rep, new_rows(dkf), new_rows(dvf), ck2, cv2, g_rows, lam_p,
                      page=page, t_new=Ts, lam_init=lam_init)
    hd = hd.reshape(S, H_D, Ts, DV_D).transpose(0, 2, 1, 3).reshape(ns, W_D)
    x1, h2 = _merge(xs, hm, hd, sgm, sgd, mods[2], mods[3], mods[4], row(g_post_mix), row(g_pre_ffn),
                    pw["wpm"], pw["wpd"], pw["wout"], tm=ns, rows_per_mod=None)
    st = state_conv[l].reshape(S * (CONV_W - 1), D_FF)
    ys, conv_s = _ffn(h2, x1, mods[5], row(g_post_ffn), pw["wup"], pw["cw"], pw["cb"], pw["wdn"], st,
                      tm=ns, rows_per_mod=None, t_seq=Ts)
    out_s = (ys.reshape(S, Ts, D_MODEL), dkf.reshape(S, Ts, H_D, 2 * DQK_D), dvf.reshape(S, Ts, H_D, DV_D),
             Cs, n_s, m_s[:, :, 0], conv_s.reshape(S, CONV_W - 1, D_FF))
    return out_p, out_s


def kernel(x_prompt, x_sample, c_prompt, c_sample, cache_k, cache_v, page_table, state_C, state_n, state_m,
           state_conv, w_ada, b_ada, g_pre_mix, g_post_mix, w_in, b_if, g_mlstm, lambda_q1, lambda_k1,
           lambda_q2, lambda_k2, g_diff, w_proj_m, w_proj_d, w_out, g_pre_ffn, g_post_ffn, w_up, conv_w,
           conv_b, w_down):
    depth = w_in.shape[0]
    B = x_prompt.shape[0]
    S = x_sample.shape[0]
    pad = (-B) % SUBLANES
    c_all = jnp.concatenate([c_prompt, jnp.zeros((pad, D_MODEL), F32), c_sample], axis=0)
    xp, xs = x_prompt, x_sample
    outs_p, outs_s = [], []
    for l in range(depth):
        ada = _ada(c_all, w_ada[l], b_ada[l].reshape(1, -1))
        pw = _prep_weights(w_in[l], b_if[l], w_proj_m[l], w_proj_d[l], w_out[l], w_up[l], conv_w[l],
                           conv_b[l], w_down[l])
        gains = (g_pre_mix[l], g_post_mix[l], g_mlstm[l], g_diff[l], g_pre_ffn[l], g_post_ffn[l])
        lam_p = jnp.stack([lambda_q1[l], lambda_k1[l], lambda_q2[l], lambda_k2[l]], axis=0)
        op, os_ = _layer(l, pw, gains, lam_p, ada[:B], ada[B + pad:], xp, xs, cache_k, cache_v, page_table,
                         state_C, state_n, state_m, state_conv, tm_p=512, l_chunk=128,
                         tq=min(1024, xp.shape[1] // 2))
        xp, xs = op[0], os_[0]
        outs_p.append(op[1:])
        outs_s.append(os_[1:])
    stack = lambda outs, i: jnp.stack([o[i] for o in outs])
    return ((xp, xs) + tuple(stack(outs_p, i) for i in range(6)) + tuple(stack(outs_s, i) for i in range(6)))
```

```python
import functools
import math

import jax
import jax.numpy as jnp
from jax import lax
from jax.experimental import pallas as pl
from jax.experimental.pallas import tpu as pltpu

F32 = jnp.float32
BF16 = jnp.bfloat16

D_MODEL = 1024
H_M = 4
DH_M = 128
W_M = H_M * DH_M
H_D = 4
DQK_D = 64
DV_D = 2 * DQK_D
W_D = H_D * DV_D
D_FF = 2816
CONV_W = 3
EPS = 1e-6
LANES = 128
SUBLANES = 8
NEG_BIG = -1e30
LOG2E = 1.4426950408889634
DECODE_SLOTS = 4
ONES_ROWS = 16
MERGE_ROW_GROUPS = 4
FFN_CHUNKS = 1

C_MQ, C_MK, C_MV, C_MO = 0, 512, 1024, 1536
C_DQ, C_DK, C_DV = 2048, 2560, 3072
C_GM, C_GD = 3584, 4608
D_IN_ROWS = 5632
GATE_ROW0 = 2048

VMEM_LIMIT = 56 * 1024 * 1024


def _cparams(sem):
    return pltpu.CompilerParams(dimension_semantics=sem, vmem_limit_bytes=VMEM_LIMIT)


def _const_spec(shape):
    nd = len(shape)
    return pl.BlockSpec(shape, lambda *_: (0,) * nd, pipeline_mode=pl.Buffered(1))


def _rms(x, g):
    ms = jnp.mean(x * x, axis=-1, keepdims=True)
    return x * lax.rsqrt(ms + EPS) * g


def _nt(a, b):
    return lax.dot_general(a, b, (((1,), (1,)), ((), ())), preferred_element_type=F32)


def _tn(a, b):
    return lax.dot_general(a, b, (((0,), (0,)), ((), ())), preferred_element_type=F32)


def _dot(a, b):
    return jnp.dot(a, b, preferred_element_type=F32)


def _idiv(x, n):
    assert n & (n - 1) == 0
    return lax.shift_right_logical(x, jnp.int32(n.bit_length() - 1))


def _imod(x, n):
    assert n & (n - 1) == 0
    return lax.bitwise_and(x, jnp.int32(n - 1))


def _split3(x):
    hi = x.astype(BF16)
    r1 = x - hi.astype(F32)
    mid = r1.astype(BF16)
    r2 = r1 - mid.astype(F32)
    return hi, mid, r2.astype(BF16)


def _select_rows(sel_b, parts):
    return _dot(sel_b, parts[0]) + _dot(sel_b, parts[1]) + _dot(sel_b, parts[2])


def _ada_kernel(c_ref, w_ref, b_ref, o_ref):
    o_ref[...] = _dot(c_ref[...].astype(BF16), w_ref[...].astype(BF16)) + b_ref[...]


def _ada(c_all, w_ada, b_ada):
    n = c_all.shape[0]
    tn = 1024
    return pl.pallas_call(
        _ada_kernel,
        grid=(6 * D_MODEL // tn,),
        in_specs=[
            pl.BlockSpec((n, D_MODEL), lambda j: (0, 0)),
            pl.BlockSpec((D_MODEL, tn), lambda j: (0, j)),
            pl.BlockSpec((1, tn), lambda j: (0, j)),
        ],
        out_specs=pl.BlockSpec((n, tn), lambda j: (0, j)),
        out_shape=jax.ShapeDtypeStruct((n, 6 * D_MODEL), F32),
        compiler_params=_cparams(("arbitrary",)),
        name="adaln",
    )(c_all, w_ada, b_ada)


def _inproj_kernel(x_ref, sh_ref, sc_ref, g_ref, w_ref, wg_ref, bif_ref,
                   mq_ref, mk_ref, mv_ref, so_ref, gt_ref, dq_ref, dkf_ref, dkb_ref, dvf_ref, dvb_ref,
                   sgm_ref, sgd_ref, *, v_transposed):
    h = _rms(x_ref[...], g_ref[...]) * (1.0 + sc_ref[...]) + sh_ref[...]
    hb = h.astype(BF16)

    def seg(c0, c1):
        return _nt(hb, w_ref[c0:c1, :])

    mq_ref[...] = seg(C_MQ, C_MK).astype(BF16)
    mk_ref[...] = (seg(C_MK, C_MV) * (DH_M ** -0.5)).astype(BF16)
    mv_ref[...] = seg(C_MV, C_MO).astype(BF16)
    so_ref[...] = jax.nn.sigmoid(seg(C_MO, C_DQ)).astype(BF16)
    gt_ref[...] = _nt(hb, wg_ref[...]) + bif_ref[...]
    dq_ref[...] = (seg(C_DQ, C_DK) * (DQK_D ** -0.5 * LOG2E)).astype(BF16)
    dk = seg(C_DK, C_DV)
    dkb_ref[...] = dk.astype(BF16)
    dv = seg(C_DV, C_GM)
    tm = dk.shape[0]
    for hh in range(H_D):
        dkf_ref[pl.ds(hh, tm, stride=H_D), :] = dk[:, hh * DV_D:(hh + 1) * DV_D]
        dvf_ref[pl.ds(hh, tm, stride=H_D), :] = dv[:, hh * DV_D:(hh + 1) * DV_D]
    if v_transposed:
        dvt = dv.T.astype(BF16)
        for hh in range(H_D):
            dvb_ref[0, hh, 0:DV_D, :] = dvt[hh * DV_D:(hh + 1) * DV_D, :]
            dvb_ref[0, hh, DV_D:DV_D + ONES_ROWS, :] = jnp.ones((ONES_ROWS, tm), BF16)
    else:
        dvb_ref[...] = dv.astype(BF16)
    sgm_ref[...] = jax.nn.sigmoid(seg(C_GM, C_GD)).astype(BF16)
    sgd_ref[...] = jax.nn.sigmoid(seg(C_GD, D_IN_ROWS)).astype(BF16)


def _mod_spec(tm, rows_per_mod):
    if rows_per_mod is None:
        return pl.BlockSpec((tm, D_MODEL), lambda i: (i, 0))
    tiles = rows_per_mod // tm
    return pl.BlockSpec((None, 1, D_MODEL), lambda i: (i // tiles, 0, 0))


def _inproj(x, shift, scale, g, w, wg, bif, *, tm, rows_per_mod, v_transposed):
    n = x.shape[0]
    row = lambda width: pl.BlockSpec((tm, width), lambda i: (i, 0))
    sds = lambda width, dt: jax.ShapeDtypeStruct((n, width), dt)
    head_rows = pl.BlockSpec((tm * H_D, DV_D), lambda i: (i, 0))
    head_rows_sds = jax.ShapeDtypeStruct((n * H_D, DV_D), F32)
    if v_transposed:
        nb = n // rows_per_mod
        tiles = rows_per_mod // tm
        dvb_spec = pl.BlockSpec((1, H_D, DV_D + ONES_ROWS, tm), lambda i: (i // tiles, 0, 0, i % tiles))
        dvb_sds = jax.ShapeDtypeStruct((nb, H_D, DV_D + ONES_ROWS, rows_per_mod), BF16)
    else:
        dvb_spec, dvb_sds = row(W_D), sds(W_D, BF16)
    return pl.pallas_call(
        functools.partial(_inproj_kernel, v_transposed=v_transposed),
        grid=(n // tm,),
        in_specs=[row(D_MODEL), _mod_spec(tm, rows_per_mod), _mod_spec(tm, rows_per_mod),
                  _const_spec((1, D_MODEL)), _const_spec((D_IN_ROWS, D_MODEL)),
                  _const_spec((2 * LANES, D_MODEL)), _const_spec((1, 2 * LANES))],
        out_specs=[row(W_M), row(W_M), row(W_M), row(W_M), row(2 * LANES), row(W_D), head_rows, row(W_D),
                   head_rows, dvb_spec, row(D_MODEL), row(D_MODEL)],
        out_shape=[sds(W_M, BF16), sds(W_M, BF16), sds(W_M, BF16), sds(W_M, BF16), sds(2 * LANES, F32),
                   sds(W_D, BF16), head_rows_sds, sds(W_D, BF16), head_rows_sds, dvb_sds,
                   sds(D_MODEL, BF16), sds(D_MODEL, BF16)],
        compiler_params=_cparams(("arbitrary",)),
        name="inproj",
    )(x, shift, scale, g, w, wg, bif)


def _mlstm_kernel(q_ref, k_ref, v_ref, so_ref, gt_ref, c0_ref, n0_ref, m0_ref, g_ref,
                  hm_ref, c_ref, n_ref, m_ref, cst, nscr, mscr, *, bb, L):
    j = pl.program_id(1)

    @pl.when(j == 0)
    def _init():
        for b in range(bb):
            for h in range(H_M):
                idx = b * H_M + h
                cst[idx] = c0_ref[b, h]
                nscr[idx] = jnp.broadcast_to(n0_ref[b, h:h + 1, :], (SUBLANES, LANES))
                mscr[idx] = jnp.broadcast_to(m0_ref[b, h:h + 1, :], (SUBLANES, LANES))

    rowi = lax.broadcasted_iota(jnp.int32, (L, L), 0)
    coli = lax.broadcasted_iota(jnp.int32, (L, L), 1)
    tril = rowi >= coli
    tril_b = jnp.where(tril, 1.0, 0.0).astype(BF16)
    ones3_b = jnp.ones((L, 3 * LANES), BF16)
    ones_b = jnp.ones((L, LANES), BF16)
    ones_sq_b = jnp.ones((DH_M, DH_M), BF16)
    lane = lax.broadcasted_iota(jnp.int32, (L, LANES), 1)

    chains = [(b, h) for b in range(bb) for h in range(H_M)]
    hsl = lambda h: slice(h * DH_M, (h + 1) * DH_M)
    rep = lambda x, h: jnp.broadcast_to(x[:, h:h + 1], (L, LANES))

    gts = [gt_ref[b] for b in range(bb)]
    logfs = [-(jnp.maximum(-g[:, LANES:], 0.0) + jnp.log(1.0 + jnp.exp(-jnp.abs(g[:, LANES:])))) for g in gts]
    i_rep = [rep(gts[b][:, 0:LANES], h) for b, h in chains]
    lf3 = [_split3(x) for x in logfs]
    f_all = [_dot(tril_b, x[0]) + _dot(tril_b, x[1]) + _dot(tril_b, x[2]) for x in lf3]
    F = [rep(f_all[b], h) for b, h in chains]
    a = [i - f for i, f in zip(i_rep, F)]
    a3 = [_split3(jnp.where(lane == 0, x, 0.0)) for x in a]
    A = [_nt(ones3_b, jnp.concatenate(x, axis=1)) for x in a3]
    cm = [jnp.broadcast_to(jnp.max(jnp.where(tril, x, -jnp.inf), axis=1, keepdims=True), (L, LANES))
          for x in A]
    m_prev = [mscr[b * H_M + h][0:1, :] for b, h in chains]
    m_tok = [f + jnp.maximum(mp, c) for f, mp, c in zip(F, m_prev, cm)]
    dmat = [jnp.exp(jnp.where(tril, (f - m)[:, 0:L] + x, NEG_BIG)) for f, m, x in zip(F, m_tok, A)]
    q = [q_ref[b, :, hsl(h)] for b, h in chains]
    k = [k_ref[b, :, hsl(h)] for b, h in chains]
    v = [v_ref[b, :, hsl(h)] for b, h in chains]
    s = [(_nt(qq, kk) * d).astype(BF16) for qq, kk, d in zip(q, k, dmat)]
    nd = [_dot(ss, jnp.concatenate([vv, ones_b], axis=1)) for ss, vv in zip(s, v)]
    cn = [jnp.concatenate([cst[i].astype(BF16),
                           jnp.broadcast_to(nscr[i][0:1, :], (DH_M, DH_M)).astype(BF16)], axis=0)
          for i in range(len(chains))]
    qc = [_nt(qq, x) for qq, x in zip(q, cn)]
    inter = [jnp.exp(f + mp - m) for f, mp, m in zip(F, m_prev, m_tok)]
    num = [x[:, 0:DH_M] + w * y[:, 0:DH_M] for x, w, y in zip(nd, inter, qc)]
    den = [x[:, DH_M:] + w * y[:, DH_M:] for x, w, y in zip(nd, inter, qc)]
    hh = [x / jnp.maximum(jnp.abs(y), jnp.exp(-m)) for x, y, m in zip(num, den, m_tok)]
    ms = [_dot((x * x).astype(BF16), ones_sq_b) * (1.0 / DH_M) for x in hh]
    for (b, h), x, y in zip(chains, hh, ms):
        hn = x * lax.rsqrt(y + EPS) * g_ref[:, hsl(h)]
        hm_ref[b, :, hsl(h)] = (so_ref[b, :, hsl(h)].astype(F32) * hn).astype(BF16)
    m_end = [m[L - 1:L, :] for m in m_tok]
    f_end = [f[L - 1:L, :] for f in F]
    w_end = [jnp.exp(fe + x - me) for fe, x, me in zip(f_end, a, m_end)]
    decay = [jnp.exp(fe + mp - me) for fe, mp, me in zip(f_end, m_prev, m_end)]
    kw = [kk.astype(F32) * w for kk, w in zip(k, w_end)]
    upd = [_tn(vv, x.astype(BF16)) for vv, x in zip(v, kw)]
    for i in range(len(chains)):
        cst[i] = decay[i] * cst[i] + upd[i]
        n_new = decay[i] * nscr[i][0:1, :] + jnp.sum(kw[i], axis=0, keepdims=True)
        nscr[i] = jnp.broadcast_to(n_new, (SUBLANES, LANES))
        mscr[i] = jnp.broadcast_to(m_end[i], (SUBLANES, LANES))

    @pl.when(j == pl.num_programs(1) - 1)
    def _fin():
        for i, (b, h) in enumerate(chains):
            c_ref[b, h] = cst[i]
            n_ref[b, h:h + 1, :] = nscr[i][0:1, :]
            m_ref[b, h:h + 1, :] = mscr[i][0:1, :]


def _mlstm(q, k, v, so, gt, c0, n0, m0, g, *, bb, L):
    B, T, _ = q.shape
    tok = lambda width: pl.BlockSpec((bb, L, width), lambda i, j: (i, j, 0))
    st3 = pl.BlockSpec((bb, H_M, LANES), lambda i, j: (i, 0, 0))
    st4 = pl.BlockSpec((bb, H_M, DH_M, DH_M), lambda i, j: (i, 0, 0, 0))
    return pl.pallas_call(
        functools.partial(_mlstm_kernel, bb=bb, L=L),
        grid=(B // bb, T // L),
        in_specs=[tok(W_M), tok(W_M), tok(W_M), tok(W_M), tok(2 * LANES), st4, st3, st3,
                  pl.BlockSpec((1, W_M), lambda i, j: (0, 0))],
        out_specs=[tok(W_M), st4, st3, st3],
        out_shape=[jax.ShapeDtypeStruct((B, T, W_M), BF16),
                   jax.ShapeDtypeStruct((B, H_M, DH_M, DH_M), F32),
                   jax.ShapeDtypeStruct((B, H_M, LANES), F32),
                   jax.ShapeDtypeStruct((B, H_M, LANES), F32)],
        scratch_shapes=[pltpu.VMEM((bb * H_M, DH_M, DH_M), F32),
                        pltpu.VMEM((bb * H_M, SUBLANES, LANES), F32),
                        pltpu.VMEM((bb * H_M, SUBLANES, LANES), F32)],
        compiler_params=_cparams(("arbitrary", "arbitrary")),
        name="mlstm",
    )(q, k, v, so, gt, c0, n0, m0, g)


def _lambda(lam_ref, lam_init):
    p = lam_ref[...]
    l1 = jnp.sum(p[0:1, :] * p[1:2, :], axis=-1, keepdims=True)
    l2 = jnp.sum(p[2:3, :] * p[3:4, :], axis=-1, keepdims=True)
    return jnp.exp(l1) - jnp.exp(l2) + lam_init


def _attn_kernel(q_ref, k_ref, vt_ref, g_ref, lam_ref, o_ref, qm, acc, mrow, sa_ref, sb_ref, ca_ref,
                 cb_ref, bias_ref, *, tq, tk, lam_init):
    g = pl.program_id(2)
    lane = lax.broadcasted_iota(jnp.int32, (tq, 2 * DQK_D), 1)
    for w in range(2):
        q = q_ref[0, w * tq:(w + 1) * tq, :]
        zero = jnp.zeros_like(q)
        qm[w, 0:tq, :] = jnp.where(lane < DQK_D, q, zero)
        qm[w, tq:2 * tq, :] = jnp.where(lane >= DQK_D, q, zero)
    acc[...] = jnp.zeros_like(acc)
    mrow[...] = jnp.full_like(mrow, NEG_BIG)
    buf_a, buf_b = (sa_ref, ca_ref), (sb_ref, cb_ref)

    @pl.when(g == 0)
    def _causal_bias():
        r = lax.broadcasted_iota(jnp.int32, (tk, 2 * tq), 0)
        c = lax.broadcasted_iota(jnp.int32, (tk, 2 * tq), 1)
        bias_ref[...] = jnp.where(r <= jnp.where(c >= tq, c - tq, c), 0.0, NEG_BIG)

    def scores(kb, w, buf):
        s_buf, c_buf = buf
        k0 = pl.multiple_of(kb * tk, tk)
        s = _nt(k_ref[0, pl.ds(k0, tk), :], qm[w])
        s_buf[...] = s
        c_buf[...] = jnp.max(s, axis=0, keepdims=True)

    def softmax_pv(kb, w, buf, masked):
        s_buf, c_buf = buf
        k0 = pl.multiple_of(kb * tk, tk)
        s = s_buf[...]
        if masked:
            s = s + bias_ref[...]
            c = jnp.max(s, axis=0, keepdims=True)
        else:
            c = c_buf[...]
        m_old = mrow[w]
        m_new = jnp.maximum(m_old, c)
        alpha = jnp.exp2(m_old - m_new)
        p = jnp.exp2(s - m_new)
        acc[w] = acc[w] * alpha + _dot(vt_ref[0, 0, :, pl.ds(k0, tk)], p.astype(BF16))
        mrow[w] = m_new

    def finalize(w):
        l = acc[w, DV_D:DV_D + 1, :]
        a = acc[w, 0:DV_D, :]
        lam = _lambda(lam_ref, lam_init)
        o = a[:, 0:tq] / l[:, 0:tq] - lam * (a[:, tq:2 * tq] / l[:, tq:2 * tq])
        ms = jnp.mean(o * o, axis=0, keepdims=True)
        on = o * lax.rsqrt(ms + EPS) * (g_ref[0] * (1.0 - lam_init))
        o_ref[0, w * tq:(w + 1) * tq, :] = on.T.astype(BF16)

    def pairs(w, first, second):
        def body(j, carry):
            kb = 2 * j
            scores(kb + 1, w, second)
            softmax_pv(kb, w, first, False)
            scores(kb + 2, w, first)
            softmax_pv(kb + 1, w, second, False)
            return carry

        lax.fori_loop(0, g, body, 0)

    scores(0, 0, buf_a)
    pairs(0, buf_a, buf_b)
    scores(0, 1, buf_b)
    softmax_pv(2 * g, 0, buf_a, True)
    finalize(0)
    pairs(1, buf_b, buf_a)
    scores(2 * g + 1, 1, buf_a)
    softmax_pv(2 * g, 1, buf_b, False)
    softmax_pv(2 * g + 1, 1, buf_a, True)
    finalize(1)


def _attn_prompt(q, k, vt, g3, lam_p, *, tq, lam_init):
    B, T, _ = q.shape
    tk = tq
    vec = lambda: pltpu.VMEM((2, 1, 2 * tq), F32)
    buf = lambda: pltpu.VMEM((tk, 2 * tq), F32)
    cmax = lambda: pltpu.VMEM((1, 2 * tq), F32)
    return pl.pallas_call(
        functools.partial(_attn_kernel, tq=tq, tk=tk, lam_init=lam_init),
        grid=(B, H_D, T // (2 * tq)),
        in_specs=[pl.BlockSpec((1, 2 * tq, DV_D), lambda b, h, i: (b, i, h)),
                  pl.BlockSpec((1, T, DV_D), lambda b, h, i: (b, 0, h)),
                  pl.BlockSpec((1, 1, DV_D + ONES_ROWS, T), lambda b, h, i: (b, h, 0, 0)),
                  pl.BlockSpec((1, DV_D, 1), lambda b, h, i: (h, 0, 0)),
                  pl.BlockSpec((4, DQK_D), lambda b, h, i: (0, 0))],
        out_specs=pl.BlockSpec((1, 2 * tq, DV_D), lambda b, h, i: (b, i, h)),
        out_shape=jax.ShapeDtypeStruct((B, T, W_D), BF16),
        scratch_shapes=[pltpu.VMEM((2, 2 * tq, DV_D), BF16), pltpu.VMEM((2, DV_D + ONES_ROWS, 2 * tq), F32),
                        vec(), buf(), buf(), cmax(), cmax(), buf()],
        compiler_params=_cparams(("arbitrary", "arbitrary", "arbitrary")),
        name="attn_prompt",
    )(q, k, vt, g3, lam_p)


def _decode_kernel(pt_ref, q_ref, kn_ref, vn_ref, g_ref, lam_ref, ck_hbm, cv_hbm, o_ref, kbuf, vbuf, sems, *,
                   n_pages, page, t_new, lam_init):
    nr = 2 * H_D * t_new
    pr = page * H_D
    nn = t_new * H_D
    b = pl.program_id(0)
    n_seq = pl.num_programs(0)
    ahead = DECODE_SLOTS - 1

    def page_copies(seq, slot):
        cps = []
        for j in range(n_pages):
            row0 = pl.multiple_of(pt_ref[seq, j] * pr, pr)
            cps.append(pltpu.make_async_copy(ck_hbm.at[pl.ds(row0, pr), :], kbuf.at[slot, j], sems.at[0, slot]))
            cps.append(pltpu.make_async_copy(cv_hbm.at[pl.ds(row0, pr), :], vbuf.at[slot, j], sems.at[1, slot]))
        return cps

    @pl.when(b == 0)
    def _prime():
        for s in range(ahead):
            for cp in page_copies(s, s):
                cp.start()

    @pl.when(b + ahead < n_seq)
    def _prefetch():
        for cp in page_copies(b + ahead, lax.rem(b + ahead, DECODE_SLOTS)):
            cp.start()

    slot = lax.rem(b, DECODE_SLOTS)
    for cp in page_copies(b, slot):
        cp.wait()
    k_refs = [kbuf.at[slot, j] for j in range(n_pages)]
    v_refs = [vbuf.at[slot, j] for j in range(n_pages)]
    q = q_ref[0]
    rq = lax.broadcasted_iota(jnp.int32, (nr, DV_D), 0)
    cq = lax.broadcasted_iota(jnp.int32, (nr, DV_D), 1)
    qb = jnp.where(_idiv(rq, H_D * t_new) == _idiv(cq, DQK_D), q, jnp.zeros_like(q))

    r = lax.broadcasted_iota(jnp.int32, (nr, pr), 0)
    c = lax.broadcasted_iota(jnp.int32, (nr, pr), 1)
    head_ok = _imod(_idiv(r, t_new), H_D) == _imod(c, H_D)
    s_old = jnp.concatenate(
        [jnp.where(head_ok, _nt(qb, k_refs[j][...].astype(BF16)), NEG_BIG) for j in range(n_pages)], axis=1)
    rn = lax.broadcasted_iota(jnp.int32, (nr, nn), 0)
    cn = lax.broadcasted_iota(jnp.int32, (nr, nn), 1)
    s_new = _nt(qb, kn_ref[0].astype(BF16))
    s_new = jnp.where(_imod(_idiv(rn, t_new), H_D) == _imod(cn, H_D), s_new, NEG_BIG)
    s_new = jnp.where(_idiv(cn, H_D) <= _imod(rn, t_new), s_new, NEG_BIG)
    m = jnp.maximum(jnp.max(s_old, axis=1, keepdims=True), jnp.max(s_new, axis=1, keepdims=True))
    p_old = jnp.exp2(s_old - m)
    p_new = jnp.exp2(s_new - m)
    l = jnp.sum(p_old, axis=1, keepdims=True) + jnp.sum(p_new, axis=1, keepdims=True)
    pb = p_old.astype(BF16)
    out = _dot(p_new.astype(BF16), vn_ref[0].astype(BF16))
    for j in range(n_pages):
        out = out + _dot(pb[:, j * pr:(j + 1) * pr], v_refs[j][...].astype(BF16))
    o_r = out / l
    half = H_D * t_new
    lam = _lambda(lam_ref, lam_init)
    o = o_r[0:half, :] - lam * o_r[half:2 * half, :]
    o_ref[0] = (_rms(o, g_ref[...]) * (1.0 - lam_init)).astype(BF16)


def _attn_decode(page_table, q_rep, k_new, v_new, cache_k2, cache_v2, g_rows, lam_p, *, page, t_new, lam_init):
    n_seq, n_pages = page_table.shape
    nr = 2 * H_D * t_new
    half = H_D * t_new

    assert n_seq >= DECODE_SLOTS - 1
    page_set = lambda: pltpu.VMEM((DECODE_SLOTS, n_pages, page * H_D, DV_D), F32)
    grid_spec = pltpu.PrefetchScalarGridSpec(
        num_scalar_prefetch=1,
        grid=(n_seq,),
        in_specs=[pl.BlockSpec((1, nr, DV_D), lambda b, pt: (b, 0, 0)),
                  pl.BlockSpec((1, half, DV_D), lambda b, pt: (b, 0, 0)),
                  pl.BlockSpec((1, half, DV_D), lambda b, pt: (b, 0, 0)),
                  pl.BlockSpec((half, DV_D), lambda b, pt: (0, 0)),
                  pl.BlockSpec((4, DQK_D), lambda b, pt: (0, 0)),
                  pl.BlockSpec(memory_space=pl.ANY), pl.BlockSpec(memory_space=pl.ANY)],
        out_specs=pl.BlockSpec((1, half, DV_D), lambda b, pt: (b, 0, 0)),
        scratch_shapes=[page_set(), page_set(), pltpu.SemaphoreType.DMA((2, DECODE_SLOTS))],
    )
    return pl.pallas_call(
        functools.partial(_decode_kernel, n_pages=n_pages, page=page, t_new=t_new, lam_init=lam_init),
        grid_spec=grid_spec,
        out_shape=jax.ShapeDtypeStruct((n_seq, half, DV_D), BF16),
        compiler_params=_cparams(("arbitrary",)),
        name="attn_decode",
    )(page_table, q_rep, k_new, v_new, g_rows, lam_p, cache_k2, cache_v2)


def _merge_kernel(x_ref, hm_ref, hd_ref, sgm_ref, sgd_ref, gate1_ref, sh2_ref, sc2_ref, gpost_ref, gpre_ref,
                  wpm_ref, wpd_ref, wout_ref, x1_ref, h2_ref):
    tm = x_ref.shape[0]
    rows = [slice(i * tm // MERGE_ROW_GROUPS, (i + 1) * tm // MERGE_ROW_GROUPS) for i in range(MERGE_ROW_GROUPS)]
    mod = lambda ref, r: ref[...] if ref.shape[0] == 1 else ref[r, :]
    pm = [_dot(hm_ref[r, :], wpm_ref[...]) for r in rows]
    pd = [_dot(hd_ref[r, :], wpd_ref[...]) for r in rows]
    merged = [(sgm_ref[r, :].astype(F32) * a + sgd_ref[r, :].astype(F32) * b).astype(BF16)
              for r, a, b in zip(rows, pm, pd)]
    y = [_dot(m, wout_ref[...]) for m in merged]
    for r, yy in zip(rows, y):
        x1 = x_ref[r, :] + mod(gate1_ref, r) * _rms(yy, gpost_ref[...])
        x1_ref[r, :] = x1
        h2_ref[r, :] = (_rms(x1, gpre_ref[...]) * (1.0 + mod(sc2_ref, r)) + mod(sh2_ref, r)).astype(BF16)


def _merge(x, hm, hd, sgm, sgd, gate1, sh2, sc2, gpost, gpre, wpm, wpd, wout, *, tm, rows_per_mod):
    n = x.shape[0]
    row = lambda width: pl.BlockSpec((tm, width), lambda i: (i, 0))
    mod = _mod_spec(tm, rows_per_mod)
    return pl.pallas_call(
        _merge_kernel,
        grid=(n // tm,),
        in_specs=[row(D_MODEL), row(W_M), row(W_D), row(D_MODEL), row(D_MODEL), mod, mod, mod,
                  _const_spec((1, D_MODEL)), _const_spec((1, D_MODEL)),
                  _const_spec((W_M, D_MODEL)), _const_spec((W_D, D_MODEL)), _const_spec((D_MODEL, D_MODEL))],
        out_specs=[row(D_MODEL), row(D_MODEL)],
        out_shape=[jax.ShapeDtypeStruct((n, D_MODEL), F32), jax.ShapeDtypeStruct((n, D_MODEL), BF16)],
        compiler_params=_cparams(("arbitrary",)),
        name="merge",
    )(x, hm, hd, sgm, sgd, gate1, sh2, sc2, gpost, gpre, wpm, wpd, wout)


def _gelu_tanh(x):
    return 0.5 * x * (1.0 + jnp.tanh(math.sqrt(2.0 / math.pi) * (x + 0.044715 * (x * x * x))))


def _ffn_kernel(*refs, tm, tiles_per_seq, t_seq):
    if tiles_per_seq is None:
        (h2_ref, x1_ref, gate2_ref, gpost_ref, wup_ref, cw_ref, cb_ref, wdn_ref, st_ref,
         y_ref, tail_ref, a_scr, p1_scr, p2_scr) = refs
        a_scr[0:SUBLANES, :] = jnp.zeros((SUBLANES, D_FF), F32)
        n_st = st_ref.shape[0]
        r = lax.broadcasted_iota(jnp.int32, (tm, n_st), 0)
        j = lax.broadcasted_iota(jnp.int32, (tm, n_st), 1)
        same_seq = _idiv(r, t_seq) == _idiv(j, CONV_W - 1)
        t_r, s_j = _imod(r, t_seq), _imod(j, CONV_W - 1)
        sel1 = jnp.where(same_seq, jnp.where(t_r == 0, jnp.where(s_j == 1, 1.0, 0.0), 0.0), 0.0)
        sel2 = jnp.where(same_seq, jnp.where(t_r == s_j, 1.0, 0.0), 0.0)
        st3 = _split3(st_ref[...])
        p1_scr[...] = _select_rows(sel1.astype(BF16), st3)
        p2_scr[...] = _select_rows(sel2.astype(BF16), st3)
    else:
        (h2_ref, x1_ref, gate2_ref, gpost_ref, wup_ref, cw_ref, cb_ref, wdn_ref,
         y_ref, tail_ref, a_scr) = refs
        @pl.when(pl.program_id(0) % tiles_per_seq == 0)
        def _zero():
            a_scr[0:SUBLANES, :] = jnp.zeros((SUBLANES, D_FF), F32)

    h2 = h2_ref[...]
    ch = D_FF // FFN_CHUNKS
    f = None
    for c in range(FFN_CHUNKS):
        cs = slice(c * ch, (c + 1) * ch)
        a = _dot(h2, wup_ref[:, c * ch:(c + 1) * ch])
        b = _dot(h2, wup_ref[:, D_FF + c * ch:D_FF + (c + 1) * ch])
        a_scr[SUBLANES:SUBLANES + tm, cs] = a
        prev1 = a_scr[SUBLANES - 1:SUBLANES - 1 + tm, cs]
        prev2 = a_scr[SUBLANES - 2:SUBLANES - 2 + tm, cs]
        if tiles_per_seq is None:
            tpos = lax.broadcasted_iota(jnp.int32, (tm, ch), 0) % t_seq
            prev1 = jnp.where(tpos >= 1, prev1, p1_scr[:, cs])
            prev2 = jnp.where(tpos >= 2, prev2, p2_scr[:, cs])
        conv = cb_ref[:, cs] + cw_ref[0:1, cs] * prev2 + cw_ref[1:2, cs] * prev1 + cw_ref[2:3, cs] * a
        act = (_gelu_tanh(conv) * b).astype(BF16)
        fc = _dot(act, wdn_ref[cs, :])
        f = fc if f is None else f + fc
    if tiles_per_seq is None:
        sel_t = jnp.where(_idiv(r, t_seq) == _idiv(j, CONV_W - 1),
                          jnp.where(t_r == s_j + (t_seq - (CONV_W - 1)), 1.0, 0.0), 0.0).astype(BF16)
        a3 = _split3(a_scr[SUBLANES:SUBLANES + tm, :])
        tail_ref[...] = _tn(sel_t, a3[0]) + _tn(sel_t, a3[1]) + _tn(sel_t, a3[2])
    else:
        tail = a_scr[tm:tm + SUBLANES, :]
        tail_ref[0] = tail
        a_scr[0:SUBLANES, :] = tail
    y_ref[...] = x1_ref[...] + gate2_ref[...] * _rms(f, gpost_ref[...])


def _ffn(h2, x1, gate2, gpost, wup, cw, cb, wdn, st=None, *, tm, rows_per_mod, t_seq):
    n = h2.shape[0]
    row = lambda width: pl.BlockSpec((tm, width), lambda i: (i, 0))
    in_specs = [row(D_MODEL), row(D_MODEL), _mod_spec(tm, rows_per_mod), _const_spec((1, D_MODEL)),
                _const_spec((D_MODEL, 2 * D_FF)), _const_spec((SUBLANES, D_FF)), _const_spec((1, D_FF)),
                _const_spec((D_FF, D_MODEL))]
    args = [h2, x1, gate2, gpost, wup, cw, cb, wdn]
    scratch = [pltpu.VMEM((tm + SUBLANES, D_FF), F32)]
    if rows_per_mod is None:
        assert n == tm and t_seq >= CONV_W - 1
        tiles_per_seq = None
        st_rows = (n // t_seq) * (CONV_W - 1)
        st_spec = pl.BlockSpec((st_rows, D_FF), lambda i: (0, 0))
        in_specs += [st_spec]
        args += [st]
        out_specs = [row(D_MODEL), st_spec]
        out_shape = [jax.ShapeDtypeStruct((n, D_MODEL), F32), jax.ShapeDtypeStruct((st_rows, D_FF), F32)]
        scratch += [pltpu.VMEM((tm, D_FF), F32), pltpu.VMEM((tm, D_FF), F32)]
    else:
        tiles_per_seq = rows_per_mod // tm
        out_specs = [row(D_MODEL), pl.BlockSpec((1, SUBLANES, D_FF), lambda i: (i // tiles_per_seq, 0, 0))]
        out_shape = [jax.ShapeDtypeStruct((n, D_MODEL), F32),
                     jax.ShapeDtypeStruct((n // rows_per_mod, SUBLANES, D_FF), F32)]
    return pl.pallas_call(
        functools.partial(_ffn_kernel, tm=tm, tiles_per_seq=tiles_per_seq, t_seq=t_seq),
        grid=(n // tm,),
        in_specs=in_specs,
        out_specs=out_specs,
        out_shape=out_shape,
        scratch_shapes=scratch,
        compiler_params=_cparams(("arbitrary",)),
        name="ffn",
    )(*args)


def _win_prep_kernel(w_ref, o_ref):
    o_ref[0:GATE_ROW0, :] = w_ref[0:GATE_ROW0, :].astype(BF16)
    o_ref[GATE_ROW0:, :] = w_ref[GATE_ROW0 + 2 * H_M:, :].astype(BF16)


def _win_prep(w_t):
    d_in = w_t.shape[0]
    tc = 256
    return pl.pallas_call(
        _win_prep_kernel,
        grid=(D_MODEL // tc,),
        in_specs=[pl.BlockSpec((d_in, tc), lambda i: (0, i))],
        out_specs=pl.BlockSpec((D_IN_ROWS, tc), lambda i: (0, i)),
        out_shape=jax.ShapeDtypeStruct((D_IN_ROWS, D_MODEL), BF16),
        compiler_params=_cparams(("arbitrary",)),
        name="win_prep",
    )(w_t)


def _prep_weights(w_in, b_if, w_proj_m, w_proj_d, w_out, w_up, conv_w, conv_b, w_down):
    assert w_in.shape[1] == D_IN_ROWS + 2 * H_M
    w_t = w_in.T
    w_rows = _win_prep(w_t)
    gpad = jnp.zeros((LANES - H_M, D_MODEL), F32)
    gates = w_t[GATE_ROW0:GATE_ROW0 + 2 * H_M]
    wg = jnp.concatenate([gates[:H_M], gpad, gates[H_M:], gpad], axis=0).astype(BF16)
    bpad = jnp.zeros((LANES - H_M,), F32)
    bif = jnp.concatenate([b_if[:H_M], bpad, b_if[H_M:], bpad]).reshape(1, 2 * LANES)
    cw = jnp.concatenate([conv_w, jnp.zeros((SUBLANES - CONV_W, D_FF), F32)], axis=0)
    return dict(w_in=w_rows, wg=wg, bif=bif, wpm=w_proj_m.astype(BF16), wpd=w_proj_d.astype(BF16),
                wout=w_out.astype(BF16), wup=w_up.astype(BF16), cw=cw, cb=conv_b.reshape(1, D_FF),
                wdn=w_down.astype(BF16))


def _layer(l, pw, gains, lam_p, ada_p, ada_s, x_prompt, x_sample, cache_k, cache_v, page_table,
           state_C, state_n, state_m, state_conv, *, tm_p, l_chunk, tq):
    B, T, _ = x_prompt.shape
    S, Ts, _ = x_sample.shape
    lam_init = 0.8 - 0.6 * math.exp(-0.3 * l)
    g_pre_mix, g_post_mix, g_mlstm, g_diff, g_pre_ffn, g_post_ffn = gains
    row = lambda g: g.reshape(1, -1)

    xp = x_prompt.reshape(B * T, D_MODEL)
    modp = [a.reshape(B, 1, D_MODEL) for a in jnp.split(ada_p, 6, axis=-1)]
    (mq, mk, mv, so, gt, dq, dkf, dkb, dvf, dvt, sgm, sgd) = _inproj(
        xp, modp[0], modp[1], row(g_pre_mix), pw["w_in"], pw["wg"], pw["bif"], tm=tm_p, rows_per_mod=T, v_transposed=True)
    tok = lambda a: a.reshape(B, T, -1)
    zc = jnp.zeros((B, H_M, DH_M, DH_M), F32)
    zn = jnp.zeros((B, H_M, LANES), F32)
    hm, Cp, n_p, m_p = _mlstm(tok(mq), tok(mk), tok(mv), tok(so), tok(gt), zc, zn, zn, row(g_mlstm),
                              bb=B, L=l_chunk)
    hd = _attn_prompt(tok(dq), tok(dkb), dvt, g_diff.reshape(H_D, DV_D, 1), lam_p, tq=tq, lam_init=lam_init)
    x1, h2 = _merge(xp, hm.reshape(B * T, W_M), hd.reshape(B * T, W_D), sgm, sgd, modp[2], modp[3], modp[4],
                    row(g_post_mix), row(g_pre_ffn), pw["wpm"], pw["wpd"], pw["wout"], tm=tm_p, rows_per_mod=T)
    yp, tail = _ffn(h2, x1, modp[5], row(g_post_ffn), pw["wup"], pw["cw"], pw["cb"], pw["wdn"],
                    tm=tm_p // 2, rows_per_mod=T, t_seq=T)
    out_p = (yp.reshape(B, T, D_MODEL), dkf.reshape(B, T, H_D, 2 * DQK_D), dvf.reshape(B, T, H_D, DV_D),
             Cp, n_p, m_p[:, :, 0], tail[:, SUBLANES - (CONV_W - 1):, :])

    ns = S * Ts
    xs = x_sample.reshape(ns, D_MODEL)
    mods = [jnp.repeat(a, Ts, axis=0) for a in jnp.split(ada_s, 6, axis=-1)]
    (mq, mk, mv, so, gt, dq, dkf, dkb, dvf, dvb, sgm, sgd) = _inproj(
        xs, mods[0], mods[1], row(g_pre_mix), pw["w_in"], pw["wg"], pw["bif"], tm=ns, rows_per_mod=None, v_transposed=False)
    Lp = 16
    padt = lambda a: jnp.pad(a.reshape(S, Ts, -1), ((0, 0), (0, Lp - Ts), (0, 0)))
    gt3 = gt.reshape(S, Ts, 2 * LANES)
    gpad = jnp.concatenate([jnp.full((S, Lp - Ts, LANES), NEG_BIG, F32),
                            jnp.full((S, Lp - Ts, LANES), -NEG_BIG, F32)], axis=-1)
    gt_p = jnp.concatenate([gt3, gpad], axis=1)
    m0 = jnp.broadcast_to(state_m[l][:, :, None], (S, H_M, LANES))
    hm, Cs, n_s, m_s = _mlstm(padt(mq), padt(mk), padt(mv), padt(so), gt_p, state_C[l], state_n[l], m0,
                              row(g_mlstm), bb=8, L=Lp)
    hm = hm[:, :Ts, :].reshape(ns, W_M)
    n_pool, page = cache_k.shape[1], cache_k.shape[2]
    ck2 = cache_k[l].reshape(n_pool * page * H_D, DV_D)
    cv2 = cache_v[l].reshape(n_pool * page * H_D, DV_D)
    q_rep = dq.reshape(S, Ts, H_D, DV_D).transpose(0, 2, 1, 3)
    q_rep = jnp.broadcast_to(q_rep[:, None], (S, 2, H_D, Ts, DV_D)).reshape(S, 2 * H_D * Ts, DV_D)
    new_rows = lambda a: a.reshape(S, Ts * H_D, DV_D)
    g_rows = jnp.repeat(g_diff.reshape(H_D, DV_D), Ts, axis=0)
    hd = _attn_decode(page_table, q_rep, new_rows(dkf), new_rows(dvf), ck2, cv2, g_rows, lam_p,
                      page=page, t_new=Ts, lam_init=lam_init)
    hd = hd.reshape(S, H_D, Ts, DV_D).transpose(0, 2, 1, 3).reshape(ns, W_D)
    x1, h2 = _merge(xs, hm, hd, sgm, sgd, mods[2], mods[3], mods[4], row(g_post_mix), row(g_pre_ffn),
                    pw["wpm"], pw["wpd"], pw["wout"], tm=ns, rows_per_mod=None)
    st = state_conv[l].reshape(S * (CONV_W - 1), D_FF)
    ys, conv_s = _ffn(h2, x1, mods[5], row(g_post_ffn), pw["wup"], pw["cw"], pw["cb"], pw["wdn"], st,
                      tm=ns, rows_per_mod=None, t_seq=Ts)
    out_s = (ys.reshape(S, Ts, D_MODEL), dkf.reshape(S, Ts, H_D, 2 * DQK_D), dvf.reshape(S, Ts, H_D, DV_D),
             Cs, n_s, m_s[:, :, 0], conv_s.reshape(S, CONV_W - 1, D_FF))
    return out_p, out_s


def kernel(x_prompt, x_sample, c_prompt, c_sample, cache_k, cache_v, page_table, state_C, state_n, state_m,
           state_conv, w_ada, b_ada, g_pre_mix, g_post_mix, w_in, b_if, g_mlstm, lambda_q1, lambda_k1,
           lambda_q2, lambda_k2, g_diff, w_proj_m, w_proj_d, w_out, g_pre_ffn, g_post_ffn, w_up, conv_w,
           conv_b, w_down):
    depth = w_in.shape[0]
    B = x_prompt.shape[0]
    S = x_sample.shape[0]
    pad = (-B) % SUBLANES
    c_all = jnp.concatenate([c_prompt, jnp.zeros((pad, D_MODEL), F32), c_sample], axis=0)
    xp, xs = x_prompt, x_sample
    outs_p, outs_s = [], []
    for l in range(depth):
        ada = _ada(c_all, w_ada[l], b_ada[l].reshape(1, -1))
        pw = _prep_weights(w_in[l], b_if[l], w_proj_m[l], w_proj_d[l], w_out[l], w_up[l], conv_w[l],
                           conv_b[l], w_down[l])
        gains = (g_pre_mix[l], g_post_mix[l], g_mlstm[l], g_diff[l], g_pre_ffn[l], g_post_ffn[l])
        lam_p = jnp.stack([lambda_q1[l], lambda_k1[l], lambda_q2[l], lambda_k2[l]], axis=0)
        op, os_ = _layer(l, pw, gains, lam_p, ada[:B], ada[B + pad:], xp, xs, cache_k, cache_v, page_table,
                         state_C, state_n, state_m, state_conv, tm_p=512, l_chunk=128,
                         tq=min(1024, xp.shape[1] // 2))
        xp, xs = op[0], os_[0]
        outs_p.append(op[1:])
        outs_s.append(os_[1:])
    stack = lambda outs, i: jnp.stack([o[i] for o in outs])
    return ((xp, xs) + tuple(stack(outs_p, i) for i in range(6)) + tuple(stack(outs_s, i) for i in range(6)))
```

```python
import functools
import math

import jax
import jax.numpy as jnp
from jax import lax
from jax.experimental import pallas as pl
from jax.experimental.pallas import tpu as pltpu

F32 = jnp.float32
BF16 = jnp.bfloat16

D_MODEL = 1024
H_M = 4
DH_M = 128
W_M = H_M * DH_M
H_D = 4
DQK_D = 64
DV_D = 2 * DQK_D
W_D = H_D * DV_D
D_FF = 2816
CONV_W = 3
EPS = 1e-6
LANES = 128
SUBLANES = 8
NEG_BIG = -1e30
LOG2E = 1.4426950408889634
DECODE_SLOTS = 3
ONES_ROWS = 16
MERGE_ROW_GROUPS = 4
FFN_CHUNKS = 1

C_MQ, C_MK, C_MV, C_MO = 0, 512, 1024, 1536
C_DQ, C_DK, C_DV = 2048, 2560, 3072
C_GM, C_GD = 3584, 4608
D_IN_ROWS = 5632
GATE_ROW0 = 2048

VMEM_LIMIT = 56 * 1024 * 1024


def _cparams(sem):
    return pltpu.CompilerParams(dimension_semantics=sem, vmem_limit_bytes=VMEM_LIMIT)


def _const_spec(shape):
    nd = len(shape)
    return pl.BlockSpec(shape, lambda *_: (0,) * nd, pipeline_mode=pl.Buffered(1))


def _rms(x, g):
    ms = jnp.mean(x * x, axis=-1, keepdims=True)
    return x * lax.rsqrt(ms + EPS) * g


def _nt(a, b):
    return lax.dot_general(a, b, (((1,), (1,)), ((), ())), preferred_element_type=F32)


def _tn(a, b):
    return lax.dot_general(a, b, (((0,), (0,)), ((), ())), preferred_element_type=F32)


def _dot(a, b):
    return jnp.dot(a, b, preferred_element_type=F32)


def _idiv(x, n):
    assert n & (n - 1) == 0
    return lax.shift_right_logical(x, jnp.int32(n.bit_length() - 1))


def _imod(x, n):
    assert n & (n - 1) == 0
    return lax.bitwise_and(x, jnp.int32(n - 1))


def _split3(x):
    hi = x.astype(BF16)
    r1 = x - hi.astype(F32)
    mid = r1.astype(BF16)
    r2 = r1 - mid.astype(F32)
    return hi, mid, r2.astype(BF16)


def _select_rows(sel_b, parts):
    return _dot(sel_b, parts[0]) + _dot(sel_b, parts[1]) + _dot(sel_b, parts[2])


def _ada_kernel(c_ref, w_ref, b_ref, o_ref):
    o_ref[...] = _dot(c_ref[...].astype(BF16), w_ref[...].astype(BF16)) + b_ref[...]


def _ada(c_all, w_ada, b_ada):
    n = c_all.shape[0]
    tn = 1024
    return pl.pallas_call(
        _ada_kernel,
        grid=(6 * D_MODEL // tn,),
        in_specs=[
            pl.BlockSpec((n, D_MODEL), lambda j: (0, 0)),
            pl.BlockSpec((D_MODEL, tn), lambda j: (0, j)),
            pl.BlockSpec((1, tn), lambda j: (0, j)),
        ],
        out_specs=pl.BlockSpec((n, tn), lambda j: (0, j)),
        out_shape=jax.ShapeDtypeStruct((n, 6 * D_MODEL), F32),
        compiler_params=_cparams(("arbitrary",)),
        name="adaln",
    )(c_all, w_ada, b_ada)


def _inproj_kernel(x_ref, sh_ref, sc_ref, g_ref, w_ref, wg_ref, bif_ref,
                   mq_ref, mk_ref, mv_ref, so_ref, gt_ref, dq_ref, dkf_ref, dkb_ref, dvf_ref, dvb_ref,
                   sgm_ref, sgd_ref, *, v_transposed):
    h = _rms(x_ref[...], g_ref[...]) * (1.0 + sc_ref[...]) + sh_ref[...]
    hb = h.astype(BF16)

    def seg(c0, c1):
        return _nt(hb, w_ref[c0:c1, :])

    mq_ref[...] = seg(C_MQ, C_MK).astype(BF16)
    mk_ref[...] = (seg(C_MK, C_MV) * (DH_M ** -0.5)).astype(BF16)
    mv_ref[...] = seg(C_MV, C_MO).astype(BF16)
    so_ref[...] = jax.nn.sigmoid(seg(C_MO, C_DQ)).astype(BF16)
    gt_ref[...] = _nt(hb, wg_ref[...]) + bif_ref[...]
    dq_ref[...] = (seg(C_DQ, C_DK) * (DQK_D ** -0.5 * LOG2E)).astype(BF16)
    dk = seg(C_DK, C_DV)
    dkb_ref[...] = dk.astype(BF16)
    dv = seg(C_DV, C_GM)
    tm = dk.shape[0]
    for hh in range(H_D):
        dkf_ref[pl.ds(hh, tm, stride=H_D), :] = dk[:, hh * DV_D:(hh + 1) * DV_D]
        dvf_ref[pl.ds(hh, tm, stride=H_D), :] = dv[:, hh * DV_D:(hh + 1) * DV_D]
    if v_transposed:
        dvt = dv.T.astype(BF16)
        for hh in range(H_D):
            dvb_ref[0, hh, 0:DV_D, :] = dvt[hh * DV_D:(hh + 1) * DV_D, :]
            dvb_ref[0, hh, DV_D:DV_D + ONES_ROWS, :] = jnp.ones((ONES_ROWS, tm), BF16)
    else:
        dvb_ref[...] = dv.astype(BF16)
    sgm_ref[...] = jax.nn.sigmoid(seg(C_GM, C_GD)).astype(BF16)
    sgd_ref[...] = jax.nn.sigmoid(seg(C_GD, D_IN_ROWS)).astype(BF16)


def _mod_spec(tm, rows_per_mod):
    if rows_per_mod is None:
        return pl.BlockSpec((tm, D_MODEL), lambda i: (i, 0))
    tiles = rows_per_mod // tm
    return pl.BlockSpec((None, 1, D_MODEL), lambda i: (i // tiles, 0, 0))


def _inproj(x, shift, scale, g, w, wg, bif, *, tm, rows_per_mod, v_transposed):
    n = x.shape[0]
    row = lambda width: pl.BlockSpec((tm, width), lambda i: (i, 0))
    sds = lambda width, dt: jax.ShapeDtypeStruct((n, width), dt)
    head_rows = pl.BlockSpec((tm * H_D, DV_D), lambda i: (i, 0))
    head_rows_sds = jax.ShapeDtypeStruct((n * H_D, DV_D), F32)
    if v_transposed:
        nb = n // rows_per_mod
        tiles = rows_per_mod // tm
        dvb_spec = pl.BlockSpec((1, H_D, DV_D + ONES_ROWS, tm), lambda i: (i // tiles, 0, 0, i % tiles))
        dvb_sds = jax.ShapeDtypeStruct((nb, H_D, DV_D + ONES_ROWS, rows_per_mod), BF16)
    else:
        dvb_spec, dvb_sds = row(W_D), sds(W_D, BF16)
    return pl.pallas_call(
        functools.partial(_inproj_kernel, v_transposed=v_transposed),
        grid=(n // tm,),
        in_specs=[row(D_MODEL), _mod_spec(tm, rows_per_mod), _mod_spec(tm, rows_per_mod),
                  _const_spec((1, D_MODEL)), _const_spec((D_IN_ROWS, D_MODEL)),
                  _const_spec((2 * LANES, D_MODEL)), _const_spec((1, 2 * LANES))],
        out_specs=[row(W_M), row(W_M), row(W_M), row(W_M), row(2 * LANES), row(W_D), head_rows, row(W_D),
                   head_rows, dvb_spec, row(D_MODEL), row(D_MODEL)],
        out_shape=[sds(W_M, BF16), sds(W_M, BF16), sds(W_M, BF16), sds(W_M, BF16), sds(2 * LANES, F32),
                   sds(W_D, BF16), head_rows_sds, sds(W_D, BF16), head_rows_sds, dvb_sds,
                   sds(D_MODEL, BF16), sds(D_MODEL, BF16)],
        compiler_params=_cparams(("arbitrary",)),
        name="inproj",
    )(x, shift, scale, g, w, wg, bif)


def _mlstm_kernel(q_ref, k_ref, v_ref, so_ref, gt_ref, c0_ref, n0_ref, m0_ref, g_ref,
                  hm_ref, c_ref, n_ref, m_ref, cst, nscr, mscr, *, bb, L):
    j = pl.program_id(1)

    @pl.when(j == 0)
    def _init():
        for b in range(bb):
            for h in range(H_M):
                idx = b * H_M + h
                cst[idx] = c0_ref[b, h]
                nscr[idx] = jnp.broadcast_to(n0_ref[b, h:h + 1, :], (SUBLANES, LANES))
                mscr[idx] = jnp.broadcast_to(m0_ref[b, h:h + 1, :], (SUBLANES, LANES))

    rowi = lax.broadcasted_iota(jnp.int32, (L, L), 0)
    coli = lax.broadcasted_iota(jnp.int32, (L, L), 1)
    tril = rowi >= coli
    tril_b = jnp.where(tril, 1.0, 0.0).astype(BF16)
    ones3_b = jnp.ones((L, 3 * LANES), BF16)
    ones_b = jnp.ones((L, LANES), BF16)
    ones_sq_b = jnp.ones((DH_M, DH_M), BF16)
    lane = lax.broadcasted_iota(jnp.int32, (L, LANES), 1)

    chains = [(b, h) for b in range(bb) for h in range(H_M)]
    hsl = lambda h: slice(h * DH_M, (h + 1) * DH_M)
    rep = lambda x, h: jnp.broadcast_to(x[:, h:h + 1], (L, LANES))

    gts = [gt_ref[b] for b in range(bb)]
    logfs = [-(jnp.maximum(-g[:, LANES:], 0.0) + jnp.log(1.0 + jnp.exp(-jnp.abs(g[:, LANES:])))) for g in gts]
    i_rep = [rep(gts[b][:, 0:LANES], h) for b, h in chains]
    lf3 = [_split3(x) for x in logfs]
    f_all = [_dot(tril_b, x[0]) + _dot(tril_b, x[1]) + _dot(tril_b, x[2]) for x in lf3]
    F = [rep(f_all[b], h) for b, h in chains]
    a = [i - f for i, f in zip(i_rep, F)]
    a3 = [_split3(jnp.where(lane == 0, x, 0.0)) for x in a]
    A = [_nt(ones3_b, jnp.concatenate(x, axis=1)) for x in a3]
    cm = [jnp.broadcast_to(jnp.max(jnp.where(tril, x, -jnp.inf), axis=1, keepdims=True), (L, LANES))
          for x in A]
    m_prev = [mscr[b * H_M + h][0:1, :] for b, h in chains]
    m_tok = [f + jnp.maximum(mp, c) for f, mp, c in zip(F, m_prev, cm)]
    dmat = [jnp.exp(jnp.where(tril, (f - m)[:, 0:L] + x, NEG_BIG)) for f, m, x in zip(F, m_tok, A)]
    q = [q_ref[b, :, hsl(h)] for b, h in chains]
    k = [k_ref[b, :, hsl(h)] for b, h in chains]
    v = [v_ref[b, :, hsl(h)] for b, h in chains]
    s = [(_nt(qq, kk) * d).astype(BF16) for qq, kk, d in zip(q, k, dmat)]
    nd = [_dot(ss, jnp.concatenate([vv, ones_b], axis=1)) for ss, vv in zip(s, v)]
    cn = [jnp.concatenate([cst[i].astype(BF16),
                           jnp.broadcast_to(nscr[i][0:1, :], (DH_M, DH_M)).astype(BF16)], axis=0)
          for i in range(len(chains))]
    qc = [_nt(qq, x) for qq, x in zip(q, cn)]
    inter = [jnp.exp(f + mp - m) for f, mp, m in zip(F, m_prev, m_tok)]
    num = [x[:, 0:DH_M] + w * y[:, 0:DH_M] for x, w, y in zip(nd, inter, qc)]
    den = [x[:, DH_M:] + w * y[:, DH_M:] for x, w, y in zip(nd, inter, qc)]
    hh = [x / jnp.maximum(jnp.abs(y), jnp.exp(-m)) for x, y, m in zip(num, den, m_tok)]
    ms = [_dot((x * x).astype(BF16), ones_sq_b) * (1.0 / DH_M) for x in hh]
    for (b, h), x, y in zip(chains, hh, ms):
        hn = x * lax.rsqrt(y + EPS) * g_ref[:, hsl(h)]
        hm_ref[b, :, hsl(h)] = (so_ref[b, :, hsl(h)].astype(F32) * hn).astype(BF16)
    m_end = [m[L - 1:L, :] for m in m_tok]
    f_end = [f[L - 1:L, :] for f in F]
    w_end = [jnp.exp(fe + x - me) for fe, x, me in zip(f_end, a, m_end)]
    decay = [jnp.exp(fe + mp - me) for fe, mp, me in zip(f_end, m_prev, m_end)]
    kw = [kk.astype(F32) * w for kk, w in zip(k, w_end)]
    upd = [_tn(vv, x.astype(BF16)) for vv, x in zip(v, kw)]
    for i in range(len(chains)):
        cst[i] = decay[i] * cst[i] + upd[i]
        n_new = decay[i] * nscr[i][0:1, :] + jnp.sum(kw[i], axis=0, keepdims=True)
        nscr[i] = jnp.broadcast_to(n_new, (SUBLANES, LANES))
        mscr[i] = jnp.broadcast_to(m_end[i], (SUBLANES, LANES))

    @pl.when(j == pl.num_programs(1) - 1)
    def _fin():
        for i, (b, h) in enumerate(chains):
            c_ref[b, h] = cst[i]
            n_ref[b, h:h + 1, :] = nscr[i][0:1, :]
            m_ref[b, h:h + 1, :] = mscr[i][0:1, :]


def _mlstm(q, k, v, so, gt, c0, n0, m0, g, *, bb, L):
    B, T, _ = q.shape
    tok = lambda width: pl.BlockSpec((bb, L, width), lambda i, j: (i, j, 0))
    st3 = pl.BlockSpec((bb, H_M, LANES), lambda i, j: (i, 0, 0))
    st4 = pl.BlockSpec((bb, H_M, DH_M, DH_M), lambda i, j: (i, 0, 0, 0))
    return pl.pallas_call(
        functools.partial(_mlstm_kernel, bb=bb, L=L),
        grid=(B // bb, T // L),
        in_specs=[tok(W_M), tok(W_M), tok(W_M), tok(W_M), tok(2 * LANES), st4, st3, st3,
                  pl.BlockSpec((1, W_M), lambda i, j: (0, 0))],
        out_specs=[tok(W_M), st4, st3, st3],
        out_shape=[jax.ShapeDtypeStruct((B, T, W_M), BF16),
                   jax.ShapeDtypeStruct((B, H_M, DH_M, DH_M), F32),
                   jax.ShapeDtypeStruct((B, H_M, LANES), F32),
                   jax.ShapeDtypeStruct((B, H_M, LANES), F32)],
        scratch_shapes=[pltpu.VMEM((bb * H_M, DH_M, DH_M), F32),
                        pltpu.VMEM((bb * H_M, SUBLANES, LANES), F32),
                        pltpu.VMEM((bb * H_M, SUBLANES, LANES), F32)],
        compiler_params=_cparams(("arbitrary", "arbitrary")),
        name="mlstm",
    )(q, k, v, so, gt, c0, n0, m0, g)


def _lambda(lam_ref, lam_init):
    p = lam_ref[...]
    l1 = jnp.sum(p[0:1, :] * p[1:2, :], axis=-1, keepdims=True)
    l2 = jnp.sum(p[2:3, :] * p[3:4, :], axis=-1, keepdims=True)
    return jnp.exp(l1) - jnp.exp(l2) + lam_init


def _attn_kernel(q_ref, k_ref, vt_ref, g_ref, lam_ref, o_ref, qm, acc, mrow, sa_ref, sb_ref, ca_ref,
                 cb_ref, bias_ref, *, tq, tk, lam_init):
    g = pl.program_id(2)
    lane = lax.broadcasted_iota(jnp.int32, (tq, 2 * DQK_D), 1)
    for w in range(2):
        q = q_ref[0, w * tq:(w + 1) * tq, :]
        zero = jnp.zeros_like(q)
        qm[w, 0:tq, :] = jnp.where(lane < DQK_D, q, zero)
        qm[w, tq:2 * tq, :] = jnp.where(lane >= DQK_D, q, zero)
    acc[...] = jnp.zeros_like(acc)
    mrow[...] = jnp.full_like(mrow, NEG_BIG)
    buf_a, buf_b = (sa_ref, ca_ref), (sb_ref, cb_ref)

    @pl.when(g == 0)
    def _causal_bias():
        r = lax.broadcasted_iota(jnp.int32, (tk, 2 * tq), 0)
        c = lax.broadcasted_iota(jnp.int32, (tk, 2 * tq), 1)
        bias_ref[...] = jnp.where(r <= jnp.where(c >= tq, c - tq, c), 0.0, NEG_BIG)

    def scores(kb, w, buf, want_max=True):
        s_buf, c_buf = buf
        k0 = pl.multiple_of(kb * tk, tk)
        s = _nt(k_ref[0, pl.ds(k0, tk), :], qm[w])
        s_buf[...] = s
        if want_max:
            c_buf[...] = jnp.max(s, axis=0, keepdims=True)

    def softmax_pv(kb, w, buf, masked):
        s_buf, c_buf = buf
        k0 = pl.multiple_of(kb * tk, tk)
        s = s_buf[...]
        if masked:
            s = s + bias_ref[...]
            c = jnp.max(s, axis=0, keepdims=True)
        else:
            c = c_buf[...]
        m_old = mrow[w]
        m_new = jnp.maximum(m_old, c)
        alpha = jnp.exp2(m_old - m_new)
        p = jnp.exp2(s - m_new)
        acc[w] = acc[w] * alpha + _dot(vt_ref[0, 0, :, pl.ds(k0, tk)], p.astype(BF16))
        mrow[w] = m_new

    def finalize(w):
        l = acc[w, DV_D:DV_D + 1, :]
        a = acc[w, 0:DV_D, :]
        lam = _lambda(lam_ref, lam_init)
        o = a[:, 0:tq] / l[:, 0:tq] - lam * (a[:, tq:2 * tq] / l[:, tq:2 * tq])
        ms = jnp.mean(o * o, axis=0, keepdims=True)
        on = o * lax.rsqrt(ms + EPS) * (g_ref[0] * (1.0 - lam_init))
        o_ref[0, w * tq:(w + 1) * tq, :] = on.T.astype(BF16)

    def pairs(w, first, second):
        def body(j, carry):
            kb = 2 * j
            scores(kb + 1, w, second)
            softmax_pv(kb, w, first, False)
            scores(kb + 2, w, first)
            softmax_pv(kb + 1, w, second, False)
            return carry

        lax.fori_loop(0, g, body, 0)

    scores(0, 0, buf_a)
    pairs(0, buf_a, buf_b)
    scores(0, 1, buf_b)
    softmax_pv(2 * g, 0, buf_a, True)
    finalize(0)
    pairs(1, buf_b, buf_a)
    scores(2 * g + 1, 1, buf_a, want_max=False)
    softmax_pv(2 * g, 1, buf_b, False)
    softmax_pv(2 * g + 1, 1, buf_a, True)
    finalize(1)


def _attn_prompt(q, k, vt, g3, lam_p, *, tq, lam_init):
    B, T, _ = q.shape
    tk = tq
    vec = lambda: pltpu.VMEM((2, 1, 2 * tq), F32)
    buf = lambda: pltpu.VMEM((tk, 2 * tq), F32)
    cmax = lambda: pltpu.VMEM((1, 2 * tq), F32)
    return pl.pallas_call(
        functools.partial(_attn_kernel, tq=tq, tk=tk, lam_init=lam_init),
        grid=(B, H_D, T // (2 * tq)),
        in_specs=[pl.BlockSpec((1, 2 * tq, DV_D), lambda b, h, i: (b, i, h)),
                  pl.BlockSpec((1, T, DV_D), lambda b, h, i: (b, 0, h)),
                  pl.BlockSpec((1, 1, DV_D + ONES_ROWS, T), lambda b, h, i: (b, h, 0, 0)),
                  pl.BlockSpec((1, DV_D, 1), lambda b, h, i: (h, 0, 0)),
                  pl.BlockSpec((4, DQK_D), lambda b, h, i: (0, 0))],
        out_specs=pl.BlockSpec((1, 2 * tq, DV_D), lambda b, h, i: (b, i, h)),
        out_shape=jax.ShapeDtypeStruct((B, T, W_D), BF16),
        scratch_shapes=[pltpu.VMEM((2, 2 * tq, DV_D), BF16), pltpu.VMEM((2, DV_D + ONES_ROWS, 2 * tq), F32),
                        vec(), buf(), buf(), cmax(), cmax(), buf()],
        compiler_params=_cparams(("arbitrary", "arbitrary", "arbitrary")),
        name="attn_prompt",
    )(q, k, vt, g3, lam_p)


def _decode_kernel(pt_ref, q_ref, kn_ref, vn_ref, g_ref, lam_ref, ck_hbm, cv_hbm, o_ref, kbuf, vbuf, sems, *,
                   n_pages, page, t_new, lam_init):
    nr = 2 * H_D * t_new
    pr = page * H_D
    nn = t_new * H_D
    b = pl.program_id(0)
    n_seq = pl.num_programs(0)
    ahead = DECODE_SLOTS - 1

    def page_copies(seq, slot):
        cps = []
        for j in range(n_pages):
            row0 = pl.multiple_of(pt_ref[seq, j] * pr, pr)
            cps.append(pltpu.make_async_copy(ck_hbm.at[pl.ds(row0, pr), :], kbuf.at[slot, j], sems.at[0, slot]))
            cps.append(pltpu.make_async_copy(cv_hbm.at[pl.ds(row0, pr), :], vbuf.at[slot, j], sems.at[1, slot]))
        return cps

    @pl.when(b == 0)
    def _prime():
        for s in range(ahead):
            for cp in page_copies(s, s):
                cp.start()

    @pl.when(b + ahead < n_seq)
    def _prefetch():
        for cp in page_copies(b + ahead, lax.rem(b + ahead, DECODE_SLOTS)):
            cp.start()

    slot = lax.rem(b, DECODE_SLOTS)
    for cp in page_copies(b, slot):
        cp.wait()
    k_refs = [kbuf.at[slot, j] for j in range(n_pages)]
    v_refs = [vbuf.at[slot, j] for j in range(n_pages)]
    q = q_ref[0]
    rq = lax.broadcasted_iota(jnp.int32, (nr, DV_D), 0)
    cq = lax.broadcasted_iota(jnp.int32, (nr, DV_D), 1)
    qb = jnp.where(_idiv(rq, H_D * t_new) == _idiv(cq, DQK_D), q, jnp.zeros_like(q))

    r = lax.broadcasted_iota(jnp.int32, (nr, pr), 0)
    c = lax.broadcasted_iota(jnp.int32, (nr, pr), 1)
    head_ok = _imod(_idiv(r, t_new), H_D) == _imod(c, H_D)
    s_old = jnp.concatenate(
        [jnp.where(head_ok, _nt(qb, k_refs[j][...].astype(BF16)), NEG_BIG) for j in range(n_pages)], axis=1)
    rn = lax.broadcasted_iota(jnp.int32, (nr, nn), 0)
    cn = lax.broadcasted_iota(jnp.int32, (nr, nn), 1)
    s_new = _nt(qb, kn_ref[0].astype(BF16))
    s_new = jnp.where(_imod(_idiv(rn, t_new), H_D) == _imod(cn, H_D), s_new, NEG_BIG)
    s_new = jnp.where(_idiv(cn, H_D) <= _imod(rn, t_new), s_new, NEG_BIG)
    m = jnp.maximum(jnp.max(s_old, axis=1, keepdims=True), jnp.max(s_new, axis=1, keepdims=True))
    p_old = jnp.exp2(s_old - m)
    p_new = jnp.exp2(s_new - m)
    l = jnp.sum(p_old, axis=1, keepdims=True) + jnp.sum(p_new, axis=1, keepdims=True)
    pb = p_old.astype(BF16)
    out = _dot(p_new.astype(BF16), vn_ref[0].astype(BF16))
    for j in range(n_pages):
        out = out + _dot(pb[:, j * pr:(j + 1) * pr], v_refs[j][...].astype(BF16))
    o_r = out / l
    half = H_D * t_new
    lam = _lambda(lam_ref, lam_init)
    o = o_r[0:half, :] - lam * o_r[half:2 * half, :]
    o_ref[0] = (_rms(o, g_ref[...]) * (1.0 - lam_init)).astype(BF16)


def _attn_decode(page_table, q_rep, k_new, v_new, cache_k2, cache_v2, g_rows, lam_p, *, page, t_new, lam_init):
    n_seq, n_pages = page_table.shape
    nr = 2 * H_D * t_new
    half = H_D * t_new

    assert n_seq >= DECODE_SLOTS - 1
    page_set = lambda: pltpu.VMEM((DECODE_SLOTS, n_pages, page * H_D, DV_D), F32)
    grid_spec = pltpu.PrefetchScalarGridSpec(
        num_scalar_prefetch=1,
        grid=(n_seq,),
        in_specs=[pl.BlockSpec((1, nr, DV_D), lambda b, pt: (b, 0, 0)),
                  pl.BlockSpec((1, half, DV_D), lambda b, pt: (b, 0, 0)),
                  pl.BlockSpec((1, half, DV_D), lambda b, pt: (b, 0, 0)),
                  pl.BlockSpec((half, DV_D), lambda b, pt: (0, 0)),
                  pl.BlockSpec((4, DQK_D), lambda b, pt: (0, 0)),
                  pl.BlockSpec(memory_space=pl.ANY), pl.BlockSpec(memory_space=pl.ANY)],
        out_specs=pl.BlockSpec((1, half, DV_D), lambda b, pt: (b, 0, 0)),
        scratch_shapes=[page_set(), page_set(), pltpu.SemaphoreType.DMA((2, DECODE_SLOTS))],
    )
    return pl.pallas_call(
        functools.partial(_decode_kernel, n_pages=n_pages, page=page, t_new=t_new, lam_init=lam_init),
        grid_spec=grid_spec,
        out_shape=jax.ShapeDtypeStruct((n_seq, half, DV_D), BF16),
        compiler_params=_cparams(("arbitrary",)),
        name="attn_decode",
    )(page_table, q_rep, k_new, v_new, g_rows, lam_p, cache_k2, cache_v2)


def _merge_kernel(x_ref, hm_ref, hd_ref, sgm_ref, sgd_ref, gate1_ref, sh2_ref, sc2_ref, gpost_ref, gpre_ref,
                  wpm_ref, wpd_ref, wout_ref, x1_ref, h2_ref):
    tm = x_ref.shape[0]
    rows = [slice(i * tm // MERGE_ROW_GROUPS, (i + 1) * tm // MERGE_ROW_GROUPS) for i in range(MERGE_ROW_GROUPS)]
    mod = lambda ref, r: ref[...] if ref.shape[0] == 1 else ref[r, :]
    pm = [_dot(hm_ref[r, :], wpm_ref[...]) for r in rows]
    pd = [_dot(hd_ref[r, :], wpd_ref[...]) for r in rows]
    merged = [(sgm_ref[r, :].astype(F32) * a + sgd_ref[r, :].astype(F32) * b).astype(BF16)
              for r, a, b in zip(rows, pm, pd)]
    y = [_dot(m, wout_ref[...]) for m in merged]
    for r, yy in zip(rows, y):
        x1 = x_ref[r, :] + mod(gate1_ref, r) * _rms(yy, gpost_ref[...])
        x1_ref[r, :] = x1
        h2_ref[r, :] = (_rms(x1, gpre_ref[...]) * (1.0 + mod(sc2_ref, r)) + mod(sh2_ref, r)).astype(BF16)


def _merge(x, hm, hd, sgm, sgd, gate1, sh2, sc2, gpost, gpre, wpm, wpd, wout, *, tm, rows_per_mod):
    n = x.shape[0]
    row = lambda width: pl.BlockSpec((tm, width), lambda i: (i, 0))
    mod = _mod_spec(tm, rows_per_mod)
    return pl.pallas_call(
        _merge_kernel,
        grid=(n // tm,),
        in_specs=[row(D_MODEL), row(W_M), row(W_D), row(D_MODEL), row(D_MODEL), mod, mod, mod,
                  _const_spec((1, D_MODEL)), _const_spec((1, D_MODEL)),
                  _const_spec((W_M, D_MODEL)), _const_spec((W_D, D_MODEL)), _const_spec((D_MODEL, D_MODEL))],
        out_specs=[row(D_MODEL), row(D_MODEL)],
        out_shape=[jax.ShapeDtypeStruct((n, D_MODEL), F32), jax.ShapeDtypeStruct((n, D_MODEL), BF16)],
        compiler_params=_cparams(("arbitrary",)),
        name="merge",
    )(x, hm, hd, sgm, sgd, gate1, sh2, sc2, gpost, gpre, wpm, wpd, wout)


def _gelu_tanh(x):
    return 0.5 * x * (1.0 + jnp.tanh(math.sqrt(2.0 / math.pi) * (x + 0.044715 * (x * x * x))))


def _ffn_kernel(*refs, tm, tiles_per_seq, t_seq):
    if tiles_per_seq is None:
        (h2_ref, x1_ref, gate2_ref, gpost_ref, wup_ref, cw_ref, cb_ref, wdn_ref, st_ref,
         y_ref, tail_ref, a_scr, p1_scr, p2_scr) = refs
        a_scr[0:SUBLANES, :] = jnp.zeros((SUBLANES, D_FF), F32)
        n_st = st_ref.shape[0]
        r = lax.broadcasted_iota(jnp.int32, (tm, n_st), 0)
        j = lax.broadcasted_iota(jnp.int32, (tm, n_st), 1)
        same_seq = _idiv(r, t_seq) == _idiv(j, CONV_W - 1)
        t_r, s_j = _imod(r, t_seq), _imod(j, CONV_W - 1)
        sel1 = jnp.where(same_seq, jnp.where(t_r == 0, jnp.where(s_j == 1, 1.0, 0.0), 0.0), 0.0)
        sel2 = jnp.where(same_seq, jnp.where(t_r == s_j, 1.0, 0.0), 0.0)
        st3 = _split3(st_ref[...])
        p1_scr[...] = _select_rows(sel1.astype(BF16), st3)
        p2_scr[...] = _select_rows(sel2.astype(BF16), st3)
    else:
        (h2_ref, x1_ref, gate2_ref, gpost_ref, wup_ref, cw_ref, cb_ref, wdn_ref,
         y_ref, tail_ref, a_scr) = refs
        @pl.when(pl.program_id(0) % tiles_per_seq == 0)
        def _zero():
            a_scr[0:SUBLANES, :] = jnp.zeros((SUBLANES, D_FF), F32)

    h2 = h2_ref[...]
    ch = D_FF // FFN_CHUNKS
    f = None
    for c in range(FFN_CHUNKS):
        cs = slice(c * ch, (c + 1) * ch)
        a = _dot(h2, wup_ref[:, c * ch:(c + 1) * ch])
        b = _dot(h2, wup_ref[:, D_FF + c * ch:D_FF + (c + 1) * ch])
        a_scr[SUBLANES:SUBLANES + tm, cs] = a
        prev1 = a_scr[SUBLANES - 1:SUBLANES - 1 + tm, cs]
        prev2 = a_scr[SUBLANES - 2:SUBLANES - 2 + tm, cs]
        if tiles_per_seq is None:
            tpos = lax.broadcasted_iota(jnp.int32, (tm, ch), 0) % t_seq
            prev1 = jnp.where(tpos >= 1, prev1, p1_scr[:, cs])
            prev2 = jnp.where(tpos >= 2, prev2, p2_scr[:, cs])
        conv = cb_ref[:, cs] + cw_ref[0:1, cs] * prev2 + cw_ref[1:2, cs] * prev1 + cw_ref[2:3, cs] * a
        act = (_gelu_tanh(conv) * b).astype(BF16)
        fc = _dot(act, wdn_ref[cs, :])
        f = fc if f is None else f + fc
    if tiles_per_seq is None:
        sel_t = jnp.where(_idiv(r, t_seq) == _idiv(j, CONV_W - 1),
                          jnp.where(t_r == s_j + (t_seq - (CONV_W - 1)), 1.0, 0.0), 0.0).astype(BF16)
        a3 = _split3(a_scr[SUBLANES:SUBLANES + tm, :])
        tail_ref[...] = _tn(sel_t, a3[0]) + _tn(sel_t, a3[1]) + _tn(sel_t, a3[2])
    else:
        tail = a_scr[tm:tm + SUBLANES, :]
        tail_ref[0] = tail
        a_scr[0:SUBLANES, :] = tail
    y_ref[...] = x1_ref[...] + gate2_ref[...] * _rms(f, gpost_ref[...])


def _ffn(h2, x1, gate2, gpost, wup, cw, cb, wdn, st=None, *, tm, rows_per_mod, t_seq):
    n = h2.shape[0]
    row = lambda width: pl.BlockSpec((tm, width), lambda i: (i, 0))
    in_specs = [row(D_MODEL), row(D_MODEL), _mod_spec(tm, rows_per_mod), _const_spec((1, D_MODEL)),
                _const_spec((D_MODEL, 2 * D_FF)), _const_spec((SUBLANES, D_FF)), _const_spec((1, D_FF)),
                _const_spec((D_FF, D_MODEL))]
    args = [h2, x1, gate2, gpost, wup, cw, cb, wdn]
    scratch = [pltpu.VMEM((tm + SUBLANES, D_FF), F32)]
    if rows_per_mod is None:
        assert n == tm and t_seq >= CONV_W - 1
        tiles_per_seq = None
        st_rows = (n // t_seq) * (CONV_W - 1)
        st_spec = pl.BlockSpec((st_rows, D_FF), lambda i: (0, 0))
        in_specs += [st_spec]
        args += [st]
        out_specs = [row(D_MODEL), st_spec]
        out_shape = [jax.ShapeDtypeStruct((n, D_MODEL), F32), jax.ShapeDtypeStruct((st_rows, D_FF), F32)]
        scratch += [pltpu.VMEM((tm, D_FF), F32), pltpu.VMEM((tm, D_FF), F32)]
    else:
        tiles_per_seq = rows_per_mod // tm
        out_specs = [row(D_MODEL), pl.BlockSpec((1, SUBLANES, D_FF), lambda i: (i // tiles_per_seq, 0, 0))]
        out_shape = [jax.ShapeDtypeStruct((n, D_MODEL), F32),
                     jax.ShapeDtypeStruct((n // rows_per_mod, SUBLANES, D_FF), F32)]
    return pl.pallas_call(
        functools.partial(_ffn_kernel, tm=tm, tiles_per_seq=tiles_per_seq, t_seq=t_seq),
        grid=(n // tm,),
        in_specs=in_specs,
        out_specs=out_specs,
        out_shape=out_shape,
        scratch_shapes=scratch,
        compiler_params=_cparams(("arbitrary",)),
        name="ffn",
    )(*args)


def _win_prep_kernel(w_ref, o_ref):
    o_ref[0:GATE_ROW0, :] = w_ref[0:GATE_ROW0, :].astype(BF16)
    o_ref[GATE_ROW0:, :] = w_ref[GATE_ROW0 + 2 * H_M:, :].astype(BF16)


def _win_prep(w_t):
    d_in = w_t.shape[0]
    tc = 256
    return pl.pallas_call(
        _win_prep_kernel,
        grid=(D_MODEL // tc,),
        in_specs=[pl.BlockSpec((d_in, tc), lambda i: (0, i))],
        out_specs=pl.BlockSpec((D_IN_ROWS, tc), lambda i: (0, i)),
        out_shape=jax.ShapeDtypeStruct((D_IN_ROWS, D_MODEL), BF16),
        compiler_params=_cparams(("arbitrary",)),
        name="win_prep",
    )(w_t)


def _prep_weights(w_in, b_if, w_proj_m, w_proj_d, w_out, w_up, conv_w, conv_b, w_down):
    assert w_in.shape[1] == D_IN_ROWS + 2 * H_M
    w_t = w_in.T
    w_rows = _win_prep(w_t)
    gpad = jnp.zeros((LANES - H_M, D_MODEL), F32)
    gates = w_t[GATE_ROW0:GATE_ROW0 + 2 * H_M]
    wg = jnp.concatenate([gates[:H_M], gpad, gates[H_M:], gpad], axis=0).astype(BF16)
    bpad = jnp.zeros((LANES - H_M,), F32)
    bif = jnp.concatenate([b_if[:H_M], bpad, b_if[H_M:], bpad]).reshape(1, 2 * LANES)
    cw = jnp.concatenate([conv_w, jnp.zeros((SUBLANES - CONV_W, D_FF), F32)], axis=0)
    return dict(w_in=w_rows, wg=wg, bif=bif, wpm=w_proj_m.astype(BF16), wpd=w_proj_d.astype(BF16),
                wout=w_out.astype(BF16), wup=w_up.astype(BF16), cw=cw, cb=conv_b.reshape(1, D_FF),
                wdn=w_down.astype(BF16))


def _layer(l, pw, gains, lam_p, ada_p, ada_s, x_prompt, x_sample, cache_k, cache_v, page_table,
           state_C, state_n, state_m, state_conv, *, tm_p, l_chunk, tq):
    B, T, _ = x_prompt.shape
    S, Ts, _ = x_sample.shape
    lam_init = 0.8 - 0.6 * math.exp(-0.3 * l)
    g_pre_mix, g_post_mix, g_mlstm, g_diff, g_pre_ffn, g_post_ffn = gains
    row = lambda g: g.reshape(1, -1)

    xp = x_prompt.reshape(B * T, D_MODEL)
    modp = [a.reshape(B, 1, D_MODEL) for a in jnp.split(ada_p, 6, axis=-1)]
    (mq, mk, mv, so, gt, dq, dkf, dkb, dvf, dvt, sgm, sgd) = _inproj(
        xp, modp[0], modp[1], row(g_pre_mix), pw["w_in"], pw["wg"], pw["bif"], tm=tm_p, rows_per_mod=T, v_transposed=True)
    tok = lambda a: a.reshape(B, T, -1)
    zc = jnp.zeros((B, H_M, DH_M, DH_M), F32)
    zn = jnp.zeros((B, H_M, LANES), F32)
    hm, Cp, n_p, m_p = _mlstm(tok(mq), tok(mk), tok(mv), tok(so), tok(gt), zc, zn, zn, row(g_mlstm),
                              bb=B, L=l_chunk)
    hd = _attn_prompt(tok(dq), tok(dkb), dvt, g_diff.reshape(H_D, DV_D, 1), lam_p, tq=tq, lam_init=lam_init)
    x1, h2 = _merge(xp, hm.reshape(B * T, W_M), hd.reshape(B * T, W_D), sgm, sgd, modp[2], modp[3], modp[4],
                    row(g_post_mix), row(g_pre_ffn), pw["wpm"], pw["wpd"], pw["wout"], tm=tm_p, rows_per_mod=T)
    yp, tail = _ffn(h2, x1, modp[5], row(g_post_ffn), pw["wup"], pw["cw"], pw["cb"], pw["wdn"],
                    tm=tm_p // 2, rows_per_mod=T, t_seq=T)
    out_p = (yp.reshape(B, T, D_MODEL), dkf.reshape(B, T, H_D, 2 * DQK_D), dvf.reshape(B, T, H_D, DV_D),
             Cp, n_p, m_p[:, :, 0], tail[:, SUBLANES - (CONV_W - 1):, :])

    ns = S * Ts
    xs = x_sample.reshape(ns, D_MODEL)
    mods = [jnp.repeat(a, Ts, axis=0) for a in jnp.split(ada_s, 6, axis=-1)]
    (mq, mk, mv, so, gt, dq, dkf, dkb, dvf, dvb, sgm, sgd) = _inproj(
        xs, mods[0], mods[1], row(g_pre_mix), pw["w_in"], pw["wg"], pw["bif"], tm=ns, rows_per_mod=None, v_transposed=False)
    Lp = 16
    padt = lambda a: jnp.pad(a.reshape(S, Ts, -1), ((0, 0), (0, Lp - Ts), (0, 0)))
    gt3 = gt.reshape(S, Ts, 2 * LANES)
    gpad = jnp.concatenate([jnp.full((S, Lp - Ts, LANES), NEG_BIG, F32),
                            jnp.full((S, Lp - Ts, LANES), -NEG_BIG, F32)], axis=-1)
    gt_p = jnp.concatenate([gt3, gpad], axis=1)
    m0 = jnp.broadcast_to(state_m[l][:, :, None], (S, H_M, LANES))
    hm, Cs, n_s, m_s = _mlstm(padt(mq), padt(mk), padt(mv), padt(so), gt_p, state_C[l], state_n[l], m0,
                              row(g_mlstm), bb=8, L=Lp)
    hm = hm[:, :Ts, :].reshape(ns, W_M)
    n_pool, page = cache_k.shape[1], cache_k.shape[2]
    ck2 = cache_k[l].reshape(n_pool * page * H_D, DV_D)
    cv2 = cache_v[l].reshape(n_pool * page * H_D, DV_D)
    q_rep = dq.reshape(S, Ts, H_D, DV_D).transpose(0, 2, 1, 3)
    q_rep = jnp.broadcast_to(q_rep[:, None], (S, 2, H_D, Ts, DV_D)).reshape(S, 2 * H_D * Ts, DV_D)
    new_rows = lambda a: a.reshape(S, Ts * H_D, DV_D)
    g_rows = jnp.repeat(g_diff.reshape(H_D, DV_D), Ts, axis=0)
    hd = _attn_decode(page_table, q_rep, new_rows(dkf), new_rows(dvf), ck2, cv2, g_rows, lam_p,
                      page=page, t_new=Ts, lam_init=lam_init)
    hd = hd.reshape(S, H_D, Ts, DV_D).transpose(0, 2, 1, 3).reshape(ns, W_D)
    x1, h2 = _merge(xs, hm, hd, sgm, sgd, mods[2], mods[3], mods[4], row(g_post_mix), row(g_pre_ffn),
                    pw["wpm"], pw["wpd"], pw["wout"], tm=ns, rows_per_mod=None)
    st = state_conv[l].reshape(S * (CONV_W - 1), D_FF)
    ys, conv_s = _ffn(h2, x1, mods[5], row(g_post_ffn), pw["wup"], pw["cw"], pw["cb"], pw["wdn"], st,
                      tm=ns, rows_per_mod=None, t_seq=Ts)
    out_s = (ys.reshape(S, Ts, D_MODEL), dkf.reshape(S, Ts, H_D, 2 * DQK_D), dvf.reshape(S, Ts, H_D, DV_D),
             Cs, n_s, m_s[:, :, 0], conv_s.reshape(S, CONV_W - 1, D_FF))
    return out_p, out_s


def kernel(x_prompt, x_sample, c_prompt, c_sample, cache_k, cache_v, page_table, state_C, state_n, state_m,
           state_conv, w_ada, b_ada, g_pre_mix, g_post_mix, w_in, b_if, g_mlstm, lambda_q1, lambda_k1,
           lambda_q2, lambda_k2, g_diff, w_proj_m, w_proj_d, w_out, g_pre_ffn, g_post_ffn, w_up, conv_w,
           conv_b, w_down):
    depth = w_in.shape[0]
    B = x_prompt.shape[0]
    S = x_sample.shape[0]
    pad = (-B) % SUBLANES
    c_all = jnp.concatenate([c_prompt, jnp.zeros((pad, D_MODEL), F32), c_sample], axis=0)
    xp, xs = x_prompt, x_sample
    outs_p, outs_s = [], []
    for l in range(depth):
        ada = _ada(c_all, w_ada[l], b_ada[l].reshape(1, -1))
        pw = _prep_weights(w_in[l], b_if[l], w_proj_m[l], w_proj_d[l], w_out[l], w_up[l], conv_w[l],
                           conv_b[l], w_down[l])
        gains = (g_pre_mix[l], g_post_mix[l], g_mlstm[l], g_diff[l], g_pre_ffn[l], g_post_ffn[l])
        lam_p = jnp.stack([lambda_q1[l], lambda_k1[l], lambda_q2[l], lambda_k2[l]], axis=0)
        op, os_ = _layer(l, pw, gains, lam_p, ada[:B], ada[B + pad:], xp, xs, cache_k, cache_v, page_table,
                         state_C, state_n, state_m, state_conv, tm_p=512, l_chunk=128,
                         tq=min(1024, xp.shape[1] // 2))
        xp, xs = op[0], os_[0]
        outs_p.append(op[1:])
        outs_s.append(os_[1:])
    stack = lambda outs, i: jnp.stack([o[i] for o in outs])
    return ((xp, xs) + tuple(stack(outs_p, i) for i in range(6)) + tuple(stack(outs_s, i) for i in range(6)))
```
